```python
import math
import jax, jax.numpy as jnp
from jax import lax
import numpy as np

D_MODEL = 2048
BATCH = 4
SEQ = 2048
DEPTH = 4
DEC_BATCH = 128
DEC_SEQ = 1
PAST_LEN = 16384
PAGE_SIZE = 128

MIX_WIDTH = D_MODEL
DN_WIDTH = MIX_WIDTH // 2
DN_HEADS = 8
DN_HEAD_DIM = DN_WIDTH // DN_HEADS
CONV_W = 4
CHUNK = 64
POOL_WIDTH = MIX_WIDTH - DN_WIDTH
POOL_WINDOWS = (2, 4, 8, 16)
N_POOL_GROUPS = len(POOL_WINDOWS)
POOL_GROUP = POOL_WIDTH // N_POOL_GROUPS
POOL_BUF = max(POOL_WINDOWS) - 1
D_FF = -(-8 * D_MODEL // (3 * 256)) * 256
IN_COLS = 4 * DN_WIDTH + 2 * DN_HEADS + POOL_WIDTH
EPS = 1e-6

kernel_name = "hymba_gdn_pool_decoder_step"


def rmsnorm(x, w):
    xf = x.astype(jnp.float32)
    y = xf * lax.rsqrt(jnp.mean(xf * xf, axis=-1, keepdims=True) + EPS)
    return (y * w.astype(jnp.float32)).astype(x.dtype)


def l2norm(x):
    xf = x.astype(jnp.float32)
    return xf * lax.rsqrt(jnp.sum(xf * xf, axis=-1, keepdims=True) + EPS)


def short_conv(buf, x, w):
    ext = jnp.concatenate([buf.astype(x.dtype), x], axis=1)
    T = x.shape[1]
    y = ext[:, 0:T] * w[0]
    for i in range(1, CONV_W):
        y = y + ext[:, i:i + T] * w[i]
    return jax.nn.silu(y), ext[:, -(CONV_W - 1):]


def gated_delta_chunked(q, k, v, g, beta, S0):
    B, T, H, K = k.shape
    n = T // CHUNK

    def to_chunks(a):
        a = a.reshape((B, n, CHUNK, H) + a.shape[3:])
        return jnp.moveaxis(a, 3, 1)

    qc, kc, vc, bc = to_chunks(q), to_chunks(k), to_chunks(v), to_chunks(beta)
    gc = jnp.cumsum(to_chunks(g), axis=-1)
    tril = jnp.tril(jnp.ones((CHUNK, CHUNK), bool))
    strict = jnp.tril(jnp.ones((CHUNK, CHUNK), bool), -1)
    diff = gc[..., :, None] - gc[..., None, :]
    decay = jnp.where(tril, jnp.exp(jnp.where(tril, diff, 0.0)), 0.0)
    kb = kc * bc[..., None]
    m = jnp.where(strict, jnp.einsum('bhncd,bhnsd->bhncs', kb, kc) * decay, 0.0)
    a_mat = m + jnp.eye(CHUNK, dtype=m.dtype)
    u = lax.linalg.triangular_solve(a_mat, vc * bc[..., None], left_side=True,
                                    lower=True, unit_diagonal=True)
    w = lax.linalg.triangular_solve(a_mat, kb * jnp.exp(gc)[..., None], left_side=True,
                                    lower=True, unit_diagonal=True)
    qk = jnp.where(tril, jnp.einsum('bhncd,bhnsd->bhncs', qc, kc) * decay, 0.0)

    def step(S, xs):
        q_i, k_i, u_i, w_i, g_i, qk_i = xs
        v_new = u_i - jnp.einsum('bhck,bhkv->bhcv', w_i, S)
        o = (jnp.einsum('bhck,bhkv->bhcv', q_i * jnp.exp(g_i)[..., None], S)
             + jnp.einsum('bhcs,bhsv->bhcv', qk_i, v_new))
        g_last = g_i[..., -1]
        k_dec = k_i * jnp.exp(g_last[..., None] - g_i)[..., None]
        S = S * jnp.exp(g_last)[..., None, None] + jnp.einsum('bhck,bhcv->bhkv', k_dec, v_new)
        return S, o

    xs = tuple(jnp.moveaxis(a, 2, 0) for a in (qc, kc, u, w, gc, qk))
    S, o = lax.scan(step, S0, xs)
    o = jnp.transpose(o, (1, 0, 3, 2, 4)).reshape(B, T, H, -1)
    return o, S


def gated_delta_recurrent(q, k, v, g, beta, S0):
    def step(S, xs):
        q_t, k_t, v_t, g_t, b_t = xs
        S = S * jnp.exp(g_t)[..., None, None]
        kv = jnp.einsum('bhk,bhkv->bhv', k_t, S)
        S = S + jnp.einsum('bhk,bhv->bhkv', k_t, (v_t - kv) * b_t[..., None])
        o = jnp.einsum('bhk,bhkv->bhv', q_t, S)
        return S, o

    xs = tuple(jnp.moveaxis(a, 1, 0) for a in (q, k, v, g, beta))
    S, o = lax.scan(step, S0, xs)
    return jnp.moveaxis(o, 0, 1), S


def multiscale_pool(buf, p, start_pos, w_pool, pool_scale):
    B, T, C = p.shape
    ext = jnp.concatenate([buf.astype(p.dtype), p], axis=1)
    extf = ext.astype(jnp.float32)
    cs = jnp.concatenate([jnp.zeros((B, 1, C), jnp.float32), jnp.cumsum(extf, axis=1)], axis=1)
    pos = start_pos + jnp.arange(T)
    means = []
    for gi, win in enumerate(POOL_WINDOWS):
        lo, hi = gi * POOL_GROUP, (gi + 1) * POOL_GROUP
        s = (cs[:, POOL_BUF + 1:POOL_BUF + 1 + T, lo:hi]
             - cs[:, POOL_BUF + 1 - win:POOL_BUF + 1 - win + T, lo:hi])
        cnt = jnp.minimum(pos + 1, win).astype(jnp.float32)
        means.append(s / cnt[None, :, None])
    d = jnp.concatenate(means, axis=-1) - extf[:, POOL_BUF:]
    d = d.reshape(B, T, N_POOL_GROUPS, POOL_GROUP).astype(p.dtype)
    y = jnp.einsum('btgc,gcd->btgd', d, w_pool).reshape(B, T, POOL_WIDTH) * pool_scale
    return y, ext[:, -POOL_BUF:]


def decoder_layer(h, conv_buf, S0, pool_buf, start_pos, chunked,
                  norm_mix, w_in, conv_w, a_log, dt_bias, dn_norm, w_pool, pool_scale,
                  w_out, norm_ffn, w_gate_up, w_down):
    B, T, _ = h.shape
    xn = rmsnorm(h, norm_mix)
    proj = xn @ w_in
    o0 = 3 * DN_WIDTH
    o1 = 4 * DN_WIDTH
    qkv = proj[..., :o0]
    z = proj[..., o0:o1]
    b_raw = proj[..., o1:o1 + DN_HEADS]
    a_raw = proj[..., o1 + DN_HEADS:o1 + 2 * DN_HEADS]
    p = proj[..., o1 + 2 * DN_HEADS:]

    qkv_c, new_conv = short_conv(conv_buf, qkv, conv_w)
    q = qkv_c[..., :DN_WIDTH].reshape(B, T, DN_HEADS, DN_HEAD_DIM)
    k = qkv_c[..., DN_WIDTH:2 * DN_WIDTH].reshape(B, T, DN_HEADS, DN_HEAD_DIM)
    v = qkv_c[..., 2 * DN_WIDTH:].reshape(B, T, DN_HEADS, DN_HEAD_DIM).astype(jnp.float32)
    q = l2norm(q) * (DN_HEAD_DIM ** -0.5)
    k = l2norm(k)
    beta = jax.nn.sigmoid(b_raw.astype(jnp.float32))
    g = -jnp.exp(a_log.astype(jnp.float32)) * jax.nn.softplus(
        a_raw.astype(jnp.float32) + dt_bias.astype(jnp.float32))
    S0f = S0.astype(jnp.float32)
    if chunked:
        o, S = gated_delta_chunked(q, k, v, g, beta, S0f)
    else:
        o, S = gated_delta_recurrent(q, k, v, g, beta, S0f)
    zf = z.astype(jnp.float32).reshape(B, T, DN_HEADS, DN_HEAD_DIM)
    dn_out = (rmsnorm(o, dn_norm) * jax.nn.silu(zf)).reshape(B, T, DN_WIDTH).astype(h.dtype)

    pool_out, new_pool = multiscale_pool(pool_buf, p, start_pos, w_pool, pool_scale)

    h = h + jnp.concatenate([dn_out, pool_out.astype(h.dtype)], axis=-1) @ w_out

    xn2 = rmsnorm(h, norm_ffn)
    gu = xn2 @ w_gate_up
    h = h + (jax.nn.silu(gu[..., :D_FF]) * gu[..., D_FF:]) @ w_down
    return h, S.astype(h.dtype), new_conv, new_pool


def setup_inputs(seed: int = 0) -> dict:
    key = jax.random.key(seed)
    ks = jax.random.split(key, 20)
    f32 = jnp.float32

    def nrm(k, shape, scale):
        return jax.random.normal(k, shape, f32) * scale

    x_prompt = nrm(ks[0], (BATCH, SEQ, D_MODEL), 1.0)
    x_sample = nrm(ks[1], (DEC_BATCH, DEC_SEQ, D_MODEL), 1.0)
    state_delta = nrm(ks[2], (DEPTH, DEC_BATCH, DN_HEADS, DN_HEAD_DIM, DN_HEAD_DIM), DN_HEAD_DIM ** -0.5)
    state_conv = nrm(ks[3], (DEPTH, DEC_BATCH, CONV_W - 1, 3 * DN_WIDTH), 1.0)
    state_pool = nrm(ks[4], (DEPTH, DEC_BATCH, POOL_BUF, POOL_WIDTH), 1.0)
    norm_mix = 1.0 + nrm(ks[5], (DEPTH, D_MODEL), 0.02)
    w_in = nrm(ks[6], (DEPTH, D_MODEL, IN_COLS), D_MODEL ** -0.5)
    conv_w = nrm(ks[7], (DEPTH, CONV_W, 3 * DN_WIDTH), CONV_W ** -0.5)
    a_log = jnp.log(jax.random.uniform(ks[8], (DEPTH, DN_HEADS), f32, 1.0, 16.0))
    dt = jnp.exp(jax.random.uniform(ks[9], (DEPTH, DN_HEADS), f32, math.log(1e-3), math.log(1e-1)))
    dt_bias = dt + jnp.log(-jnp.expm1(-dt))
    dn_norm = 1.0 + nrm(ks[10], (DEPTH, DN_HEAD_DIM), 0.02)
    w_pool = nrm(ks[11], (DEPTH, N_POOL_GROUPS, POOL_GROUP, POOL_GROUP), POOL_GROUP ** -0.5)
    pool_scale = 1.0 + nrm(ks[12], (DEPTH, POOL_WIDTH), 0.1)
    w_out = nrm(ks[13], (DEPTH, MIX_WIDTH, D_MODEL), (2 * DEPTH * MIX_WIDTH) ** -0.5)
    norm_ffn = 1.0 + nrm(ks[14], (DEPTH, D_MODEL), 0.02)
    w_gate_up = nrm(ks[15], (DEPTH, D_MODEL, 2 * D_FF), D_MODEL ** -0.5)
    w_down = nrm(ks[16], (DEPTH, D_FF, D_MODEL), (2 * DEPTH * D_FF) ** -0.5)
    norm_final = 1.0 + nrm(ks[17], (D_MODEL,), 0.02)
    return {"x_prompt": x_prompt, "x_sample": x_sample,
            "state_delta": state_delta, "state_conv": state_conv, "state_pool": state_pool,
            "norm_mix": norm_mix, "w_in": w_in, "conv_w": conv_w, "a_log": a_log,
            "dt_bias": dt_bias, "dn_norm": dn_norm, "w_pool": w_pool, "pool_scale": pool_scale,
            "w_out": w_out, "norm_ffn": norm_ffn, "w_gate_up": w_gate_up, "w_down": w_down,
            "norm_final": norm_final}


def reference(x_prompt, x_sample, state_delta, state_conv, state_pool, norm_mix, w_in, conv_w,
              a_log, dt_bias, dn_norm, w_pool, pool_scale, w_out, norm_ffn, w_gate_up, w_down,
              norm_final):
    hp, hs = x_prompt, x_sample
    dtp = x_prompt.dtype
    zero_S = jnp.zeros((BATCH, DN_HEADS, DN_HEAD_DIM, DN_HEAD_DIM), jnp.float32)
    zero_conv = jnp.zeros((BATCH, CONV_W - 1, 3 * DN_WIDTH), dtp)
    zero_pool = jnp.zeros((BATCH, POOL_BUF, POOL_WIDTH), dtp)
    dp, cp, pp, ds, cs, ps = [], [], [], [], [], []
    for l in range(DEPTH):
        lw = (norm_mix[l], w_in[l], conv_w[l], a_log[l], dt_bias[l], dn_norm[l], w_pool[l],
              pool_scale[l], w_out[l], norm_ffn[l], w_gate_up[l], w_down[l])
        hp, S_p, c_p, p_p = decoder_layer(hp, zero_conv, zero_S, zero_pool, 0, True, *lw)
        hs, S_s, c_s, p_s = decoder_layer(hs, state_conv[l], state_delta[l], state_pool[l],
                                          PAST_LEN, False, *lw)
        dp.append(S_p); cp.append(c_p); pp.append(p_p)
        ds.append(S_s); cs.append(c_s); ps.append(p_s)
    y_prompt = rmsnorm(hp, norm_final)
    y_sample = rmsnorm(hs, norm_final)
    return (y_prompt, y_sample, jnp.stack(dp), jnp.stack(cp), jnp.stack(pp),
            jnp.stack(ds), jnp.stack(cs), jnp.stack(ps))
```

```python
import functools

import jax
import jax.numpy as jnp
from jax import lax
from jax.experimental import pallas as pl
from jax.experimental.pallas import tpu as pltpu

F32 = jnp.float32
BF16 = jnp.bfloat16

D_MODEL = 2048
DN_WIDTH = 1024
DN_HEADS = 8
HEAD_DIM = 128
CONV_W = 4
CHUNK = 64
POOL_WIDTH = 1024
POOL_WINDOWS = (2, 4, 8, 16)
POOL_GROUP = 256
POOL_BUF = 15
D_FF = 5632
EPS = 1e-6
PAST_LEN = 16384

MAIN_COLS = 4 * DN_WIDTH + POOL_WIDTH
GATE_COLS = 256
LANES = 128
SUBLANES = 8

ROW_TILE = 640
INPROJ_COL_TILE = 512
OUTPROJ_COL_TILE = 1024
FFN_TILE = 512
CONV_ROWS = 256
POOL_ROWS = 512
SAMPLE_BLOCK = 16
POOL_SAMPLE_BLOCK = 32
MIB = 1024 * 1024


def _params(semantics, vmem_mib):
    return pltpu.CompilerParams(dimension_semantics=semantics, vmem_limit_bytes=vmem_mib * MIB)


def _silu(x):
    return x * jax.nn.sigmoid(x)


def _softplus(x):
    return jnp.maximum(x, 0.0) + jnp.log1p(jnp.exp(-jnp.abs(x)))


def _rms(x, w):
    return x * lax.rsqrt(jnp.mean(x * x, axis=-1, keepdims=True) + EPS) * w


def _l2(x):
    return x * lax.rsqrt(jnp.sum(x * x, axis=-1, keepdims=True) + EPS)


def _dot(a, b, **kw):
    return jnp.dot(a, b, preferred_element_type=F32, **kw)


def _inproj_body(h_ref, nw_ref, w_ref, wg_ref, proj_ref, gate_ref, xn_ref):
    @pl.when(pl.program_id(1) == 0)
    def _():
        xn = _rms(h_ref[...], nw_ref[...]).astype(BF16)
        xn_ref[...] = xn
        gate_ref[...] = _dot(xn, wg_ref[...])

    proj_ref[...] = _dot(xn_ref[...], w_ref[...])


def _inproj(h, nw, w_main, w_gate):
    m = h.shape[0]
    tm, tn = ROW_TILE, INPROJ_COL_TILE
    return pl.pallas_call(
        _inproj_body,
        grid=(m // tm, MAIN_COLS // tn),
        in_specs=[
            pl.BlockSpec((tm, D_MODEL), lambda i, j: (i, 0)),
            pl.BlockSpec((1, D_MODEL), lambda i, j: (0, 0)),
            pl.BlockSpec((D_MODEL, tn), lambda i, j: (0, j)),
            pl.BlockSpec((D_MODEL, GATE_COLS), lambda i, j: (0, 0)),
        ],
        out_specs=[
            pl.BlockSpec((tm, tn), lambda i, j: (i, j)),
            pl.BlockSpec((tm, GATE_COLS), lambda i, j: (i, 0)),
        ],
        out_shape=[
            jax.ShapeDtypeStruct((m, MAIN_COLS), F32),
            jax.ShapeDtypeStruct((m, GATE_COLS), F32),
        ],
        scratch_shapes=[pltpu.VMEM((tm, D_MODEL), BF16)],
        compiler_params=_params(("parallel", "arbitrary"), 48),
        name="inproj",
    )(h, nw, w_main, w_gate)


def _gdn_prompt_body(q_ref, k_ref, v_ref, z_ref, gate_ref, cwq_ref, cwk_ref, cwv_ref,
                     alog_ref, dtb_ref, dnw_ref,
                     o_ref, s_out_ref,
                     qs, ks, vs, beta_s, g_s, wq_s, u_s, kd_s, qk_s, gl_s, o_s, st_s, *, seq):
    head = pl.program_id(1)
    rb = min(CONV_ROWS, seq)
    n_chunks = seq // CHUNK

    def conv_silu(x_ref, cw_ref, r):
        start = pl.multiple_of(r * rb, rb)
        cur = x_ref[pl.ds(start, rb), :]
        pstart = pl.multiple_of(jnp.maximum(start - SUBLANES, 0), SUBLANES)
        prev = jnp.where(r > 0, x_ref[pl.ds(pstart, SUBLANES), :], 0.0)
        ext = jnp.concatenate([prev, cur], axis=0)
        cw = cw_ref[...]
        off = SUBLANES - (CONV_W - 1)
        y = ext[off:off + rb] * cw[0:1]
        for i in range(1, CONV_W):
            y = y + ext[off + i:off + i + rb] * cw[i:i + 1]
        return _silu(y)

    def prep(r, carry):
        rows = pl.ds(pl.multiple_of(r * rb, rb), rb)
        qs[rows, :] = _l2(conv_silu(q_ref, cwq_ref, r)) * (HEAD_DIM ** -0.5)
        ks[rows, :] = _l2(conv_silu(k_ref, cwk_ref, r))
        vs[rows, :] = conv_silu(v_ref, cwv_ref, r)
        beta_s[rows, :] = jax.nn.sigmoid(gate_ref[rows, 0:LANES])
        g_s[rows, :] = -jnp.exp(alog_ref[...]) * _softplus(gate_ref[rows, LANES:2 * LANES] + dtb_ref[...])
        return carry

    lax.fori_loop(0, seq // rb, prep, 0)

    head_lane = lax.broadcasted_iota(jnp.int32, (CHUNK, LANES), 1) == head
    ri = lax.broadcasted_iota(jnp.int32, (CHUNK, CHUNK), 0)
    ci = lax.broadcasted_iota(jnp.int32, (CHUNK, CHUNK), 1)
    tril = ri >= ci
    strict = ri > ci
    tril_f = tril.astype(F32)
    eye = (ri == ci).astype(F32)
    ones = jnp.ones((CHUNK, CHUNK), F32)

    def chunk_prep(c, carry):
        rows = pl.ds(pl.multiple_of(c * CHUNK, CHUNK), CHUNK)
        q, k, v = qs[rows, :], ks[rows, :], vs[rows, :]
        bcol = jnp.sum(jnp.where(head_lane, beta_s[rows, :], 0.0), axis=1, keepdims=True)
        gcol = jnp.sum(jnp.where(head_lane, g_s[rows, :], 0.0), axis=1, keepdims=True)
        gc = _dot(tril_f, jnp.broadcast_to(gcol, (CHUNK, LANES)), precision=lax.Precision.HIGHEST)
        gcs = gc[:, :CHUNK]
        grow = _dot(ones, gcs * eye, precision=lax.Precision.HIGHEST)
        decay = jnp.where(tril, jnp.exp(jnp.where(tril, gcs - grow, 0.0)), 0.0)
        kb = k * bcol
        a1 = lax.dot_general(jnp.concatenate([q, kb], axis=0), k, (((1,), (1,)), ((), ())),
                             preferred_element_type=F32)
        qk = jnp.where(tril, a1[:CHUNK] * decay, 0.0)
        m = jnp.where(strict, a1[CHUNK:] * decay, 0.0)
        inv = eye - m
        mp = m
        sh = 1
        while 2 * sh < CHUNK:
            sh *= 2
            mp = _dot(mp, mp)
            inv = inv + _dot(inv, mp)
        eg = jnp.exp(gc)
        uw = _dot(inv, jnp.concatenate([v * bcol, kb * eg], axis=1))
        u_s[rows, :] = uw[:, :HEAD_DIM]
        wq_s[c, 0:CHUNK, :] = uw[:, HEAD_DIM:]
        wq_s[c, CHUNK:2 * CHUNK, :] = q * eg
        glast = gc[CHUNK - 1:CHUNK, :]
        kd_s[rows, :] = k * jnp.exp(glast - gc)
        qk_s[c] = qk
        gl_s[c] = jnp.broadcast_to(jnp.exp(glast), (SUBLANES, LANES))
        return carry

    lax.fori_loop(0, n_chunks, chunk_prep, 0)

    st_s[...] = jnp.zeros((HEAD_DIM, HEAD_DIM), F32)

    def chunk_scan(c, carry):
        rows = pl.ds(pl.multiple_of(c * CHUNK, CHUNK), CHUNK)
        st = st_s[...]
        r = _dot(wq_s[c], st)
        vnew = u_s[rows, :] - r[:CHUNK]
        o_s[rows, :] = r[CHUNK:] + _dot(qk_s[c], vnew)
        st_s[...] = st * gl_s[c][0:1, :] + lax.dot_general(
            kd_s[rows, :], vnew, (((0,), (0,)), ((), ())), preferred_element_type=F32)
        return carry

    lax.fori_loop(0, n_chunks, chunk_scan, 0)
    s_out_ref[0, 0] = st_s[...]

    def finish(r, carry):
        rows = pl.ds(pl.multiple_of(r * rb, rb), rb)
        o_ref[rows, :] = (_rms(o_s[rows, :], dnw_ref[...]) * _silu(z_ref[rows, :])).astype(BF16)
        return carry

    lax.fori_loop(0, seq // rb, finish, 0)


def _gdn_prompt(proj, gate, conv_w, alog, dtb, dnw, batch, seq):
    m = proj.shape[0]
    nh = DN_HEADS
    n_chunks = seq // CHUNK
    col = lambda base: (lambda b, h: (b, base + h))
    cwcol = lambda base: (lambda b, h: (0, base + h))
    vec = pl.BlockSpec((1, LANES), lambda b, h: (0, 0))
    return pl.pallas_call(
        functools.partial(_gdn_prompt_body, seq=seq),
        grid=(batch, nh),
        in_specs=[
            pl.BlockSpec((seq, HEAD_DIM), col(0)),
            pl.BlockSpec((seq, HEAD_DIM), col(nh)),
            pl.BlockSpec((seq, HEAD_DIM), col(2 * nh)),
            pl.BlockSpec((seq, HEAD_DIM), col(3 * nh)),
            pl.BlockSpec((seq, GATE_COLS), lambda b, h: (b, 0)),
            pl.BlockSpec((CONV_W, HEAD_DIM), cwcol(0)),
            pl.BlockSpec((CONV_W, HEAD_DIM), cwcol(nh)),
            pl.BlockSpec((CONV_W, HEAD_DIM), cwcol(2 * nh)),
            vec, vec, vec,
        ],
        out_specs=[
            pl.BlockSpec((seq, HEAD_DIM), lambda b, h: (b, h)),
            pl.BlockSpec((1, 1, HEAD_DIM, HEAD_DIM), lambda b, h: (b, h, 0, 0)),
        ],
        out_shape=[
            jax.ShapeDtypeStruct((m, DN_WIDTH), BF16),
            jax.ShapeDtypeStruct((batch, nh, HEAD_DIM, HEAD_DIM), F32),
        ],
        scratch_shapes=[
            pltpu.VMEM((seq, HEAD_DIM), F32),
            pltpu.VMEM((seq, HEAD_DIM), F32),
            pltpu.VMEM((seq, HEAD_DIM), F32),
            pltpu.VMEM((seq, LANES), F32),
            pltpu.VMEM((seq, LANES), F32),
            pltpu.VMEM((n_chunks, 2 * CHUNK, HEAD_DIM), F32),
            pltpu.VMEM((seq, HEAD_DIM), F32),
            pltpu.VMEM((seq, HEAD_DIM), F32),
            pltpu.VMEM((n_chunks, CHUNK, CHUNK), F32),
            pltpu.VMEM((n_chunks, SUBLANES, LANES), F32),
            pltpu.VMEM((seq, HEAD_DIM), F32),
            pltpu.VMEM((HEAD_DIM, HEAD_DIM), F32),
        ],
        compiler_params=_params(("parallel", "parallel"), 40),
        name="gdn_prompt",
    )(proj, proj, proj, proj, gate, conv_w, conv_w, conv_w, alog, dtb, dnw)


def _gdn_sample_body(qkv_ref, z_ref, gate_ref, cs_ref, s_ref, cw_ref, alog_ref, dtb_ref, dnw_ref, dn_any,
                     dn_ref, sn_ref, cn_ref, q_s, k_s, v_s, eg_s, beta_s, o_s):
    del dn_any
    nb = qkv_ref.shape[0]
    x = qkv_ref[...]
    c0, c1, c2 = cs_ref[:, 0, :], cs_ref[:, 1, :], cs_ref[:, 2, :]
    cw = cw_ref[...]
    y = c0 * cw[0:1] + c1 * cw[1:2] + c2 * cw[2:3] + x * cw[3:4]
    y = _silu(y)
    cn_ref[:, 0, :] = c1
    cn_ref[:, 1, :] = c2
    cn_ref[:, 2, :] = x
    for h in range(DN_HEADS):
        lo = h * HEAD_DIM
        q_s[h] = _l2(y[:, lo:lo + HEAD_DIM]) * (HEAD_DIM ** -0.5)
        k_s[h] = _l2(y[:, DN_WIDTH + lo:DN_WIDTH + lo + HEAD_DIM])
        v_s[h] = y[:, 2 * DN_WIDTH + lo:2 * DN_WIDTH + lo + HEAD_DIM]
    beta_s[...] = jax.nn.sigmoid(gate_ref[:, 0:LANES])
    eg_s[...] = jnp.exp(-jnp.exp(alog_ref[...]) * _softplus(gate_ref[:, LANES:2 * LANES] + dtb_ref[...]))

    pad7 = jnp.zeros((SUBLANES - 1, HEAD_DIM), F32)
    pad6 = jnp.zeros((SUBLANES - 2, HEAD_DIM), F32)

    def per_sample(b, carry):
        one = pl.ds(b, 1)
        eg_row = eg_s[one, :]
        beta_row = beta_s[one, :]
        for h in range(DN_HEADS):
            k = k_s[h, one, :]
            q = q_s[h, one, :]
            v = v_s[h, one, :]
            st = s_ref[b, h] * eg_row[:, h:h + 1]
            r = _dot(jnp.concatenate([k, q, pad6], axis=0), st)
            dv = (v - r[0:1]) * beta_row[:, h:h + 1]
            outer = lax.dot_general(jnp.concatenate([k, pad7], axis=0), jnp.concatenate([dv, pad7], axis=0),
                                    (((0,), (0,)), ((), ())), preferred_element_type=F32)
            sn_ref[b, h] = st + outer
            o_s[h, one, :] = r[1:2] + jnp.sum(q * k, axis=-1, keepdims=True) * dv
        return carry

    lax.fori_loop(0, nb, per_sample, 0)

    z = z_ref[...]
    for h in range(DN_HEADS):
        lo = h * HEAD_DIM
        dn_ref[:, lo:lo + HEAD_DIM] = (_rms(o_s[h], dnw_ref[...]) * _silu(z[:, lo:lo + HEAD_DIM])).astype(BF16)


def _gdn_sample(proj, gate, conv_state, delta_state, conv_w, alog, dtb, dnw, dn_buf, row0):
    nseq = delta_state.shape[0]
    nb = SAMPLE_BLOCK
    base = row0 // nb
    vec = pl.BlockSpec((1, LANES), lambda i: (0, 0))
    return pl.pallas_call(
        _gdn_sample_body,
        grid=(nseq // nb,),
        in_specs=[
            pl.BlockSpec((nb, 3 * DN_WIDTH), lambda i: (base + i, 0)),
            pl.BlockSpec((nb, DN_WIDTH), lambda i: (base + i, 3)),
            pl.BlockSpec((nb, GATE_COLS), lambda i: (base + i, 0)),
            pl.BlockSpec((nb, CONV_W - 1, 3 * DN_WIDTH), lambda i: (i, 0, 0)),
            pl.BlockSpec((nb, DN_HEADS, HEAD_DIM, HEAD_DIM), lambda i: (i, 0, 0, 0)),
            pl.BlockSpec((CONV_W, 3 * DN_WIDTH), lambda i: (0, 0)),
            vec, vec, vec,
            pl.BlockSpec(memory_space=pl.ANY),
        ],
        out_specs=[
            pl.BlockSpec((nb, DN_WIDTH), lambda i: (base + i, 0)),
            pl.BlockSpec((nb, DN_HEADS, HEAD_DIM, HEAD_DIM), lambda i: (i, 0, 0, 0)),
            pl.BlockSpec((nb, CONV_W - 1, 3 * DN_WIDTH), lambda i: (i, 0, 0)),
        ],
        out_shape=[
            jax.ShapeDtypeStruct(dn_buf.shape, dn_buf.dtype),
            jax.ShapeDtypeStruct(delta_state.shape, F32),
            jax.ShapeDtypeStruct(conv_state.shape, F32),
        ],
        scratch_shapes=[
            pltpu.VMEM((DN_HEADS, nb, HEAD_DIM), F32),
            pltpu.VMEM((DN_HEADS, nb, HEAD_DIM), F32),
            pltpu.VMEM((DN_HEADS, nb, HEAD_DIM), F32),
            pltpu.VMEM((nb, LANES), F32),
            pltpu.VMEM((nb, LANES), F32),
            pltpu.VMEM((DN_HEADS, nb, HEAD_DIM), F32),
        ],
        input_output_aliases={9: 0},
        compiler_params=_params(("parallel",), 48),
        name="gdn_sample",
    )(proj, proj, gate, conv_state, delta_state, conv_w, alog, dtb, dnw, dn_buf)


def _pool_matmul(d, w_ref, sc_ref, o_ref, gi):
    lo = gi * POOL_GROUP
    y = _dot(d.astype(BF16), w_ref[gi]) * sc_ref[:, lo:lo + POOL_GROUP]
    o_ref[:, lo:lo + POOL_GROUP] = y.astype(BF16)


def _pool_prompt_body(p_ref, prev_ref, w_ref, sc_ref, o_ref, *, tiles_per_seq):
    rt = p_ref.shape[0]
    hist = prev_ref.shape[0]
    t = pl.program_id(0) % tiles_per_seq
    cur = p_ref[...]
    prev = jnp.where(t == 0, 0.0, prev_ref[...])
    pos = t * rt + lax.broadcasted_iota(jnp.int32, (rt, POOL_GROUP), 0)
    for gi, win in enumerate(POOL_WINDOWS):
        lo = gi * POOL_GROUP
        x = cur[:, lo:lo + POOL_GROUP]
        s = jnp.concatenate([prev[:, lo:lo + POOL_GROUP], x], axis=0)
        sh = 1
        while sh < win:
            s = s + pltpu.roll(s, sh, axis=0)
            sh *= 2
        cnt = jnp.minimum(pos + 1, win).astype(F32)
        _pool_matmul(s[hist:] / cnt - x, w_ref, sc_ref, o_ref, gi)


def _pool_prompt(proj, w_pool, scale, batch, seq):
    m = proj.shape[0]
    rt = min(POOL_ROWS, seq)
    hist = POOL_BUF + 1
    tiles_per_seq = seq // rt
    pcol = (4 * DN_WIDTH) // POOL_WIDTH
    return pl.pallas_call(
        functools.partial(_pool_prompt_body, tiles_per_seq=tiles_per_seq),
        grid=(batch * tiles_per_seq,),
        in_specs=[
            pl.BlockSpec((rt, POOL_WIDTH), lambda r: (r, pcol)),
            pl.BlockSpec((hist, POOL_WIDTH), lambda r: (jnp.maximum(r * (rt // hist) - 1, 0), pcol)),
            pl.BlockSpec((len(POOL_WINDOWS), POOL_GROUP, POOL_GROUP), lambda r: (0, 0, 0)),
            pl.BlockSpec((1, POOL_WIDTH), lambda r: (0, 0)),
        ],
        out_specs=pl.BlockSpec((rt, POOL_WIDTH), lambda r: (r, 0)),
        out_shape=jax.ShapeDtypeStruct((m, POOL_WIDTH), BF16),
        compiler_params=_params(("parallel",), 32),
        name="pool_prompt",
    )(proj, proj, w_pool, scale)


def _pool_sample_body(p_ref, buf_ref, w_ref, sc_ref, pool_any, o_ref, nb_ref):
    del pool_any
    cur = p_ref[...]
    for gi, win in enumerate(POOL_WINDOWS):
        lo = gi * POOL_GROUP
        x = cur[:, lo:lo + POOL_GROUP]
        s = x
        for j in range(1, win):
            s = s + buf_ref[:, POOL_BUF - j, lo:lo + POOL_GROUP]
        cnt = float(min(PAST_LEN + 1, win))
        _pool_matmul(s / cnt - x, w_ref, sc_ref, o_ref, gi)
    for j in range(POOL_BUF - 1):
        nb_ref[:, j, :] = buf_ref[:, j + 1, :]
    nb_ref[:, POOL_BUF - 1, :] = cur


def _pool_sample(proj, pool_state, w_pool, scale, pool_buf, row0):
    nseq = pool_state.shape[0]
    nb = POOL_SAMPLE_BLOCK
    base = row0 // nb
    pcol = (4 * DN_WIDTH) // POOL_WIDTH
    return pl.pallas_call(
        _pool_sample_body,
        grid=(nseq // nb,),
        in_specs=[
            pl.BlockSpec((nb, POOL_WIDTH), lambda i: (base + i, pcol)),
            pl.BlockSpec((nb, POOL_BUF, POOL_WIDTH), lambda i: (i, 0, 0)),
            pl.BlockSpec((len(POOL_WINDOWS), POOL_GROUP, POOL_GROUP), lambda i: (0, 0, 0)),
            pl.BlockSpec((1, POOL_WIDTH), lambda i: (0, 0)),
            pl.BlockSpec(memory_space=pl.ANY),
        ],
        out_specs=[
            pl.BlockSpec((nb, POOL_WIDTH), lambda i: (base + i, 0)),
            pl.BlockSpec((nb, POOL_BUF, POOL_WIDTH), lambda i: (i, 0, 0)),
        ],
        out_shape=[
            jax.ShapeDtypeStruct(pool_buf.shape, pool_buf.dtype),
            jax.ShapeDtypeStruct(pool_state.shape, F32),
        ],
        input_output_aliases={4: 0},
        compiler_params=_params(("parallel",), 32),
        name="pool_sample",
    )(proj, pool_state, w_pool, scale, pool_buf)


def _outproj_body(dn_ref, pool_ref, w1_ref, w2_ref, h_ref, o_ref):
    o_ref[...] = h_ref[...] + _dot(dn_ref[...], w1_ref[...]) + _dot(pool_ref[...], w2_ref[...])


def _outproj(dn, pool, w_out, h):
    m = h.shape[0]
    tm, tn = ROW_TILE, OUTPROJ_COL_TILE
    return pl.pallas_call(
        _outproj_body,
        grid=(m // tm, D_MODEL // tn),
        in_specs=[
            pl.BlockSpec((tm, DN_WIDTH), lambda i, j: (i, 0)),
            pl.BlockSpec((tm, POOL_WIDTH), lambda i, j: (i, 0)),
            pl.BlockSpec((DN_WIDTH, tn), lambda i, j: (0, j)),
            pl.BlockSpec((POOL_WIDTH, tn), lambda i, j: (1, j)),
            pl.BlockSpec((tm, tn), lambda i, j: (i, j)),
        ],
        out_specs=pl.BlockSpec((tm, tn), lambda i, j: (i, j)),
        out_shape=jax.ShapeDtypeStruct((m, D_MODEL), F32),
        compiler_params=_params(("parallel", "parallel"), 40),
        name="outproj",
    )(dn, pool, w_out, w_out, h)


def _ffn_body(h_ref, nw_ref, wg_ref, wu_ref, wd_ref, nf_ref, o_ref, xn_ref, *, final_norm):
    f = pl.program_id(1)

    @pl.when(f == 0)
    def _():
        x = h_ref[...]
        xn_ref[...] = _rms(x, nw_ref[...]).astype(BF16)
        o_ref[...] = x

    xn = xn_ref[...]
    act = (_silu(_dot(xn, wg_ref[...])) * _dot(xn, wu_ref[...])).astype(BF16)
    o_ref[...] += _dot(act, wd_ref[...])

    if final_norm:
        @pl.when(f == pl.num_programs(1) - 1)
        def _():
            o_ref[...] = _rms(o_ref[...], nf_ref[...])


def _ffn(h, nw, w_gate_up, w_down, nf, final_norm):
    m = h.shape[0]
    tm, tf = ROW_TILE, FFN_TILE
    nft = D_FF // tf
    return pl.pallas_call(
        functools.partial(_ffn_body, final_norm=final_norm),
        grid=(m // tm, nft),
        in_specs=[
            pl.BlockSpec((tm, D_MODEL), lambda i, f: (i, 0)),
            pl.BlockSpec((1, D_MODEL), lambda i, f: (0, 0)),
            pl.BlockSpec((D_MODEL, tf), lambda i, f: (0, f)),
            pl.BlockSpec((D_MODEL, tf), lambda i, f: (0, nft + f)),
            pl.BlockSpec((tf, D_MODEL), lambda i, f: (f, 0)),
            pl.BlockSpec((1, D_MODEL), lambda i, f: (0, 0)),
        ],
        out_specs=pl.BlockSpec((tm, D_MODEL), lambda i, f: (i, 0)),
        out_shape=jax.ShapeDtypeStruct((m, D_MODEL), F32),
        scratch_shapes=[pltpu.VMEM((tm, D_MODEL), BF16)],
        compiler_params=_params(("parallel", "arbitrary"), 52),
        name="ffn",
    )(h, nw, w_gate_up, w_gate_up, w_down, nf)


def _lane_vec(x):
    return jnp.zeros((1, LANES), F32).at[0, :x.shape[0]].set(x)


def kernel(x_prompt, x_sample, state_delta, state_conv, state_pool, norm_mix, w_in, conv_w, a_log, dt_bias,
           dn_norm, w_pool, pool_scale, w_out, norm_ffn, w_gate_up, w_down, norm_final):
    batch, seq, _ = x_prompt.shape
    nseq = x_sample.shape[0]
    depth = w_in.shape[0]
    rows_p = batch * seq
    h = jnp.concatenate([x_prompt.reshape(rows_p, D_MODEL), x_sample.reshape(nseq, D_MODEL)], axis=0)

    o1 = 4 * DN_WIDTH
    w_main = jnp.concatenate([w_in[:, :, :o1], w_in[:, :, o1 + 2 * DN_HEADS:]], axis=-1).astype(BF16)
    w_gate = jnp.zeros((depth, D_MODEL, GATE_COLS), F32)
    w_gate = w_gate.at[:, :, :DN_HEADS].set(w_in[:, :, o1:o1 + DN_HEADS])
    w_gate = w_gate.at[:, :, LANES:LANES + DN_HEADS].set(w_in[:, :, o1 + DN_HEADS:o1 + 2 * DN_HEADS])
    w_gate = w_gate.astype(BF16)
    w_out_b = w_out.astype(BF16)
    w_gu_b = w_gate_up.astype(BF16)
    w_down_b = w_down.astype(BF16)
    w_pool_b = w_pool.astype(BF16)

    delta_p, conv_p, pool_p, delta_s, conv_s, pool_s = [], [], [], [], [], []
    for l in range(depth):
        alog, dtb, dnw = _lane_vec(a_log[l]), _lane_vec(dt_bias[l]), dn_norm[l].reshape(1, HEAD_DIM)
        scale = pool_scale[l].reshape(1, POOL_WIDTH)
        proj, gate = _inproj(h, norm_mix[l].reshape(1, D_MODEL), w_main[l], w_gate[l])

        dn, s_p = _gdn_prompt(proj, gate, conv_w[l], alog, dtb, dnw, batch, seq)
        dn, s_s, c_s = _gdn_sample(proj, gate, state_conv[l], state_delta[l], conv_w[l], alog, dtb, dnw,
                                   dn, rows_p)
        pool = _pool_prompt(proj, w_pool_b[l], scale, batch, seq)
        pool, p_s = _pool_sample(proj, state_pool[l], w_pool_b[l], scale, pool, rows_p)

        proj_p = proj[:rows_p].reshape(batch, seq, MAIN_COLS)
        delta_p.append(s_p)
        conv_p.append(proj_p[:, seq - (CONV_W - 1):, :3 * DN_WIDTH])
        pool_p.append(proj_p[:, seq - POOL_BUF:, o1:])
        delta_s.append(s_s)
        conv_s.append(c_s)
        pool_s.append(p_s)

        h = _outproj(dn, pool, w_out_b[l], h)
        h = _ffn(h, norm_ffn[l].reshape(1, D_MODEL), w_gu_b[l], w_down_b[l], norm_final.reshape(1, D_MODEL),
                 l == depth - 1)

    y_prompt = h[:rows_p].reshape(batch, seq, D_MODEL)
    y_sample = h[rows_p:].reshape(nseq, 1, D_MODEL)
    return (y_prompt, y_sample, jnp.stack(delta_p), jnp.stack(conv_p), jnp.stack(pool_p),
            jnp.stack(delta_s), jnp.stack(conv_s), jnp.stack(pool_s))
```

```python
import functools

import jax
import jax.numpy as jnp
from jax import lax
from jax.experimental import pallas as pl
from jax.experimental.pallas import tpu as pltpu

F32 = jnp.float32
BF16 = jnp.bfloat16

D_MODEL = 2048
DN_WIDTH = 1024
DN_HEADS = 8
HEAD_DIM = 128
CONV_W = 4
CHUNK = 64
POOL_WIDTH = 1024
POOL_WINDOWS = (2, 4, 8, 16)
POOL_GROUP = 256
POOL_BUF = 15
D_FF = 5632
EPS = 1e-6
PAST_LEN = 16384

MAIN_COLS = 4 * DN_WIDTH + POOL_WIDTH
GATE_COLS = 256
LANES = 128
SUBLANES = 8
PAIR = 2 * CHUNK

ROW_TILES = (640, 512, 256, 128)
INPROJ_COL_TILE = 512
OUTPROJ_COL_TILE = 1024
FFN_TILE = 512
CONV_ROWS = 256
PREP_ROWS = 1024
PREP_UNROLL = 4
SCAN_HEADS = 4
POOL_ROWS = 512
SAMPLE_BLOCK = 16
POOL_SAMPLE_BLOCK = 32
MIB = 1024 * 1024


def _params(semantics, vmem_mib):
    return pltpu.CompilerParams(dimension_semantics=semantics, vmem_limit_bytes=vmem_mib * MIB)


def _silu(x):
    return x * jax.nn.sigmoid(x)


def _softplus(x):
    return jnp.maximum(x, 0.0) + jnp.log1p(jnp.exp(-jnp.abs(x)))


def _rms(x, w):
    return x * lax.rsqrt(jnp.mean(x * x, axis=-1, keepdims=True) + EPS) * w


def _l2(x):
    return x * lax.rsqrt(jnp.sum(x * x, axis=-1, keepdims=True) + EPS)


def _dot(a, b):
    return jnp.dot(a, b, preferred_element_type=F32)


def _row_tile(m):
    return next(t for t in ROW_TILES if m % t == 0)


def _any_spec():
    return pl.BlockSpec(memory_space=pl.ANY)


def _inproj_body(h_ref, nw_ref, wa_ref, wb_ref, wg_ref, proj_ref, gate_ref, xn_ref, *, n_a):
    j = pl.program_id(1)

    @pl.when(j == 0)
    def _():
        xn = _rms(h_ref[...], nw_ref[...]).astype(BF16)
        xn_ref[...] = xn
        gate_ref[...] = _dot(xn, wg_ref[...])

    @pl.when(j < n_a)
    def _():
        proj_ref[...] = _dot(xn_ref[...], wa_ref[...])

    @pl.when(j >= n_a)
    def _():
        proj_ref[...] = _dot(xn_ref[...], wb_ref[...])


def _inproj(h, nw, w_a, w_b, w_gate, l):
    m = h.shape[0]
    tm, tn = _row_tile(m), INPROJ_COL_TILE
    n_a = w_a.shape[-1] // tn
    return pl.pallas_call(
        functools.partial(_inproj_body, n_a=n_a),
        grid=(m // tm, MAIN_COLS // tn),
        in_specs=[
            pl.BlockSpec((tm, D_MODEL), lambda i, j: (i, 0)),
            pl.BlockSpec((None, 1, D_MODEL), lambda i, j: (l, 0, 0)),
            pl.BlockSpec((None, D_MODEL, tn), lambda i, j: (l, 0, jnp.minimum(j, n_a - 1))),
            pl.BlockSpec((None, D_MODEL, tn), lambda i, j: (l, 0, jnp.maximum(j - n_a, 0))),
            pl.BlockSpec((None, D_MODEL, GATE_COLS), lambda i, j: (l, 0, 0)),
        ],
        out_specs=[
            pl.BlockSpec((tm, tn), lambda i, j: (i, j)),
            pl.BlockSpec((tm, GATE_COLS), lambda i, j: (i, 0)),
        ],
        out_shape=[
            jax.ShapeDtypeStruct((m, MAIN_COLS), F32),
            jax.ShapeDtypeStruct((m, GATE_COLS), F32),
        ],
        scratch_shapes=[pltpu.VMEM((tm, D_MODEL), BF16)],
        compiler_params=_params(("parallel", "arbitrary"), 48),
        name="inproj",
    )(h, nw, w_a, w_b, w_gate)


def _gates_body(gate_ref, alog_ref, dtb_ref, beta_ref, gc_ref):
    beta_ref[...] = jax.nn.sigmoid(gate_ref[:, 0:LANES])
    g = -jnp.exp(alog_ref[...]) * _softplus(gate_ref[:, LANES:2 * LANES] + dtb_ref[...])
    pos = lax.broadcasted_iota(jnp.int32, g.shape, 0) & (CHUNK - 1)
    sh = 1
    while sh < CHUNK:
        g = g + jnp.where(pos >= sh, pltpu.roll(g, sh, axis=0), 0.0)
        sh *= 2
    gc_ref[...] = g


def _gates(gate, alog, dtb, l, batch, seq):
    vec = pl.BlockSpec((None, 1, LANES), lambda b: (l, 0, 0))
    return pl.pallas_call(
        _gates_body,
        grid=(batch,),
        in_specs=[pl.BlockSpec((seq, GATE_COLS), lambda b: (b, 0)), vec, vec],
        out_specs=[pl.BlockSpec((seq, LANES), lambda b: (b, 0))] * 2,
        out_shape=[jax.ShapeDtypeStruct((batch * seq, LANES), F32)] * 2,
        compiler_params=_params(("parallel",), 32),
        name="gates",
    )(gate, alog, dtb)


def _gdn_prep_body(q_ref, k_ref, v_ref, qh_ref, kh_ref, vh_ref, cwq_ref, cwk_ref, cwv_ref, beta_ref, gc_ref,
                   u_ref, wq_ref, kdt_ref, qk_ref, gl_ref, qs, ks, vs):
    head = pl.program_id(1)
    r = pl.program_id(2)
    rp = q_ref.shape[0]
    rb = min(CONV_ROWS, rp)

    def conv_silu(x_ref, halo_ref, cw_ref, i):
        start = pl.multiple_of(i * rb, rb)
        cur = x_ref[pl.ds(start, rb), :]
        pstart = pl.multiple_of(jnp.maximum(start - SUBLANES, 0), SUBLANES)
        halo = jnp.where(r > 0, halo_ref[...], 0.0)
        prev = jnp.where(i > 0, x_ref[pl.ds(pstart, SUBLANES), :], halo)
        ext = jnp.concatenate([prev, cur], axis=0)
        cw = cw_ref[...]
        off = SUBLANES - (CONV_W - 1)
        y = ext[off:off + rb] * cw[0:1]
        for t in range(1, CONV_W):
            y = y + ext[off + t:off + t + rb] * cw[t:t + 1]
        return _silu(y)

    def prep(i, carry):
        rows = pl.ds(pl.multiple_of(i * rb, rb), rb)
        qs[rows, :] = _l2(conv_silu(q_ref, qh_ref, cwq_ref, i)) * (HEAD_DIM ** -0.5)
        ks[rows, :] = _l2(conv_silu(k_ref, kh_ref, cwk_ref, i))
        vs[rows, :] = conv_silu(v_ref, vh_ref, cwv_ref, i)
        return carry

    lax.fori_loop(0, rp // rb, prep, 0)

    head_lane = lax.broadcasted_iota(jnp.int32, (PAIR, LANES), 1) == head
    ri = lax.broadcasted_iota(jnp.int32, (PAIR, PAIR), 0)
    ci = lax.broadcasted_iota(jnp.int32, (PAIR, PAIR), 1)
    same = (ri >= CHUNK) == (ci >= CHUNK)
    tril = same & (ri >= ci)
    strict = same & (ri > ci)
    eye = (ri == ci).astype(F32)
    first = ri < CHUNK

    def lane_bcast(x):
        return jnp.broadcast_to(jnp.sum(jnp.where(head_lane, x, 0.0), axis=1, keepdims=True), (PAIR, LANES))

    n_pairs = rp // PAIR
    unroll = min(PREP_UNROLL, n_pairs)
    nt_dims = (((1,), (1,)), ((), ()))

    def group(g, carry):
        js = [g * unroll + t for t in range(unroll)]
        rows = [pl.ds(pl.multiple_of(j * PAIR, PAIR), PAIR) for j in js]
        q = [qs[r, :] for r in rows]
        k = [ks[r, :] for r in rows]
        bb = [lane_bcast(beta_ref[r, :]) for r in rows]
        gc = [lane_bcast(gc_ref[r, :]) for r in rows]
        decay = [jnp.where(tril, jnp.exp(jnp.where(tril, x - x.T, 0.0)), 0.0) for x in gc]
        kb = [a * b for a, b in zip(k, bb)]
        a1 = [lax.dot_general(jnp.concatenate([a, b], axis=0), c, nt_dims, preferred_element_type=F32)
              for a, b, c in zip(q, kb, k)]
        for r, a, d in zip(rows, a1, decay):
            qk_ref[r, :] = (a[:PAIR] * d).astype(BF16)
        m = [jnp.where(strict, a[PAIR:] * d, 0.0) for a, d in zip(a1, decay)]
        inv = [eye - x for x in m]
        mp = m
        sh = 1
        while 2 * sh < CHUNK:
            sh *= 2
            mp = [_dot(x, x) for x in mp]
            inv = [a + _dot(a, x) for a, x in zip(inv, mp)]
        eg = [jnp.exp(x) for x in gc]
        uw = [_dot(a, jnp.concatenate([vs[r, :] * b, c * e], axis=1))
              for a, r, b, c, e in zip(inv, rows, bb, kb, eg)]
        for j, r, x, a, e, kk, gcx in zip(js, rows, uw, q, eg, k, gc):
            u_ref[r, :] = x[:, :HEAD_DIM]
            w = x[:, HEAD_DIM:]
            qg = a * e
            wq_ref[pl.ds(pl.multiple_of(j * 2 * PAIR, 2 * PAIR), 2 * PAIR), :] = jnp.concatenate(
                [w[:CHUNK], qg[:CHUNK], w[CHUNK:], qg[CHUNK:]], axis=0).astype(BF16)
            gl0 = gcx[CHUNK - 1:CHUNK, :]
            gl1 = gcx[PAIR - 1:PAIR, :]
            kd = kk * jnp.exp(jnp.where(first, gl0, gl1) - gcx)
            kdt_ref[:, r] = kd.T.astype(BF16)
            gl_ref[pl.ds(pl.multiple_of(j * 2 * SUBLANES, 2 * SUBLANES), 2 * SUBLANES), :] = jnp.concatenate(
                [jnp.broadcast_to(jnp.exp(gl0), (SUBLANES, LANES)),
                 jnp.broadcast_to(jnp.exp(gl1), (SUBLANES, LANES))], axis=0)
        return carry

    lax.fori_loop(0, n_pairs // unroll, group, 0)


def _gdn_prep(proj, beta, gc, conv_w, l, batch, seq):
    nh = DN_HEADS
    rp = min(PREP_ROWS, seq)
    nr = seq // rp
    n_chunks = seq // CHUNK
    col = lambda base: (lambda b, h, r: (b * nr + r, base + h))
    halo = lambda base: (lambda b, h, r: (jnp.maximum((b * nr + r) * (rp // SUBLANES) - 1, 0), base + h))
    cwcol = lambda base: (lambda b, h, r: (l, 0, base + h))
    gate_spec = pl.BlockSpec((rp, LANES), lambda b, h, r: (b * nr + r, 0))
    per_head = lambda rows, cols: pl.BlockSpec((None, None, rows, cols), lambda b, h, r: (b, h, r, 0))
    return pl.pallas_call(
        _gdn_prep_body,
        grid=(batch, nh, nr),
        in_specs=[
            pl.BlockSpec((rp, HEAD_DIM), col(0)),
            pl.BlockSpec((rp, HEAD_DIM), col(nh)),
            pl.BlockSpec((rp, HEAD_DIM), col(2 * nh)),
            pl.BlockSpec((SUBLANES, HEAD_DIM), halo(0)),
            pl.BlockSpec((SUBLANES, HEAD_DIM), halo(nh)),
            pl.BlockSpec((SUBLANES, HEAD_DIM), halo(2 * nh)),
            pl.BlockSpec((None, CONV_W, HEAD_DIM), cwcol(0)),
            pl.BlockSpec((None, CONV_W, HEAD_DIM), cwcol(nh)),
            pl.BlockSpec((None, CONV_W, HEAD_DIM), cwcol(2 * nh)),
            gate_spec, gate_spec,
        ],
        out_specs=[
            per_head(rp, HEAD_DIM),
            per_head(2 * rp, HEAD_DIM),
            pl.BlockSpec((None, None, HEAD_DIM, rp), lambda b, h, r: (b, h, 0, r)),
            per_head(rp, HEAD_DIM),
            per_head((rp // CHUNK) * SUBLANES, LANES),
        ],
        out_shape=[
            jax.ShapeDtypeStruct((batch, nh, seq, HEAD_DIM), F32),
            jax.ShapeDtypeStruct((batch, nh, 2 * seq, HEAD_DIM), BF16),
            jax.ShapeDtypeStruct((batch, nh, HEAD_DIM, seq), BF16),
            jax.ShapeDtypeStruct((batch, nh, seq, PAIR), BF16),
            jax.ShapeDtypeStruct((batch, nh, n_chunks * SUBLANES, LANES), F32),
        ],
        scratch_shapes=[pltpu.VMEM((rp, HEAD_DIM), F32)] * 3,
        compiler_params=_params(("parallel", "parallel", "parallel"), 32),
        name="gdn_prep",
    )(proj, proj, proj, proj, proj, proj, conv_w, conv_w, conv_w, beta, gc)


def _gdn_scan_body(u_ref, wq_ref, kdt_ref, qk_ref, gl_ref, z_ref, dnw_ref, o_ref, s_out_ref, st_s, *, seq):
    hs = u_ref.shape[0]
    st_s[...] = jnp.zeros(st_s.shape, F32)
    zeros = jnp.zeros((CHUNK, HEAD_DIM), BF16)

    def pair(p, carry):
        kcols = pl.ds(pl.multiple_of(p * PAIR, PAIR), PAIR)
        heads = range(hs)
        st = [st_s[hh] for hh in heads]
        for s in range(2):
            c = 2 * p + s
            rows = pl.ds(pl.multiple_of(c * CHUNK, CHUNK), CHUNK)
            wrows = pl.ds(pl.multiple_of(c * PAIR, PAIR), PAIR)
            grows = pl.ds(pl.multiple_of(c * SUBLANES, SUBLANES), SUBLANES)
            r = [_dot(wq_ref[hh, wrows, :], st[hh].astype(BF16)) for hh in heads]
            vnew = [(u_ref[hh, rows, :] - r[hh][:CHUNK]).astype(BF16) for hh in heads]
            vpad = [jnp.concatenate([x, zeros] if s == 0 else [zeros, x], axis=0) for x in vnew]
            st = [st[hh] * gl_ref[hh, grows, :][0:1, :] + _dot(kdt_ref[hh, :, kcols], vpad[hh]) for hh in heads]
            o = [r[hh][CHUNK:] + _dot(qk_ref[hh, rows, :], vpad[hh]) for hh in heads]
            for hh in heads:
                lo = hh * HEAD_DIM
                o_ref[rows, lo:lo + HEAD_DIM] = (
                    _rms(o[hh], dnw_ref[...]) * _silu(z_ref[rows, lo:lo + HEAD_DIM])).astype(BF16)
        for hh in heads:
            st_s[hh] = st[hh]
        return carry

    lax.fori_loop(0, seq // PAIR, pair, 0)
    s_out_ref[...] = st_s[...]


def _gdn_scan(u, wq, kdt, qk, gl, proj, dnw, l, batch, seq):
    m = proj.shape[0]
    hs = SCAN_HEADS
    zcol = (3 * DN_WIDTH) // (hs * HEAD_DIM)
    per_group = lambda rows, cols: pl.BlockSpec((None, hs, rows, cols), lambda b, g: (b, g, 0, 0))
    return pl.pallas_call(
        functools.partial(_gdn_scan_body, seq=seq),
        grid=(batch, DN_HEADS // hs),
        in_specs=[
            per_group(seq, HEAD_DIM),
            per_group(2 * seq, HEAD_DIM),
            per_group(HEAD_DIM, seq),
            per_group(seq, PAIR),
            per_group((seq // CHUNK) * SUBLANES, LANES),
            pl.BlockSpec((seq, hs * HEAD_DIM), lambda b, g: (b, zcol + g)),
            pl.BlockSpec((None, 1, HEAD_DIM), lambda b, g: (l, 0, 0)),
        ],
        out_specs=[
            pl.BlockSpec((seq, hs * HEAD_DIM), lambda b, g: (b, g)),
            per_group(HEAD_DIM, HEAD_DIM),
        ],
        out_shape=[
            jax.ShapeDtypeStruct((m, DN_WIDTH), BF16),
            jax.ShapeDtypeStruct((batch, DN_HEADS, HEAD_DIM, HEAD_DIM), F32),
        ],
        scratch_shapes=[pltpu.VMEM((hs, HEAD_DIM, HEAD_DIM), F32)],
        compiler_params=_params(("parallel", "parallel"), 48),
        name="gdn_scan",
    )(u, wq, kdt, qk, gl, proj, dnw)


def _gdn_sample_body(qkv_ref, z_ref, gate_ref, cs_ref, s_ref, cw_ref, alog_ref, dtb_ref, dnw_ref, *rest,
                     n_alias):
    dn_ref, sn_ref, cn_ref, q_s, k_s, v_s, eg_s, beta_s, o_s = rest[n_alias:]
    nb = qkv_ref.shape[0]
    x = qkv_ref[...]
    c0, c1, c2 = cs_ref[:, 0, :], cs_ref[:, 1, :], cs_ref[:, 2, :]
    cw = cw_ref[...]
    y = c0 * cw[0:1] + c1 * cw[1:2] + c2 * cw[2:3] + x * cw[3:4]
    y = _silu(y)
    cn_ref[:, 0, :] = c1
    cn_ref[:, 1, :] = c2
    cn_ref[:, 2, :] = x
    for h in range(DN_HEADS):
        lo = h * HEAD_DIM
        q_s[h] = _l2(y[:, lo:lo + HEAD_DIM]) * (HEAD_DIM ** -0.5)
        k_s[h] = _l2(y[:, DN_WIDTH + lo:DN_WIDTH + lo + HEAD_DIM])
        v_s[h] = y[:, 2 * DN_WIDTH + lo:2 * DN_WIDTH + lo + HEAD_DIM]
    beta_s[...] = jax.nn.sigmoid(gate_ref[:, 0:LANES])
    eg_s[...] = jnp.exp(-jnp.exp(alog_ref[...]) * _softplus(gate_ref[:, LANES:2 * LANES] + dtb_ref[...]))

    pad7 = jnp.zeros((SUBLANES - 1, HEAD_DIM), F32)
    pad6 = jnp.zeros((SUBLANES - 2, HEAD_DIM), F32)

    tn_dims = (((0,), (0,)), ((), ()))

    def per_sample(b, carry):
        one = pl.ds(b, 1)
        eg_row = eg_s[one, :]
        beta_row = beta_s[one, :]
        heads = range(DN_HEADS)
        k = [k_s[h, one, :] for h in heads]
        q = [q_s[h, one, :] for h in heads]
        st = [s_ref[b, h] * eg_row[:, h:h + 1] for h in heads]
        r = [_dot(jnp.concatenate([k[h], q[h], pad6], axis=0), st[h]) for h in heads]
        dv = [(v_s[h, one, :] - r[h][0:1]) * beta_row[:, h:h + 1] for h in heads]
        outer = [lax.dot_general(jnp.concatenate([k[h], pad7], axis=0), jnp.concatenate([dv[h], pad7], axis=0),
                                 tn_dims, preferred_element_type=F32) for h in heads]
        for h in heads:
            sn_ref[b, h] = st[h] + outer[h]
            o_s[h, one, :] = r[h][1:2] + jnp.sum(q[h] * k[h], axis=-1, keepdims=True) * dv[h]
        return carry

    lax.fori_loop(0, nb, per_sample, 0)

    z = z_ref[...]
    for h in range(DN_HEADS):
        lo = h * HEAD_DIM
        dn_ref[:, lo:lo + HEAD_DIM] = (_rms(o_s[h], dnw_ref[...]) * _silu(z[:, lo:lo + HEAD_DIM])).astype(BF16)


def _gdn_sample(proj, gate, state_conv, state_delta, conv_w, alog, dtb, dnw, dn_buf, prev, l, row0):
    depth, nseq = state_delta.shape[:2]
    nb = SAMPLE_BLOCK
    base = row0 // nb
    vec = pl.BlockSpec((None, 1, LANES), lambda i: (l, 0, 0))
    delta_spec = pl.BlockSpec((None, nb, DN_HEADS, HEAD_DIM, HEAD_DIM), lambda i: (l, i, 0, 0, 0))
    conv_spec = pl.BlockSpec((None, nb, CONV_W - 1, 3 * DN_WIDTH), lambda i: (l, i, 0, 0))
    inputs = [proj, proj, gate, state_conv, state_delta, conv_w, alog, dtb, dnw, dn_buf]
    in_specs = [
        pl.BlockSpec((nb, 3 * DN_WIDTH), lambda i: (base + i, 0)),
        pl.BlockSpec((nb, DN_WIDTH), lambda i: (base + i, 3)),
        pl.BlockSpec((nb, GATE_COLS), lambda i: (base + i, 0)),
        conv_spec, delta_spec,
        pl.BlockSpec((None, CONV_W, 3 * DN_WIDTH), lambda i: (l, 0, 0)),
        vec, vec, vec,
        _any_spec(),
    ]
    aliases = {9: 0}
    if prev is not None:
        inputs += list(prev)
        in_specs += [_any_spec(), _any_spec()]
        aliases.update({10: 1, 11: 2})
    return pl.pallas_call(
        functools.partial(_gdn_sample_body, n_alias=len(inputs) - 9),
        grid=(nseq // nb,),
        in_specs=in_specs,
        out_specs=[pl.BlockSpec((nb, DN_WIDTH), lambda i: (base + i, 0)), delta_spec, conv_spec],
        out_shape=[
            jax.ShapeDtypeStruct(dn_buf.shape, dn_buf.dtype),
            jax.ShapeDtypeStruct(state_delta.shape, F32),
            jax.ShapeDtypeStruct(state_conv.shape, F32),
        ],
        scratch_shapes=[
            pltpu.VMEM((DN_HEADS, nb, HEAD_DIM), F32),
            pltpu.VMEM((DN_HEADS, nb, HEAD_DIM), F32),
            pltpu.VMEM((DN_HEADS, nb, HEAD_DIM), F32),
            pltpu.VMEM((nb, LANES), F32),
            pltpu.VMEM((nb, LANES), F32),
            pltpu.VMEM((DN_HEADS, nb, HEAD_DIM), F32),
        ],
        input_output_aliases=aliases,
        compiler_params=_params(("parallel",), 48),
        name="gdn_sample",
    )(*inputs)


def _pool_matmul(d, w_ref, sc_ref, o_ref, gi):
    lo = gi * POOL_GROUP
    y = _dot(d.astype(BF16), w_ref[gi]) * sc_ref[:, lo:lo + POOL_GROUP]
    o_ref[:, lo:lo + POOL_GROUP] = y.astype(BF16)


def _pool_prompt_body(p_ref, prev_ref, w_ref, sc_ref, o_ref, *, tiles_per_seq):
    rt = p_ref.shape[0]
    hist = prev_ref.shape[0]
    t = pl.program_id(0) % tiles_per_seq
    cur = p_ref[...]
    prev = jnp.where(t == 0, 0.0, prev_ref[...])
    pos = t * rt + lax.broadcasted_iota(jnp.int32, (rt, POOL_GROUP), 0)
    for gi, win in enumerate(POOL_WINDOWS):
        lo = gi * POOL_GROUP
        x = cur[:, lo:lo + POOL_GROUP]
        s = jnp.concatenate([prev[:, lo:lo + POOL_GROUP], x], axis=0)
        sh = 1
        while sh < win:
            s = s + pltpu.roll(s, sh, axis=0)
            sh *= 2
        cnt = jnp.minimum(pos + 1, win).astype(F32)
        _pool_matmul(s[hist:] / cnt - x, w_ref, sc_ref, o_ref, gi)


def _pool_prompt(proj, w_pool, scale, l, batch, seq):
    m = proj.shape[0]
    rt = min(POOL_ROWS, seq)
    hist = POOL_BUF + 1
    tiles_per_seq = seq // rt
    pcol = (4 * DN_WIDTH) // POOL_WIDTH
    ng = len(POOL_WINDOWS)
    return pl.pallas_call(
        functools.partial(_pool_prompt_body, tiles_per_seq=tiles_per_seq),
        grid=(batch * tiles_per_seq,),
        in_specs=[
            pl.BlockSpec((rt, POOL_WIDTH), lambda r: (r, pcol)),
            pl.BlockSpec((hist, POOL_WIDTH), lambda r: (jnp.maximum(r * (rt // hist) - 1, 0), pcol)),
            pl.BlockSpec((None, ng, POOL_GROUP, POOL_GROUP), lambda r: (l, 0, 0, 0)),
            pl.BlockSpec((None, 1, POOL_WIDTH), lambda r: (l, 0, 0)),
        ],
        out_specs=pl.BlockSpec((rt, POOL_WIDTH), lambda r: (r, 0)),
        out_shape=jax.ShapeDtypeStruct((m, POOL_WIDTH), BF16),
        compiler_params=_params(("parallel",), 32),
        name="pool_prompt",
    )(proj, proj, w_pool, scale)


def _pool_sample_body(p_ref, buf_ref, w_ref, sc_ref, *rest):
    o_ref, nb_ref = rest[-2:]
    cur = p_ref[...]
    for gi, win in enumerate(POOL_WINDOWS):
        lo = gi * POOL_GROUP
        x = cur[:, lo:lo + POOL_GROUP]
        s = x
        for j in range(1, win):
            s = s + buf_ref[:, POOL_BUF - j, lo:lo + POOL_GROUP]
        cnt = float(min(PAST_LEN + 1, win))
        _pool_matmul(s / cnt - x, w_ref, sc_ref, o_ref, gi)
    for j in range(POOL_BUF - 1):
        nb_ref[:, j, :] = buf_ref[:, j + 1, :]
    nb_ref[:, POOL_BUF - 1, :] = cur


def _pool_sample(proj, state_pool, w_pool, scale, pool_buf, prev, l, row0):
    nseq = state_pool.shape[1]
    nb = POOL_SAMPLE_BLOCK
    base = row0 // nb
    pcol = (4 * DN_WIDTH) // POOL_WIDTH
    ng = len(POOL_WINDOWS)
    state_spec = pl.BlockSpec((None, nb, POOL_BUF, POOL_WIDTH), lambda i: (l, i, 0, 0))
    inputs = [proj, state_pool, w_pool, scale, pool_buf]
    in_specs = [
        pl.BlockSpec((nb, POOL_WIDTH), lambda i: (base + i, pcol)),
        state_spec,
        pl.BlockSpec((None, ng, POOL_GROUP, POOL_GROUP), lambda i: (l, 0, 0, 0)),
        pl.BlockSpec((None, 1, POOL_WIDTH), lambda i: (l, 0, 0)),
        _any_spec(),
    ]
    aliases = {4: 0}
    if prev is not None:
        inputs.append(prev)
        in_specs.append(_any_spec())
        aliases[5] = 1
    return pl.pallas_call(
        _pool_sample_body,
        grid=(nseq // nb,),
        in_specs=in_specs,
        out_specs=[pl.BlockSpec((nb, POOL_WIDTH), lambda i: (base + i, 0)), state_spec],
        out_shape=[
            jax.ShapeDtypeStruct(pool_buf.shape, pool_buf.dtype),
            jax.ShapeDtypeStruct(state_pool.shape, F32),
        ],
        input_output_aliases=aliases,
        compiler_params=_params(("parallel",), 32),
        name="pool_sample",
    )(*inputs)


def _outproj_body(dn_ref, pool_ref, w1_ref, w2_ref, h_ref, o_ref):
    o_ref[...] = h_ref[...] + _dot(dn_ref[...], w1_ref[...]) + _dot(pool_ref[...], w2_ref[...])


def _outproj(dn, pool, w_out, h, l):
    m = h.shape[0]
    tm, tn = _row_tile(m), OUTPROJ_COL_TILE
    return pl.pallas_call(
        _outproj_body,
        grid=(m // tm, D_MODEL // tn),
        in_specs=[
            pl.BlockSpec((tm, DN_WIDTH), lambda i, j: (i, 0)),
            pl.BlockSpec((tm, POOL_WIDTH), lambda i, j: (i, 0)),
            pl.BlockSpec((None, DN_WIDTH, tn), lambda i, j: (l, 0, j)),
            pl.BlockSpec((None, POOL_WIDTH, tn), lambda i, j: (l, 1, j)),
            pl.BlockSpec((tm, tn), lambda i, j: (i, j)),
        ],
        out_specs=pl.BlockSpec((tm, tn), lambda i, j: (i, j)),
        out_shape=jax.ShapeDtypeStruct((m, D_MODEL), F32),
        compiler_params=_params(("parallel", "parallel"), 40),
        name="outproj",
    )(dn, pool, w_out, w_out, h)


def _ffn_body(h_ref, nw_ref, wg_ref, wu_ref, wd_ref, nf_ref, o_ref, xn_ref, *, final_norm):
    f = pl.program_id(1)

    @pl.when(f == 0)
    def _():
        x = h_ref[...]
        xn_ref[...] = _rms(x, nw_ref[...]).astype(BF16)
        o_ref[...] = x

    xn = xn_ref[...]
    act = (_silu(_dot(xn, wg_ref[...])) * _dot(xn, wu_ref[...])).astype(BF16)
    o_ref[...] += _dot(act, wd_ref[...])

    if final_norm:
        @pl.when(f == pl.num_programs(1) - 1)
        def _():
            o_ref[...] = _rms(o_ref[...], nf_ref[...])


def _ffn(h, nw, w_gate_up, w_down, nf, l, final_norm):
    m = h.shape[0]
    tm, tf = _row_tile(m), FFN_TILE
    nft = D_FF // tf
    return pl.pallas_call(
        functools.partial(_ffn_body, final_norm=final_norm),
        grid=(m // tm, nft),
        in_specs=[
            pl.BlockSpec((tm, D_MODEL), lambda i, f: (i, 0)),
            pl.BlockSpec((None, 1, D_MODEL), lambda i, f: (l, 0, 0)),
            pl.BlockSpec((None, D_MODEL, tf), lambda i, f: (l, 0, f)),
            pl.BlockSpec((None, D_MODEL, tf), lambda i, f: (l, 0, nft + f)),
            pl.BlockSpec((None, tf, D_MODEL), lambda i, f: (l, f, 0)),
            pl.BlockSpec((1, D_MODEL), lambda i, f: (0, 0)),
        ],
        out_specs=pl.BlockSpec((tm, D_MODEL), lambda i, f: (i, 0)),
        out_shape=jax.ShapeDtypeStruct((m, D_MODEL), F32),
        scratch_shapes=[pltpu.VMEM((tm, D_MODEL), BF16)],
        compiler_params=_params(("parallel", "arbitrary"), 52),
        name="ffn",
    )(h, nw, w_gate_up, w_gate_up, w_down, nf)


def _lane_vecs(x):
    depth, n = x.shape
    return jnp.zeros((depth, 1, LANES), F32).at[:, 0, :n].set(x)


def kernel(x_prompt, x_sample, state_delta, state_conv, state_pool, norm_mix, w_in, conv_w, a_log, dt_bias,
           dn_norm, w_pool, pool_scale, w_out, norm_ffn, w_gate_up, w_down, norm_final):
    batch, seq, _ = x_prompt.shape
    nseq = x_sample.shape[0]
    depth = w_in.shape[0]
    rows_p = batch * seq
    h = jnp.concatenate([x_prompt.reshape(rows_p, D_MODEL), x_sample.reshape(nseq, D_MODEL)], axis=0)

    o1 = 4 * DN_WIDTH
    w_a = w_in[:, :, :o1].astype(BF16)
    w_b = w_in[:, :, o1 + 2 * DN_HEADS:].astype(BF16)
    w_gate = jnp.zeros((depth, D_MODEL, GATE_COLS), BF16)
    w_gate = w_gate.at[:, :, :DN_HEADS].set(w_in[:, :, o1:o1 + DN_HEADS].astype(BF16))
    w_gate = w_gate.at[:, :, LANES:LANES + DN_HEADS].set(w_in[:, :, o1 + DN_HEADS:o1 + 2 * DN_HEADS].astype(BF16))
    w_out_b = w_out.astype(BF16)
    w_gu_b = w_gate_up.astype(BF16)
    w_down_b = w_down.astype(BF16)
    w_pool_b = w_pool.astype(BF16)
    alog, dtb = _lane_vecs(a_log), _lane_vecs(dt_bias)
    dnw = dn_norm.reshape(depth, 1, HEAD_DIM)
    scale = pool_scale.reshape(depth, 1, POOL_WIDTH)
    nmix = norm_mix.reshape(depth, 1, D_MODEL)
    nffn = norm_ffn.reshape(depth, 1, D_MODEL)
    nfin = norm_final.reshape(1, D_MODEL)

    delta_p, conv_p, pool_p = [], [], []
    sample_states = None
    pool_states = None
    for l in range(depth):
        proj, gate = _inproj(h, nmix, w_a, w_b, w_gate, l)

        beta, gc = _gates(gate, alog, dtb, l, batch, seq)
        u, wq, kdt, qk, gl = _gdn_prep(proj, beta, gc, conv_w, l, batch, seq)
        dn, s_p = _gdn_scan(u, wq, kdt, qk, gl, proj, dnw, l, batch, seq)
        dn, *sample_states = _gdn_sample(proj, gate, state_conv, state_delta, conv_w, alog, dtb, dnw, dn,
                                         sample_states, l, rows_p)
        pool = _pool_prompt(proj, w_pool_b, scale, l, batch, seq)
        pool, pool_states = _pool_sample(proj, state_pool, w_pool_b, scale, pool, pool_states, l, rows_p)

        delta_p.append(s_p)
        conv_p.append(jnp.stack([proj[(b + 1) * seq - (CONV_W - 1):(b + 1) * seq, :3 * DN_WIDTH]
                                 for b in range(batch)]))
        pool_p.append(jnp.stack([proj[(b + 1) * seq - POOL_BUF:(b + 1) * seq, o1:] for b in range(batch)]))

        h = _outproj(dn, pool, w_out_b, h, l)
        h = _ffn(h, nffn, w_gu_b, w_down_b, nfin, l, l == depth - 1)

    delta_s, conv_s = sample_states
    y_prompt = h[:rows_p].reshape(batch, seq, D_MODEL)
    y_sample = h[rows_p:].reshape(nseq, 1, D_MODEL)
    return (y_prompt, y_sample, jnp.stack(delta_p), jnp.stack(conv_p), jnp.stack(pool_p),
            delta_s, conv_s, pool_states)
```

```python
import functools

import jax
import jax.numpy as jnp
from jax import lax
from jax.experimental import pallas as pl
from jax.experimental.pallas import tpu as pltpu

F32 = jnp.float32
BF16 = jnp.bfloat16

D_MODEL = 2048
DN_WIDTH = 1024
DN_HEADS = 8
HEAD_DIM = 128
CONV_W = 4
CHUNK = 64
POOL_WIDTH = 1024
POOL_WINDOWS = (2, 4, 8, 16)
POOL_GROUP = 256
POOL_BUF = 15
D_FF = 5632
EPS = 1e-6
PAST_LEN = 16384

MAIN_COLS = 4 * DN_WIDTH + POOL_WIDTH
GATE_COLS = 256
LANES = 128
SUBLANES = 8
PAIR = 2 * CHUNK

ROW_TILES = (640, 512, 256, 128)
INPROJ_COL_TILE = 1024
OUTPROJ_COL_TILE = 2048
FFN_TILE = 512
CONV_ROWS = 256
PREP_ROWS = 1024
PREP_UNROLL = 8
SCAN_HEADS = 4
POOL_ROWS = 512
SAMPLE_BLOCK = 16
POOL_SAMPLE_BLOCK = 32
MIB = 1024 * 1024


def _params(semantics, vmem_mib):
    return pltpu.CompilerParams(dimension_semantics=semantics, vmem_limit_bytes=vmem_mib * MIB)


def _silu(x):
    return x * jax.nn.sigmoid(x)


def _softplus(x):
    return jnp.maximum(x, 0.0) + jnp.log1p(jnp.exp(-jnp.abs(x)))


def _rms(x, w):
    return x * lax.rsqrt(jnp.mean(x * x, axis=-1, keepdims=True) + EPS) * w


def _l2(x):
    return x * lax.rsqrt(jnp.sum(x * x, axis=-1, keepdims=True) + EPS)


def _dot(a, b):
    return jnp.dot(a, b, preferred_element_type=F32)


def _row_tile(m):
    return next(t for t in ROW_TILES if m % t == 0)


def _any_spec():
    return pl.BlockSpec(memory_space=pl.ANY)


def _inproj_body(h_ref, nw_ref, wa_ref, wb_ref, wg_ref, proj_ref, gate_ref, xn_ref, *, n_a):
    j = pl.program_id(1)

    @pl.when(j == 0)
    def _():
        xn = _rms(h_ref[...], nw_ref[...]).astype(BF16)
        xn_ref[...] = xn
        gate_ref[...] = _dot(xn, wg_ref[...])

    @pl.when(j < n_a)
    def _():
        proj_ref[...] = _dot(xn_ref[...], wa_ref[...])

    @pl.when(j >= n_a)
    def _():
        proj_ref[...] = _dot(xn_ref[...], wb_ref[...])


def _inproj(h, nw, w_a, w_b, w_gate, l):
    m = h.shape[0]
    tm, tn = _row_tile(m), INPROJ_COL_TILE
    n_a = w_a.shape[-1] // tn
    return pl.pallas_call(
        functools.partial(_inproj_body, n_a=n_a),
        grid=(m // tm, MAIN_COLS // tn),
        in_specs=[
            pl.BlockSpec((tm, D_MODEL), lambda i, j: (i, 0)),
            pl.BlockSpec((None, 1, D_MODEL), lambda i, j: (l, 0, 0)),
            pl.BlockSpec((None, D_MODEL, tn), lambda i, j: (l, 0, jnp.minimum(j, n_a - 1))),
            pl.BlockSpec((None, D_MODEL, tn), lambda i, j: (l, 0, jnp.maximum(j - n_a, 0))),
            pl.BlockSpec((None, D_MODEL, GATE_COLS), lambda i, j: (l, 0, 0)),
        ],
        out_specs=[
            pl.BlockSpec((tm, tn), lambda i, j: (i, j)),
            pl.BlockSpec((tm, GATE_COLS), lambda i, j: (i, 0)),
        ],
        out_shape=[
            jax.ShapeDtypeStruct((m, MAIN_COLS), F32),
            jax.ShapeDtypeStruct((m, GATE_COLS), F32),
        ],
        scratch_shapes=[pltpu.VMEM((tm, D_MODEL), BF16)],
        compiler_params=_params(("parallel", "arbitrary"), 48),
        name="inproj",
    )(h, nw, w_a, w_b, w_gate)


def _gates_body(gate_ref, alog_ref, dtb_ref, beta_ref, gc_ref):
    beta_ref[...] = jax.nn.sigmoid(gate_ref[:, 0:LANES])
    g = -jnp.exp(alog_ref[...]) * _softplus(gate_ref[:, LANES:2 * LANES] + dtb_ref[...])
    pos = lax.broadcasted_iota(jnp.int32, g.shape, 0) & (CHUNK - 1)
    sh = 1
    while sh < CHUNK:
        g = g + jnp.where(pos >= sh, pltpu.roll(g, sh, axis=0), 0.0)
        sh *= 2
    gc_ref[...] = g


def _gates(gate, alog, dtb, l, batch, seq):
    vec = pl.BlockSpec((None, 1, LANES), lambda b: (l, 0, 0))
    return pl.pallas_call(
        _gates_body,
        grid=(batch,),
        in_specs=[pl.BlockSpec((seq, GATE_COLS), lambda b: (b, 0)), vec, vec],
        out_specs=[pl.BlockSpec((seq, LANES), lambda b: (b, 0))] * 2,
        out_shape=[jax.ShapeDtypeStruct((batch * seq, LANES), F32)] * 2,
        compiler_params=_params(("parallel",), 32),
        name="gates",
    )(gate, alog, dtb)


def _gdn_prep_body(q_ref, k_ref, v_ref, qh_ref, kh_ref, vh_ref, cwq_ref, cwk_ref, cwv_ref, beta_ref, gc_ref,
                   u_ref, wq_ref, kdt_ref, qk_ref, gl_ref, qs, ks, vs):
    head = pl.program_id(1)
    r = pl.program_id(2)
    rp = q_ref.shape[0]
    rb = min(CONV_ROWS, rp)

    def conv_silu(x_ref, halo_ref, cw_ref, i):
        cw = cw_ref[...]
        if i == 0:
            halo = jnp.where(r > 0, halo_ref[...], 0.0)
            ext = jnp.concatenate([halo, x_ref[0:rb, :]], axis=0)
            off = SUBLANES - (CONV_W - 1)
            taps = [ext[off + t:off + t + rb] for t in range(CONV_W)]
        else:
            off = i * rb - (CONV_W - 1)
            taps = [x_ref[off + t:off + t + rb, :] for t in range(CONV_W)]
        y = taps[0] * cw[0:1]
        for t in range(1, CONV_W):
            y = y + taps[t] * cw[t:t + 1]
        return _silu(y)

    for i in range(rp // rb):
        rows = slice(i * rb, (i + 1) * rb)
        qs[rows, :] = _l2(conv_silu(q_ref, qh_ref, cwq_ref, i)) * (HEAD_DIM ** -0.5)
        ks[rows, :] = _l2(conv_silu(k_ref, kh_ref, cwk_ref, i))
        vs[rows, :] = conv_silu(v_ref, vh_ref, cwv_ref, i)

    head_lane = lax.broadcasted_iota(jnp.int32, (PAIR, LANES), 1) == head
    ri = lax.broadcasted_iota(jnp.int32, (PAIR, PAIR), 0)
    ci = lax.broadcasted_iota(jnp.int32, (PAIR, PAIR), 1)
    same = (ri >= CHUNK) == (ci >= CHUNK)
    tril = same & (ri >= ci)
    strict = same & (ri > ci)
    eye = (ri == ci).astype(F32)
    first = ri < CHUNK

    def lane_bcast(x):
        return jnp.broadcast_to(jnp.sum(jnp.where(head_lane, x, 0.0), axis=1, keepdims=True), (PAIR, LANES))

    n_pairs = rp // PAIR
    unroll = min(PREP_UNROLL, n_pairs)
    nt_dims = (((1,), (1,)), ((), ()))

    def group(g, carry):
        js = [g * unroll + t for t in range(unroll)]
        rows = [pl.ds(pl.multiple_of(j * PAIR, PAIR), PAIR) for j in js]
        q = [qs[r, :] for r in rows]
        k = [ks[r, :] for r in rows]
        bb = [lane_bcast(beta_ref[r, :]) for r in rows]
        gc = [lane_bcast(gc_ref[r, :]) for r in rows]
        decay = [jnp.where(tril, jnp.exp(jnp.where(tril, x - x.T, 0.0)), 0.0) for x in gc]
        kb = [a * b for a, b in zip(k, bb)]
        a1 = [lax.dot_general(jnp.concatenate([a, b], axis=0), c, nt_dims, preferred_element_type=F32)
              for a, b, c in zip(q, kb, k)]
        for r, a, d in zip(rows, a1, decay):
            qk_ref[r, :] = (a[:PAIR] * d).astype(BF16)
        m = [jnp.where(strict, a[PAIR:] * d, 0.0) for a, d in zip(a1, decay)]
        inv = [eye - x for x in m]
        mp = m
        sh = 1
        while 2 * sh < CHUNK:
            sh *= 2
            mp = [_dot(x, x) for x in mp]
            inv = [a + _dot(a, x) for a, x in zip(inv, mp)]
        eg = [jnp.exp(x) for x in gc]
        uw = [_dot(a, jnp.concatenate([vs[r, :] * b, c * e], axis=1))
              for a, r, b, c, e in zip(inv, rows, bb, kb, eg)]
        for j, r, x, a, e, kk, gcx in zip(js, rows, uw, q, eg, k, gc):
            u_ref[r, :] = x[:, :HEAD_DIM]
            w = x[:, HEAD_DIM:]
            qg = a * e
            wq_ref[pl.ds(pl.multiple_of(j * 2 * PAIR, 2 * PAIR), 2 * PAIR), :] = jnp.concatenate(
                [w[:CHUNK], qg[:CHUNK], w[CHUNK:], qg[CHUNK:]], axis=0).astype(BF16)
            gl0 = gcx[CHUNK - 1:CHUNK, :]
            gl1 = gcx[PAIR - 1:PAIR, :]
            kd = kk * jnp.exp(jnp.where(first, gl0, gl1) - gcx)
            kdt_ref[:, r] = kd.T.astype(BF16)
            gl_ref[pl.ds(pl.multiple_of(j * 2 * SUBLANES, 2 * SUBLANES), 2 * SUBLANES), :] = jnp.concatenate(
                [jnp.broadcast_to(jnp.exp(gl0), (SUBLANES, LANES)),
                 jnp.broadcast_to(jnp.exp(gl1), (SUBLANES, LANES))], axis=0)
        return carry

    lax.fori_loop(0, n_pairs // unroll, group, 0)


def _gdn_prep(proj, beta, gc, conv_w, l, batch, seq):
    nh = DN_HEADS
    rp = min(PREP_ROWS, seq)
    nr = seq // rp
    n_chunks = seq // CHUNK
    col = lambda base: (lambda b, h, r: (b * nr + r, base + h))
    halo = lambda base: (lambda b, h, r: (jnp.maximum((b * nr + r) * (rp // SUBLANES) - 1, 0), base + h))
    cwcol = lambda base: (lambda b, h, r: (l, 0, base + h))
    gate_spec = pl.BlockSpec((rp, LANES), lambda b, h, r: (b * nr + r, 0))
    per_head = lambda rows, cols: pl.BlockSpec((None, None, rows, cols), lambda b, h, r: (b, h, r, 0))
    return pl.pallas_call(
        _gdn_prep_body,
        grid=(batch, nh, nr),
        in_specs=[
            pl.BlockSpec((rp, HEAD_DIM), col(0)),
            pl.BlockSpec((rp, HEAD_DIM), col(nh)),
            pl.BlockSpec((rp, HEAD_DIM), col(2 * nh)),
            pl.BlockSpec((SUBLANES, HEAD_DIM), halo(0)),
            pl.BlockSpec((SUBLANES, HEAD_DIM), halo(nh)),
            pl.BlockSpec((SUBLANES, HEAD_DIM), halo(2 * nh)),
            pl.BlockSpec((None, CONV_W, HEAD_DIM), cwcol(0)),
            pl.BlockSpec((None, CONV_W, HEAD_DIM), cwcol(nh)),
            pl.BlockSpec((None, CONV_W, HEAD_DIM), cwcol(2 * nh)),
            gate_spec, gate_spec,
        ],
        out_specs=[
            per_head(rp, HEAD_DIM),
            per_head(2 * rp, HEAD_DIM),
            pl.BlockSpec((None, None, HEAD_DIM, rp), lambda b, h, r: (b, h, 0, r)),
            per_head(rp, HEAD_DIM),
            per_head((rp // CHUNK) * SUBLANES, LANES),
        ],
        out_shape=[
            jax.ShapeDtypeStruct((batch, nh, seq, HEAD_DIM), F32),
            jax.ShapeDtypeStruct((batch, nh, 2 * seq, HEAD_DIM), BF16),
            jax.ShapeDtypeStruct((batch, nh, HEAD_DIM, seq), BF16),
            jax.ShapeDtypeStruct((batch, nh, seq, PAIR), BF16),
            jax.ShapeDtypeStruct((batch, nh, n_chunks * SUBLANES, LANES), F32),
        ],
        scratch_shapes=[pltpu.VMEM((rp, HEAD_DIM), F32)] * 3,
        compiler_params=_params(("parallel", "parallel", "parallel"), 32),
        name="gdn_prep",
    )(proj, proj, proj, proj, proj, proj, conv_w, conv_w, conv_w, beta, gc)


def _gdn_scan_body(u_ref, wq_ref, kdt_ref, qk_ref, gl_ref, z_ref, dnw_ref, o_ref, s_out_ref, st_s, *, seq):
    hs = u_ref.shape[0]
    st_s[...] = jnp.zeros(st_s.shape, F32)
    zeros = jnp.zeros((CHUNK, HEAD_DIM), BF16)

    def pair(p, carry):
        kcols = pl.ds(pl.multiple_of(p * PAIR, PAIR), PAIR)
        heads = range(hs)
        st = [st_s[hh] for hh in heads]
        for s in range(2):
            c = 2 * p + s
            rows = pl.ds(pl.multiple_of(c * CHUNK, CHUNK), CHUNK)
            wrows = pl.ds(pl.multiple_of(c * PAIR, PAIR), PAIR)
            grows = pl.ds(pl.multiple_of(c * SUBLANES, SUBLANES), SUBLANES)
            r = [_dot(wq_ref[hh, wrows, :], st[hh].astype(BF16)) for hh in heads]
            vnew = [(u_ref[hh, rows, :] - r[hh][:CHUNK]).astype(BF16) for hh in heads]
            vpad = [jnp.concatenate([x, zeros] if s == 0 else [zeros, x], axis=0) for x in vnew]
            st = [st[hh] * gl_ref[hh, grows, :][0:1, :] + _dot(kdt_ref[hh, :, kcols], vpad[hh]) for hh in heads]
            o = [r[hh][CHUNK:] + _dot(qk_ref[hh, rows, :], vpad[hh]) for hh in heads]
            for hh in heads:
                lo = hh * HEAD_DIM
                o_ref[rows, lo:lo + HEAD_DIM] = (
                    _rms(o[hh], dnw_ref[...]) * _silu(z_ref[rows, lo:lo + HEAD_DIM])).astype(BF16)
        for hh in heads:
            st_s[hh] = st[hh]
        return carry

    lax.fori_loop(0, seq // PAIR, pair, 0)
    s_out_ref[...] = st_s[...]


def _gdn_scan(u, wq, kdt, qk, gl, proj, dnw, l, batch, seq):
    m = proj.shape[0]
    hs = SCAN_HEADS
    zcol = (3 * DN_WIDTH) // (hs * HEAD_DIM)
    per_group = lambda rows, cols: pl.BlockSpec((None, hs, rows, cols), lambda b, g: (b, g, 0, 0))
    return pl.pallas_call(
        functools.partial(_gdn_scan_body, seq=seq),
        grid=(batch, DN_HEADS // hs),
        in_specs=[
            per_group(seq, HEAD_DIM),
            per_group(2 * seq, HEAD_DIM),
            per_group(HEAD_DIM, seq),
            per_group(seq, PAIR),
            per_group((seq // CHUNK) * SUBLANES, LANES),
            pl.BlockSpec((seq, hs * HEAD_DIM), lambda b, g: (b, zcol + g)),
            pl.BlockSpec((None, 1, HEAD_DIM), lambda b, g: (l, 0, 0)),
        ],
        out_specs=[
            pl.BlockSpec((seq, hs * HEAD_DIM), lambda b, g: (b, g)),
            per_group(HEAD_DIM, HEAD_DIM),
        ],
        out_shape=[
            jax.ShapeDtypeStruct((m, DN_WIDTH), BF16),
            jax.ShapeDtypeStruct((batch, DN_HEADS, HEAD_DIM, HEAD_DIM), F32),
        ],
        scratch_shapes=[pltpu.VMEM((hs, HEAD_DIM, HEAD_DIM), F32)],
        compiler_params=_params(("parallel", "parallel"), 48),
        name="gdn_scan",
    )(u, wq, kdt, qk, gl, proj, dnw)


def _gdn_sample_body(qkv_ref, z_ref, gate_ref, cs_ref, s_ref, cw_ref, alog_ref, dtb_ref, dnw_ref, *rest,
                     n_alias):
    dn_ref, sn_ref, cn_ref, q_s, k_s, v_s, eg_s, beta_s, o_s = rest[n_alias:]
    nb = qkv_ref.shape[0]
    x = qkv_ref[...]
    c0, c1, c2 = cs_ref[:, 0, :], cs_ref[:, 1, :], cs_ref[:, 2, :]
    cw = cw_ref[...]
    y = c0 * cw[0:1] + c1 * cw[1:2] + c2 * cw[2:3] + x * cw[3:4]
    y = _silu(y)
    cn_ref[:, 0, :] = c1
    cn_ref[:, 1, :] = c2
    cn_ref[:, 2, :] = x
    for h in range(DN_HEADS):
        lo = h * HEAD_DIM
        q_s[h] = _l2(y[:, lo:lo + HEAD_DIM]) * (HEAD_DIM ** -0.5)
        k_s[h] = _l2(y[:, DN_WIDTH + lo:DN_WIDTH + lo + HEAD_DIM])
        v_s[h] = y[:, 2 * DN_WIDTH + lo:2 * DN_WIDTH + lo + HEAD_DIM]
    beta_s[...] = jax.nn.sigmoid(gate_ref[:, 0:LANES])
    eg_s[...] = jnp.exp(-jnp.exp(alog_ref[...]) * _softplus(gate_ref[:, LANES:2 * LANES] + dtb_ref[...]))

    pad7 = jnp.zeros((SUBLANES - 1, HEAD_DIM), F32)
    pad6 = jnp.zeros((SUBLANES - 2, HEAD_DIM), F32)

    tn_dims = (((0,), (0,)), ((), ()))

    def per_sample(b, carry):
        one = pl.ds(b, 1)
        eg_row = eg_s[one, :]
        beta_row = beta_s[one, :]
        heads = range(DN_HEADS)
        k = [k_s[h, one, :] for h in heads]
        q = [q_s[h, one, :] for h in heads]
        st = [s_ref[b, h] * eg_row[:, h:h + 1] for h in heads]
        r = [_dot(jnp.concatenate([k[h], q[h], pad6], axis=0), st[h]) for h in heads]
        dv = [(v_s[h, one, :] - r[h][0:1]) * beta_row[:, h:h + 1] for h in heads]
        outer = [lax.dot_general(jnp.concatenate([k[h], pad7], axis=0), jnp.concatenate([dv[h], pad7], axis=0),
                                 tn_dims, preferred_element_type=F32) for h in heads]
        for h in heads:
            sn_ref[b, h] = st[h] + outer[h]
            o_s[h, one, :] = r[h][1:2] + jnp.sum(q[h] * k[h], axis=-1, keepdims=True) * dv[h]
        return carry

    lax.fori_loop(0, nb, per_sample, 0)

    z = z_ref[...]
    for h in range(DN_HEADS):
        lo = h * HEAD_DIM
        dn_ref[:, lo:lo + HEAD_DIM] = (_rms(o_s[h], dnw_ref[...]) * _silu(z[:, lo:lo + HEAD_DIM])).astype(BF16)


def _gdn_sample(proj, gate, state_conv, state_delta, conv_w, alog, dtb, dnw, dn_buf, prev, l, row0):
    depth, nseq = state_delta.shape[:2]
    nb = SAMPLE_BLOCK
    base = row0 // nb
    vec = pl.BlockSpec((None, 1, LANES), lambda i: (l, 0, 0))
    delta_spec = pl.BlockSpec((None, nb, DN_HEADS, HEAD_DIM, HEAD_DIM), lambda i: (l, i, 0, 0, 0))
    conv_spec = pl.BlockSpec((None, nb, CONV_W - 1, 3 * DN_WIDTH), lambda i: (l, i, 0, 0))
    inputs = [proj, proj, gate, state_conv, state_delta, conv_w, alog, dtb, dnw, dn_buf]
    in_specs = [
        pl.BlockSpec((nb, 3 * DN_WIDTH), lambda i: (base + i, 0)),
        pl.BlockSpec((nb, DN_WIDTH), lambda i: (base + i, 3)),
        pl.BlockSpec((nb, GATE_COLS), lambda i: (base + i, 0)),
        conv_spec, delta_spec,
        pl.BlockSpec((None, CONV_W, 3 * DN_WIDTH), lambda i: (l, 0, 0)),
        vec, vec, vec,
        _any_spec(),
    ]
    aliases = {9: 0}
    if prev is not None:
        inputs += list(prev)
        in_specs += [_any_spec(), _any_spec()]
        aliases.update({10: 1, 11: 2})
    return pl.pallas_call(
        functools.partial(_gdn_sample_body, n_alias=len(inputs) - 9),
        grid=(nseq // nb,),
        in_specs=in_specs,
        out_specs=[pl.BlockSpec((nb, DN_WIDTH), lambda i: (base + i, 0)), delta_spec, conv_spec],
        out_shape=[
            jax.ShapeDtypeStruct(dn_buf.shape, dn_buf.dtype),
            jax.ShapeDtypeStruct(state_delta.shape, F32),
            jax.ShapeDtypeStruct(state_conv.shape, F32),
        ],
        scratch_shapes=[
            pltpu.VMEM((DN_HEADS, nb, HEAD_DIM), F32),
            pltpu.VMEM((DN_HEADS, nb, HEAD_DIM), F32),
            pltpu.VMEM((DN_HEADS, nb, HEAD_DIM), F32),
            pltpu.VMEM((nb, LANES), F32),
            pltpu.VMEM((nb, LANES), F32),
            pltpu.VMEM((DN_HEADS, nb, HEAD_DIM), F32),
        ],
        input_output_aliases=aliases,
        compiler_params=_params(("parallel",), 48),
        name="gdn_sample",
    )(*inputs)


def _pool_matmul(d, w_ref, sc_ref, o_ref, gi):
    lo = gi * POOL_GROUP
    y = _dot(d.astype(BF16), w_ref[gi]) * sc_ref[:, lo:lo + POOL_GROUP]
    o_ref[:, lo:lo + POOL_GROUP] = y.astype(BF16)


def _pool_prompt_body(p_ref, prev_ref, w_ref, sc_ref, o_ref, *, tiles_per_seq):
    rt = p_ref.shape[0]
    hist = prev_ref.shape[0]
    t = pl.program_id(0) % tiles_per_seq
    cur = p_ref[...]
    prev = jnp.where(t == 0, 0.0, prev_ref[...])
    pos = t * rt + lax.broadcasted_iota(jnp.int32, (rt, POOL_GROUP), 0)
    for gi, win in enumerate(POOL_WINDOWS):
        lo = gi * POOL_GROUP
        x = cur[:, lo:lo + POOL_GROUP]
        s = jnp.concatenate([prev[:, lo:lo + POOL_GROUP], x], axis=0)
        sh = 1
        while sh < win:
            s = s + pltpu.roll(s, sh, axis=0)
            sh *= 2
        cnt = jnp.minimum(pos + 1, win).astype(F32)
        _pool_matmul(s[hist:] / cnt - x, w_ref, sc_ref, o_ref, gi)


def _pool_prompt(proj, w_pool, scale, l, batch, seq):
    m = proj.shape[0]
    rt = min(POOL_ROWS, seq)
    hist = POOL_BUF + 1
    tiles_per_seq = seq // rt
    pcol = (4 * DN_WIDTH) // POOL_WIDTH
    ng = len(POOL_WINDOWS)
    return pl.pallas_call(
        functools.partial(_pool_prompt_body, tiles_per_seq=tiles_per_seq),
        grid=(batch * tiles_per_seq,),
        in_specs=[
            pl.BlockSpec((rt, POOL_WIDTH), lambda r: (r, pcol)),
            pl.BlockSpec((hist, POOL_WIDTH), lambda r: (jnp.maximum(r * (rt // hist) - 1, 0), pcol)),
            pl.BlockSpec((None, ng, POOL_GROUP, POOL_GROUP), lambda r: (l, 0, 0, 0)),
            pl.BlockSpec((None, 1, POOL_WIDTH), lambda r: (l, 0, 0)),
        ],
        out_specs=pl.BlockSpec((rt, POOL_WIDTH), lambda r: (r, 0)),
        out_shape=jax.ShapeDtypeStruct((m, POOL_WIDTH), BF16),
        compiler_params=_params(("parallel",), 32),
        name="pool_prompt",
    )(proj, proj, w_pool, scale)


def _pool_sample_body(p_ref, buf_ref, w_ref, sc_ref, *rest):
    o_ref, nb_ref = rest[-2:]
    cur = p_ref[...]
    for gi, win in enumerate(POOL_WINDOWS):
        lo = gi * POOL_GROUP
        x = cur[:, lo:lo + POOL_GROUP]
        s = x
        for j in range(1, win):
            s = s + buf_ref[:, POOL_BUF - j, lo:lo + POOL_GROUP]
        cnt = float(min(PAST_LEN + 1, win))
        _pool_matmul(s / cnt - x, w_ref, sc_ref, o_ref, gi)
    for j in range(POOL_BUF - 1):
        nb_ref[:, j, :] = buf_ref[:, j + 1, :]
    nb_ref[:, POOL_BUF - 1, :] = cur


def _pool_sample(proj, state_pool, w_pool, scale, pool_buf, prev, l, row0):
    nseq = state_pool.shape[1]
    nb = POOL_SAMPLE_BLOCK
    base = row0 // nb
    pcol = (4 * DN_WIDTH) // POOL_WIDTH
    ng = len(POOL_WINDOWS)
    state_spec = pl.BlockSpec((None, nb, POOL_BUF, POOL_WIDTH), lambda i: (l, i, 0, 0))
    inputs = [proj, state_pool, w_pool, scale, pool_buf]
    in_specs = [
        pl.BlockSpec((nb, POOL_WIDTH), lambda i: (base + i, pcol)),
        state_spec,
        pl.BlockSpec((None, ng, POOL_GROUP, POOL_GROUP), lambda i: (l, 0, 0, 0)),
        pl.BlockSpec((None, 1, POOL_WIDTH), lambda i: (l, 0, 0)),
        _any_spec(),
    ]
    aliases = {4: 0}
    if prev is not None:
        inputs.append(prev)
        in_specs.append(_any_spec())
        aliases[5] = 1
    return pl.pallas_call(
        _pool_sample_body,
        grid=(nseq // nb,),
        in_specs=in_specs,
        out_specs=[pl.BlockSpec((nb, POOL_WIDTH), lambda i: (base + i, 0)), state_spec],
        out_shape=[
            jax.ShapeDtypeStruct(pool_buf.shape, pool_buf.dtype),
            jax.ShapeDtypeStruct(state_pool.shape, F32),
        ],
        input_output_aliases=aliases,
        compiler_params=_params(("parallel",), 32),
        name="pool_sample",
    )(*inputs)


def _outproj_body(dn_ref, pool_ref, w1_ref, w2_ref, h_ref, o_ref):
    o_ref[...] = h_ref[...] + _dot(dn_ref[...], w1_ref[...]) + _dot(pool_ref[...], w2_ref[...])


def _outproj(dn, pool, w_out, h, l):
    m = h.shape[0]
    tm, tn = _row_tile(m), OUTPROJ_COL_TILE
    return pl.pallas_call(
        _outproj_body,
        grid=(m // tm, D_MODEL // tn),
        in_specs=[
            pl.BlockSpec((tm, DN_WIDTH), lambda i, j: (i, 0)),
            pl.BlockSpec((tm, POOL_WIDTH), lambda i, j: (i, 0)),
            pl.BlockSpec((None, DN_WIDTH, tn), lambda i, j: (l, 0, j)),
            pl.BlockSpec((None, POOL_WIDTH, tn), lambda i, j: (l, 1, j)),
            pl.BlockSpec((tm, tn), lambda i, j: (i, j)),
        ],
        out_specs=pl.BlockSpec((tm, tn), lambda i, j: (i, j)),
        out_shape=jax.ShapeDtypeStruct((m, D_MODEL), F32),
        compiler_params=_params(("parallel", "parallel"), 40),
        name="outproj",
    )(dn, pool, w_out, w_out, h)


def _ffn_body(h_ref, nw_ref, wg_ref, wu_ref, wd_ref, nf_ref, o_ref, xn_ref, *, final_norm):
    f = pl.program_id(1)

    @pl.when(f == 0)
    def _():
        x = h_ref[...]
        xn_ref[...] = _rms(x, nw_ref[...]).astype(BF16)
        o_ref[...] = x

    xn = xn_ref[...]
    act = (_silu(_dot(xn, wg_ref[...])) * _dot(xn, wu_ref[...])).astype(BF16)
    o_ref[...] += _dot(act, wd_ref[...])

    if final_norm:
        @pl.when(f == pl.num_programs(1) - 1)
        def _():
            o_ref[...] = _rms(o_ref[...], nf_ref[...])


def _ffn(h, nw, w_gate_up, w_down, nf, l, final_norm):
    m = h.shape[0]
    tm, tf = _row_tile(m), FFN_TILE
    nft = D_FF // tf
    return pl.pallas_call(
        functools.partial(_ffn_body, final_norm=final_norm),
        grid=(m // tm, nft),
        in_specs=[
            pl.BlockSpec((tm, D_MODEL), lambda i, f: (i, 0)),
            pl.BlockSpec((None, 1, D_MODEL), lambda i, f: (l, 0, 0)),
            pl.BlockSpec((None, D_MODEL, tf), lambda i, f: (l, 0, f)),
            pl.BlockSpec((None, D_MODEL, tf), lambda i, f: (l, 0, nft + f)),
            pl.BlockSpec((None, tf, D_MODEL), lambda i, f: (l, f, 0)),
            pl.BlockSpec((1, D_MODEL), lambda i, f: (0, 0)),
        ],
        out_specs=pl.BlockSpec((tm, D_MODEL), lambda i, f: (i, 0)),
        out_shape=jax.ShapeDtypeStruct((m, D_MODEL), F32),
        scratch_shapes=[pltpu.VMEM((tm, D_MODEL), BF16)],
        compiler_params=_params(("parallel", "arbitrary"), 52),
        name="ffn",
    )(h, nw, w_gate_up, w_gate_up, w_down, nf)


def _lane_vecs(x):
    depth, n = x.shape
    return jnp.zeros((depth, 1, LANES), F32).at[:, 0, :n].set(x)


def kernel(x_prompt, x_sample, state_delta, state_conv, state_pool, norm_mix, w_in, conv_w, a_log, dt_bias,
           dn_norm, w_pool, pool_scale, w_out, norm_ffn, w_gate_up, w_down, norm_final):
    batch, seq, _ = x_prompt.shape
    nseq = x_sample.shape[0]
    depth = w_in.shape[0]
    rows_p = batch * seq
    h = jnp.concatenate([x_prompt.reshape(rows_p, D_MODEL), x_sample.reshape(nseq, D_MODEL)], axis=0)

    o1 = 4 * DN_WIDTH
    w_a = w_in[:, :, :o1].astype(BF16)
    w_b = w_in[:, :, o1 + 2 * DN_HEADS:].astype(BF16)
    w_gate = jnp.zeros((depth, D_MODEL, GATE_COLS), BF16)
    w_gate = w_gate.at[:, :, :DN_HEADS].set(w_in[:, :, o1:o1 + DN_HEADS].astype(BF16))
    w_gate = w_gate.at[:, :, LANES:LANES + DN_HEADS].set(w_in[:, :, o1 + DN_HEADS:o1 + 2 * DN_HEADS].astype(BF16))
    w_out_b = w_out.astype(BF16)
    w_gu_b = w_gate_up.astype(BF16)
    w_down_b = w_down.astype(BF16)
    w_pool_b = w_pool.astype(BF16)
    alog, dtb = _lane_vecs(a_log), _lane_vecs(dt_bias)
    dnw = dn_norm.reshape(depth, 1, HEAD_DIM)
    scale = pool_scale.reshape(depth, 1, POOL_WIDTH)
    nmix = norm_mix.reshape(depth, 1, D_MODEL)
    nffn = norm_ffn.reshape(depth, 1, D_MODEL)
    nfin = norm_final.reshape(1, D_MODEL)

    delta_p, conv_p, pool_p = [], [], []
    sample_states = None
    pool_states = None
    for l in range(depth):
        proj, gate = _inproj(h, nmix, w_a, w_b, w_gate, l)

        beta, gc = _gates(gate, alog, dtb, l, batch, seq)
        u, wq, kdt, qk, gl = _gdn_prep(proj, beta, gc, conv_w, l, batch, seq)
        dn, s_p = _gdn_scan(u, wq, kdt, qk, gl, proj, dnw, l, batch, seq)
        dn, *sample_states = _gdn_sample(proj, gate, state_conv, state_delta, conv_w, alog, dtb, dnw, dn,
                                         sample_states, l, rows_p)
        pool = _pool_prompt(proj, w_pool_b, scale, l, batch, seq)
        pool, pool_states = _pool_sample(proj, state_pool, w_pool_b, scale, pool, pool_states, l, rows_p)

        delta_p.append(s_p)
        conv_p.append(jnp.stack([proj[(b + 1) * seq - (CONV_W - 1):(b + 1) * seq, :3 * DN_WIDTH]
                                 for b in range(batch)]))
        pool_p.append(jnp.stack([proj[(b + 1) * seq - POOL_BUF:(b + 1) * seq, o1:] for b in range(batch)]))

        h = _outproj(dn, pool, w_out_b, h, l)
        h = _ffn(h, nffn, w_gu_b, w_down_b, nfin, l, l == depth - 1)

    delta_s, conv_s = sample_states
    y_prompt = h[:rows_p].reshape(batch, seq, D_MODEL)
    y_sample = h[rows_p:].reshape(nseq, 1, D_MODEL)
    return (y_prompt, y_sample, jnp.stack(delta_p), jnp.stack(conv_p), jnp.stack(pool_p),
            delta_s, conv_s, pool_states)
```

```python
import functools

import jax
import jax.numpy as jnp
from jax import lax
from jax.experimental import pallas as pl
from jax.experimental.pallas import tpu as pltpu

F32 = jnp.float32
BF16 = jnp.bfloat16

D_MODEL = 2048
DN_WIDTH = 1024
DN_HEADS = 8
HEAD_DIM = 128
CONV_W = 4
CHUNK = 64
POOL_WIDTH = 1024
POOL_WINDOWS = (2, 4, 8, 16)
POOL_GROUP = 256
POOL_BUF = 15
D_FF = 5632
EPS = 1e-6
PAST_LEN = 16384

MAIN_COLS = 4 * DN_WIDTH + POOL_WIDTH
GATE_COLS = 256
LANES = 128
SUBLANES = 8
PAIR = 2 * CHUNK

ROW_TILES = (1040, 640, 512, 256, 128)
INPROJ_COL_TILE = 1024
OUTPROJ_ROW_TILE = 640
OUTPROJ_COL_TILE = 2048
FFN_TILE = 512
CONV_ROWS = 256
PREP_ROWS = 1024
PREP_UNROLL = 8
SCAN_HEADS = 4
POOL_ROWS = 512
SAMPLE_BLOCK = 16
SAMPLE_UNROLL = 2
POOL_SAMPLE_BLOCK = 32
MIB = 1024 * 1024


def _params(semantics, vmem_mib):
    return pltpu.CompilerParams(dimension_semantics=semantics, vmem_limit_bytes=vmem_mib * MIB)


def _silu(x):
    return x * jax.nn.sigmoid(x)


def _softplus(x):
    return jnp.maximum(x, 0.0) + jnp.log1p(jnp.exp(-jnp.abs(x)))


def _rms(x, w):
    return x * lax.rsqrt(jnp.mean(x * x, axis=-1, keepdims=True) + EPS) * w


def _l2(x):
    return x * lax.rsqrt(jnp.sum(x * x, axis=-1, keepdims=True) + EPS)


def _dot(a, b):
    return jnp.dot(a, b, preferred_element_type=F32)


def _row_tile(m, largest=ROW_TILES[0]):
    return next(t for t in ROW_TILES if t <= largest and m % t == 0)


def _any_spec():
    return pl.BlockSpec(memory_space=pl.ANY)


def _inproj_body(h_ref, nw_ref, wa_ref, wb_ref, wg_ref, proj_ref, gate_ref, xn_ref, *, n_a):
    j = pl.program_id(1)

    @pl.when(j == 0)
    def _():
        xn = _rms(h_ref[...], nw_ref[...]).astype(BF16)
        xn_ref[...] = xn
        gate_ref[...] = _dot(xn, wg_ref[...])

    @pl.when(j < n_a)
    def _():
        proj_ref[...] = _dot(xn_ref[...], wa_ref[...])

    @pl.when(j >= n_a)
    def _():
        proj_ref[...] = _dot(xn_ref[...], wb_ref[...])


def _inproj(h, nw, w_a, w_b, w_gate, l):
    m = h.shape[0]
    tm, tn = _row_tile(m), INPROJ_COL_TILE
    n_a = (MAIN_COLS - w_b.shape[-1]) // tn
    return pl.pallas_call(
        functools.partial(_inproj_body, n_a=n_a),
        grid=(m // tm, MAIN_COLS // tn),
        in_specs=[
            pl.BlockSpec((tm, D_MODEL), lambda i, j: (i, 0)),
            pl.BlockSpec((None, 1, D_MODEL), lambda i, j: (l, 0, 0)),
            pl.BlockSpec((None, D_MODEL, tn), lambda i, j: (l, 0, jnp.minimum(j, n_a - 1))),
            pl.BlockSpec((None, D_MODEL, tn), lambda i, j: (l, 0, jnp.maximum(j - n_a, 0))),
            pl.BlockSpec((None, D_MODEL, GATE_COLS), lambda i, j: (l, 0, 0)),
        ],
        out_specs=[
            pl.BlockSpec((tm, tn), lambda i, j: (i, j)),
            pl.BlockSpec((tm, GATE_COLS), lambda i, j: (i, 0)),
        ],
        out_shape=[
            jax.ShapeDtypeStruct((m, MAIN_COLS), F32),
            jax.ShapeDtypeStruct((m, GATE_COLS), F32),
        ],
        scratch_shapes=[pltpu.VMEM((tm, D_MODEL), BF16)],
        compiler_params=_params(("parallel", "arbitrary"), 54),
        name="inproj",
    )(h, nw, w_a, w_b, w_gate)


def _gates_body(gate_ref, alog_ref, dtb_ref, beta_ref, gc_ref):
    beta_ref[...] = jax.nn.sigmoid(gate_ref[:, 0:LANES])
    g = -jnp.exp(alog_ref[...]) * _softplus(gate_ref[:, LANES:2 * LANES] + dtb_ref[...])
    pos = lax.broadcasted_iota(jnp.int32, g.shape, 0) & (CHUNK - 1)
    sh = 1
    while sh < CHUNK:
        g = g + jnp.where(pos >= sh, pltpu.roll(g, sh, axis=0), 0.0)
        sh *= 2
    gc_ref[...] = g


def _gates(gate, alog, dtb, l, batch, seq):
    vec = pl.BlockSpec((None, 1, LANES), lambda b: (l, 0, 0))
    return pl.pallas_call(
        _gates_body,
        grid=(batch,),
        in_specs=[pl.BlockSpec((seq, GATE_COLS), lambda b: (b, 0)), vec, vec],
        out_specs=[pl.BlockSpec((seq, LANES), lambda b: (b, 0))] * 2,
        out_shape=[jax.ShapeDtypeStruct((batch * seq, LANES), F32)] * 2,
        compiler_params=_params(("parallel",), 32),
        name="gates",
    )(gate, alog, dtb)


def _gdn_prep_body(q_ref, k_ref, v_ref, qh_ref, kh_ref, vh_ref, cwq_ref, cwk_ref, cwv_ref, beta_ref, gc_ref,
                   u_ref, wq_ref, kdt_ref, qk_ref, gl_ref, qs, ks, vs):
    head = pl.program_id(1)
    r = pl.program_id(2)
    rp = q_ref.shape[0]
    rb = min(CONV_ROWS, rp)

    def conv_silu(x_ref, halo_ref, cw_ref, i):
        cw = cw_ref[...]
        if i == 0:
            halo = jnp.where(r > 0, halo_ref[...], 0.0)
            ext = jnp.concatenate([halo, x_ref[0:rb, :]], axis=0)
            off = SUBLANES - (CONV_W - 1)
            taps = [ext[off + t:off + t + rb] for t in range(CONV_W)]
        else:
            off = i * rb - (CONV_W - 1)
            taps = [x_ref[off + t:off + t + rb, :] for t in range(CONV_W)]
        y = taps[0] * cw[0:1]
        for t in range(1, CONV_W):
            y = y + taps[t] * cw[t:t + 1]
        return _silu(y)

    for i in range(rp // rb):
        rows = slice(i * rb, (i + 1) * rb)
        qs[rows, :] = _l2(conv_silu(q_ref, qh_ref, cwq_ref, i)) * (HEAD_DIM ** -0.5)
        ks[rows, :] = _l2(conv_silu(k_ref, kh_ref, cwk_ref, i))
        vs[rows, :] = conv_silu(v_ref, vh_ref, cwv_ref, i)

    head_lane = lax.broadcasted_iota(jnp.int32, (PAIR, LANES), 1) == head
    ri = lax.broadcasted_iota(jnp.int32, (PAIR, PAIR), 0)
    ci = lax.broadcasted_iota(jnp.int32, (PAIR, PAIR), 1)
    same = (ri >= CHUNK) == (ci >= CHUNK)
    tril = same & (ri >= ci)
    strict = same & (ri > ci)
    eye = (ri == ci).astype(F32)
    first = ri < CHUNK

    def lane_bcast(x):
        return jnp.broadcast_to(jnp.sum(jnp.where(head_lane, x, 0.0), axis=1, keepdims=True), (PAIR, LANES))

    n_pairs = rp // PAIR
    unroll = min(PREP_UNROLL, n_pairs)
    nt_dims = (((1,), (1,)), ((), ()))

    def group(g, carry):
        js = [g * unroll + t for t in range(unroll)]
        rows = [pl.ds(pl.multiple_of(j * PAIR, PAIR), PAIR) for j in js]
        q = [qs[r, :] for r in rows]
        k = [ks[r, :] for r in rows]
        bb = [lane_bcast(beta_ref[r, :]) for r in rows]
        gc = [lane_bcast(gc_ref[r, :]) for r in rows]
        decay = [jnp.where(tril, jnp.exp(jnp.where(tril, x - x.T, 0.0)), 0.0) for x in gc]
        kb = [a * b for a, b in zip(k, bb)]
        a1 = [lax.dot_general(jnp.concatenate([a, b], axis=0), c, nt_dims, preferred_element_type=F32)
              for a, b, c in zip(q, kb, k)]
        for r, a, d in zip(rows, a1, decay):
            qk_ref[r, :] = (a[:PAIR] * d).astype(BF16)
        m = [jnp.where(strict, a[PAIR:] * d, 0.0) for a, d in zip(a1, decay)]
        inv = [eye - x for x in m]
        mp = m
        sh = 1
        while 2 * sh < CHUNK:
            sh *= 2
            mp = [_dot(x, x) for x in mp]
            inv = [a + _dot(a, x) for a, x in zip(inv, mp)]
        eg = [jnp.exp(x) for x in gc]
        uw = [_dot(a, jnp.concatenate([vs[r, :] * b, c * e], axis=1))
              for a, r, b, c, e in zip(inv, rows, bb, kb, eg)]
        for j, r, x, a, e, kk, gcx in zip(js, rows, uw, q, eg, k, gc):
            u_ref[r, :] = x[:, :HEAD_DIM]
            w = x[:, HEAD_DIM:]
            qg = a * e
            wq_ref[pl.ds(pl.multiple_of(j * 2 * PAIR, 2 * PAIR), 2 * PAIR), :] = jnp.concatenate(
                [w[:CHUNK], qg[:CHUNK], w[CHUNK:], qg[CHUNK:]], axis=0).astype(BF16)
            gl0 = gcx[CHUNK - 1:CHUNK, :]
            gl1 = gcx[PAIR - 1:PAIR, :]
            kd = kk * jnp.exp(jnp.where(first, gl0, gl1) - gcx)
            kdt_ref[:, r] = kd.T.astype(BF16)
            gl_ref[pl.ds(pl.multiple_of(j * 2 * SUBLANES, 2 * SUBLANES), 2 * SUBLANES), :] = jnp.concatenate(
                [jnp.broadcast_to(jnp.exp(gl0), (SUBLANES, LANES)),
                 jnp.broadcast_to(jnp.exp(gl1), (SUBLANES, LANES))], axis=0)
        return carry

    lax.fori_loop(0, n_pairs // unroll, group, 0)


def _gdn_prep(proj, beta, gc, conv_w, l, batch, seq):
    nh = DN_HEADS
    rp = min(PREP_ROWS, seq)
    nr = seq // rp
    n_chunks = seq // CHUNK
    col = lambda base: (lambda b, h, r: (b * nr + r, base + h))
    halo = lambda base: (lambda b, h, r: (jnp.maximum((b * nr + r) * (rp // SUBLANES) - 1, 0), base + h))
    cwcol = lambda base: (lambda b, h, r: (l, 0, base + h))
    gate_spec = pl.BlockSpec((rp, LANES), lambda b, h, r: (b * nr + r, 0))
    per_head = lambda rows, cols: pl.BlockSpec((None, None, rows, cols), lambda b, h, r: (b, h, r, 0))
    return pl.pallas_call(
        _gdn_prep_body,
        grid=(batch, nh, nr),
        in_specs=[
            pl.BlockSpec((rp, HEAD_DIM), col(0)),
            pl.BlockSpec((rp, HEAD_DIM), col(nh)),
            pl.BlockSpec((rp, HEAD_DIM), col(2 * nh)),
            pl.BlockSpec((SUBLANES, HEAD_DIM), halo(0)),
            pl.BlockSpec((SUBLANES, HEAD_DIM), halo(nh)),
            pl.BlockSpec((SUBLANES, HEAD_DIM), halo(2 * nh)),
            pl.BlockSpec((None, CONV_W, HEAD_DIM), cwcol(0)),
            pl.BlockSpec((None, CONV_W, HEAD_DIM), cwcol(nh)),
            pl.BlockSpec((None, CONV_W, HEAD_DIM), cwcol(2 * nh)),
            gate_spec, gate_spec,
        ],
        out_specs=[
            per_head(rp, HEAD_DIM),
            per_head(2 * rp, HEAD_DIM),
            pl.BlockSpec((None, None, HEAD_DIM, rp), lambda b, h, r: (b, h, 0, r)),
            per_head(rp, HEAD_DIM),
            per_head((rp // CHUNK) * SUBLANES, LANES),
        ],
        out_shape=[
            jax.ShapeDtypeStruct((batch, nh, seq, HEAD_DIM), F32),
            jax.ShapeDtypeStruct((batch, nh, 2 * seq, HEAD_DIM), BF16),
            jax.ShapeDtypeStruct((batch, nh, HEAD_DIM, seq), BF16),
            jax.ShapeDtypeStruct((batch, nh, seq, PAIR), BF16),
            jax.ShapeDtypeStruct((batch, nh, n_chunks * SUBLANES, LANES), F32),
        ],
        scratch_shapes=[pltpu.VMEM((rp, HEAD_DIM), F32)] * 3,
        compiler_params=_params(("parallel", "parallel", "parallel"), 32),
        name="gdn_prep",
    )(proj, proj, proj, proj, proj, proj, conv_w, conv_w, conv_w, beta, gc)


def _gdn_scan_body(u_ref, wq_ref, kdt_ref, qk_ref, gl_ref, z_ref, dnw_ref, o_ref, s_out_ref, st_s, *, seq):
    hs = u_ref.shape[0]
    st_s[...] = jnp.zeros(st_s.shape, F32)
    zeros = jnp.zeros((CHUNK, HEAD_DIM), BF16)

    def pair(p, carry):
        kcols = pl.ds(pl.multiple_of(p * PAIR, PAIR), PAIR)
        heads = range(hs)
        st = [st_s[hh] for hh in heads]
        for s in range(2):
            c = 2 * p + s
            rows = pl.ds(pl.multiple_of(c * CHUNK, CHUNK), CHUNK)
            wrows = pl.ds(pl.multiple_of(c * PAIR, PAIR), PAIR)
            grows = pl.ds(pl.multiple_of(c * SUBLANES, SUBLANES), SUBLANES)
            r = [_dot(wq_ref[hh, wrows, :], st[hh].astype(BF16)) for hh in heads]
            vnew = [(u_ref[hh, rows, :] - r[hh][:CHUNK]).astype(BF16) for hh in heads]
            vpad = [jnp.concatenate([x, zeros] if s == 0 else [zeros, x], axis=0) for x in vnew]
            st = [st[hh] * gl_ref[hh, grows, :][0:1, :] + _dot(kdt_ref[hh, :, kcols], vpad[hh]) for hh in heads]
            o = [r[hh][CHUNK:] + _dot(qk_ref[hh, rows, :], vpad[hh]) for hh in heads]
            for hh in heads:
                lo = hh * HEAD_DIM
                o_ref[rows, lo:lo + HEAD_DIM] = (
                    _rms(o[hh], dnw_ref[...]) * _silu(z_ref[rows, lo:lo + HEAD_DIM])).astype(BF16)
        for hh in heads:
            st_s[hh] = st[hh]
        return carry

    lax.fori_loop(0, seq // PAIR, pair, 0)
    s_out_ref[...] = st_s[...]


def _gdn_scan(u, wq, kdt, qk, gl, proj, dnw, l, batch, seq):
    m = proj.shape[0]
    hs = SCAN_HEADS
    zcol = (3 * DN_WIDTH) // (hs * HEAD_DIM)
    per_group = lambda rows, cols: pl.BlockSpec((None, hs, rows, cols), lambda b, g: (b, g, 0, 0))
    return pl.pallas_call(
        functools.partial(_gdn_scan_body, seq=seq),
        grid=(batch, DN_HEADS // hs),
        in_specs=[
            per_group(seq, HEAD_DIM),
            per_group(2 * seq, HEAD_DIM),
            per_group(HEAD_DIM, seq),
            per_group(seq, PAIR),
            per_group((seq // CHUNK) * SUBLANES, LANES),
            pl.BlockSpec((seq, hs * HEAD_DIM), lambda b, g: (b, zcol + g)),
            pl.BlockSpec((None, 1, HEAD_DIM), lambda b, g: (l, 0, 0)),
        ],
        out_specs=[
            pl.BlockSpec((seq, hs * HEAD_DIM), lambda b, g: (b, g)),
            per_group(HEAD_DIM, HEAD_DIM),
        ],
        out_shape=[
            jax.ShapeDtypeStruct((m, DN_WIDTH), BF16),
            jax.ShapeDtypeStruct((batch, DN_HEADS, HEAD_DIM, HEAD_DIM), F32),
        ],
        scratch_shapes=[pltpu.VMEM((hs, HEAD_DIM, HEAD_DIM), F32)],
        compiler_params=_params(("parallel", "parallel"), 48),
        name="gdn_scan",
    )(u, wq, kdt, qk, gl, proj, dnw)


def _gdn_sample_body(qkv_ref, z_ref, gate_ref, cs_ref, s_ref, cw_ref, alog_ref, dtb_ref, dnw_ref, *rest,
                     n_alias):
    dn_ref, sn_ref, cn_ref, q_s, k_s, v_s, eg_s, beta_s, o_s = rest[n_alias:]
    nb = qkv_ref.shape[0]
    x = qkv_ref[...]
    c0, c1, c2 = cs_ref[:, 0, :], cs_ref[:, 1, :], cs_ref[:, 2, :]
    cw = cw_ref[...]
    y = c0 * cw[0:1] + c1 * cw[1:2] + c2 * cw[2:3] + x * cw[3:4]
    y = _silu(y)
    cn_ref[:, 0, :] = c1
    cn_ref[:, 1, :] = c2
    cn_ref[:, 2, :] = x
    for h in range(DN_HEADS):
        lo = h * HEAD_DIM
        q_s[h] = _l2(y[:, lo:lo + HEAD_DIM]) * (HEAD_DIM ** -0.5)
        k_s[h] = _l2(y[:, DN_WIDTH + lo:DN_WIDTH + lo + HEAD_DIM])
        v_s[h] = y[:, 2 * DN_WIDTH + lo:2 * DN_WIDTH + lo + HEAD_DIM]
    beta_s[...] = jax.nn.sigmoid(gate_ref[:, 0:LANES])
    eg_s[...] = jnp.exp(-jnp.exp(alog_ref[...]) * _softplus(gate_ref[:, LANES:2 * LANES] + dtb_ref[...]))

    pad7 = jnp.zeros((SUBLANES - 1, HEAD_DIM), F32)
    pad6 = jnp.zeros((SUBLANES - 2, HEAD_DIM), F32)

    tn_dims = (((0,), (0,)), ((), ()))

    def per_samples(i, carry):
        units = [(i * SAMPLE_UNROLL + t, h) for t in range(SAMPLE_UNROLL) for h in range(DN_HEADS)]
        k = [k_s[h, pl.ds(b, 1), :] for b, h in units]
        q = [q_s[h, pl.ds(b, 1), :] for b, h in units]
        st = [s_ref[b, h] * eg_s[pl.ds(b, 1), :][:, h:h + 1] for b, h in units]
        r = [_dot(jnp.concatenate([kk, qq, pad6], axis=0), s) for kk, qq, s in zip(k, q, st)]
        dv = [(v_s[h, pl.ds(b, 1), :] - x[0:1]) * beta_s[pl.ds(b, 1), :][:, h:h + 1]
              for (b, h), x in zip(units, r)]
        outer = [lax.dot_general(jnp.concatenate([kk, pad7], axis=0), jnp.concatenate([d, pad7], axis=0),
                                 tn_dims, preferred_element_type=F32) for kk, d in zip(k, dv)]
        for (b, h), s, x, o, kk, qq, d in zip(units, st, outer, r, k, q, dv):
            sn_ref[b, h] = s + x
            o_s[h, pl.ds(b, 1), :] = o[1:2] + jnp.sum(qq * kk, axis=-1, keepdims=True) * d
        return carry

    lax.fori_loop(0, nb // SAMPLE_UNROLL, per_samples, 0)

    z = z_ref[...]
    for h in range(DN_HEADS):
        lo = h * HEAD_DIM
        dn_ref[:, lo:lo + HEAD_DIM] = (_rms(o_s[h], dnw_ref[...]) * _silu(z[:, lo:lo + HEAD_DIM])).astype(BF16)


def _gdn_sample(proj, gate, state_conv, state_delta, conv_w, alog, dtb, dnw, dn_buf, prev, l, row0):
    depth, nseq = state_delta.shape[:2]
    nb = SAMPLE_BLOCK
    base = row0 // nb
    vec = pl.BlockSpec((None, 1, LANES), lambda i: (l, 0, 0))
    delta_spec = pl.BlockSpec((None, nb, DN_HEADS, HEAD_DIM, HEAD_DIM), lambda i: (l, i, 0, 0, 0))
    conv_spec = pl.BlockSpec((None, nb, CONV_W - 1, 3 * DN_WIDTH), lambda i: (l, i, 0, 0))
    inputs = [proj, proj, gate, state_conv, state_delta, conv_w, alog, dtb, dnw, dn_buf]
    in_specs = [
        pl.BlockSpec((nb, 3 * DN_WIDTH), lambda i: (base + i, 0)),
        pl.BlockSpec((nb, DN_WIDTH), lambda i: (base + i, 3)),
        pl.BlockSpec((nb, GATE_COLS), lambda i: (base + i, 0)),
        conv_spec, delta_spec,
        pl.BlockSpec((None, CONV_W, 3 * DN_WIDTH), lambda i: (l, 0, 0)),
        vec, vec, vec,
        _any_spec(),
    ]
    aliases = {9: 0}
    if prev is not None:
        inputs += list(prev)
        in_specs += [_any_spec(), _any_spec()]
        aliases.update({10: 1, 11: 2})
    return pl.pallas_call(
        functools.partial(_gdn_sample_body, n_alias=len(inputs) - 9),
        grid=(nseq // nb,),
        in_specs=in_specs,
        out_specs=[pl.BlockSpec((nb, DN_WIDTH), lambda i: (base + i, 0)), delta_spec, conv_spec],
        out_shape=[
            jax.ShapeDtypeStruct(dn_buf.shape, dn_buf.dtype),
            jax.ShapeDtypeStruct(state_delta.shape, F32),
            jax.ShapeDtypeStruct(state_conv.shape, F32),
        ],
        scratch_shapes=[
            pltpu.VMEM((DN_HEADS, nb, HEAD_DIM), F32),
            pltpu.VMEM((DN_HEADS, nb, HEAD_DIM), F32),
            pltpu.VMEM((DN_HEADS, nb, HEAD_DIM), F32),
            pltpu.VMEM((nb, LANES), F32),
            pltpu.VMEM((nb, LANES), F32),
            pltpu.VMEM((DN_HEADS, nb, HEAD_DIM), F32),
        ],
        input_output_aliases=aliases,
        compiler_params=_params(("parallel",), 48),
        name="gdn_sample",
    )(*inputs)


def _pool_matmul(d, w_ref, sc_ref, o_ref, gi):
    lo = gi * POOL_GROUP
    y = _dot(d.astype(BF16), w_ref[gi]) * sc_ref[:, lo:lo + POOL_GROUP]
    o_ref[:, lo:lo + POOL_GROUP] = y.astype(BF16)


def _pool_prompt_body(p_ref, prev_ref, w_ref, sc_ref, o_ref, *, tiles_per_seq):
    rt = p_ref.shape[0]
    hist = prev_ref.shape[0]
    t = pl.program_id(0) % tiles_per_seq
    cur = p_ref[...]
    prev = jnp.where(t == 0, 0.0, prev_ref[...])
    pos = t * rt + lax.broadcasted_iota(jnp.int32, (rt, POOL_GROUP), 0)
    for gi, win in enumerate(POOL_WINDOWS):
        lo = gi * POOL_GROUP
        x = cur[:, lo:lo + POOL_GROUP]
        s = jnp.concatenate([prev[:, lo:lo + POOL_GROUP], x], axis=0)
        sh = 1
        while sh < win:
            s = s + pltpu.roll(s, sh, axis=0)
            sh *= 2
        cnt = jnp.minimum(pos + 1, win).astype(F32)
        _pool_matmul(s[hist:] / cnt - x, w_ref, sc_ref, o_ref, gi)


def _pool_prompt(proj, w_pool, scale, l, batch, seq):
    m = proj.shape[0]
    rt = min(POOL_ROWS, seq)
    hist = POOL_BUF + 1
    tiles_per_seq = seq // rt
    pcol = (4 * DN_WIDTH) // POOL_WIDTH
    ng = len(POOL_WINDOWS)
    return pl.pallas_call(
        functools.partial(_pool_prompt_body, tiles_per_seq=tiles_per_seq),
        grid=(batch * tiles_per_seq,),
        in_specs=[
            pl.BlockSpec((rt, POOL_WIDTH), lambda r: (r, pcol)),
            pl.BlockSpec((hist, POOL_WIDTH), lambda r: (jnp.maximum(r * (rt // hist) - 1, 0), pcol)),
            pl.BlockSpec((None, ng, POOL_GROUP, POOL_GROUP), lambda r: (l, 0, 0, 0)),
            pl.BlockSpec((None, 1, POOL_WIDTH), lambda r: (l, 0, 0)),
        ],
        out_specs=pl.BlockSpec((rt, POOL_WIDTH), lambda r: (r, 0)),
        out_shape=jax.ShapeDtypeStruct((m, POOL_WIDTH), BF16),
        compiler_params=_params(("parallel",), 32),
        name="pool_prompt",
    )(proj, proj, w_pool, scale)


def _pool_sample_body(p_ref, buf_ref, w_ref, sc_ref, *rest):
    o_ref, nb_ref = rest[-2:]
    cur = p_ref[...]
    for gi, win in enumerate(POOL_WINDOWS):
        lo = gi * POOL_GROUP
        x = cur[:, lo:lo + POOL_GROUP]
        s = x
        for j in range(1, win):
            s = s + buf_ref[:, POOL_BUF - j, lo:lo + POOL_GROUP]
        cnt = float(min(PAST_LEN + 1, win))
        _pool_matmul(s / cnt - x, w_ref, sc_ref, o_ref, gi)
    for j in range(POOL_BUF - 1):
        nb_ref[:, j, :] = buf_ref[:, j + 1, :]
    nb_ref[:, POOL_BUF - 1, :] = cur


def _pool_sample(proj, state_pool, w_pool, scale, pool_buf, prev, l, row0):
    nseq = state_pool.shape[1]
    nb = POOL_SAMPLE_BLOCK
    base = row0 // nb
    pcol = (4 * DN_WIDTH) // POOL_WIDTH
    ng = len(POOL_WINDOWS)
    state_spec = pl.BlockSpec((None, nb, POOL_BUF, POOL_WIDTH), lambda i: (l, i, 0, 0))
    inputs = [proj, state_pool, w_pool, scale, pool_buf]
    in_specs = [
        pl.BlockSpec((nb, POOL_WIDTH), lambda i: (base + i, pcol)),
        state_spec,
        pl.BlockSpec((None, ng, POOL_GROUP, POOL_GROUP), lambda i: (l, 0, 0, 0)),
        pl.BlockSpec((None, 1, POOL_WIDTH), lambda i: (l, 0, 0)),
        _any_spec(),
    ]
    aliases = {4: 0}
    if prev is not None:
        inputs.append(prev)
        in_specs.append(_any_spec())
        aliases[5] = 1
    return pl.pallas_call(
        _pool_sample_body,
        grid=(nseq // nb,),
        in_specs=in_specs,
        out_specs=[pl.BlockSpec((nb, POOL_WIDTH), lambda i: (base + i, 0)), state_spec],
        out_shape=[
            jax.ShapeDtypeStruct(pool_buf.shape, pool_buf.dtype),
            jax.ShapeDtypeStruct(state_pool.shape, F32),
        ],
        input_output_aliases=aliases,
        compiler_params=_params(("parallel",), 32),
        name="pool_sample",
    )(*inputs)


def _outproj_body(dn_ref, pool_ref, w1_ref, w2_ref, h_ref, o_ref):
    o_ref[...] = h_ref[...] + _dot(dn_ref[...], w1_ref[...]) + _dot(pool_ref[...], w2_ref[...])


def _outproj(dn, pool, w_out, h, l):
    m = h.shape[0]
    tm, tn = _row_tile(m, OUTPROJ_ROW_TILE), OUTPROJ_COL_TILE
    return pl.pallas_call(
        _outproj_body,
        grid=(m // tm, D_MODEL // tn),
        in_specs=[
            pl.BlockSpec((tm, DN_WIDTH), lambda i, j: (i, 0)),
            pl.BlockSpec((tm, POOL_WIDTH), lambda i, j: (i, 0)),
            pl.BlockSpec((None, DN_WIDTH, tn), lambda i, j: (l, 0, j)),
            pl.BlockSpec((None, POOL_WIDTH, tn), lambda i, j: (l, 1, j)),
            pl.BlockSpec((tm, tn), lambda i, j: (i, j)),
        ],
        out_specs=pl.BlockSpec((tm, tn), lambda i, j: (i, j)),
        out_shape=jax.ShapeDtypeStruct((m, D_MODEL), F32),
        compiler_params=_params(("parallel", "parallel"), 54),
        name="outproj",
    )(dn, pool, w_out, w_out, h)


def _ffn_body(h_ref, nw_ref, wg_ref, wu_ref, wd_ref, nf_ref, o_ref, xn_ref, *, final_norm):
    f = pl.program_id(1)

    @pl.when(f == 0)
    def _():
        x = h_ref[...]
        xn_ref[...] = _rms(x, nw_ref[...]).astype(BF16)
        o_ref[...] = x

    xn = xn_ref[...]
    act = (_silu(_dot(xn, wg_ref[...])) * _dot(xn, wu_ref[...])).astype(BF16)
    o_ref[...] += _dot(act, wd_ref[...])

    if final_norm:
        @pl.when(f == pl.num_programs(1) - 1)
        def _():
            o_ref[...] = _rms(o_ref[...], nf_ref[...])


def _ffn(h, nw, w_gate_up, w_down, nf, l, final_norm):
    m = h.shape[0]
    tm, tf = _row_tile(m), FFN_TILE
    nft = D_FF // tf
    return pl.pallas_call(
        functools.partial(_ffn_body, final_norm=final_norm),
        grid=(m // tm, nft),
        in_specs=[
            pl.BlockSpec((tm, D_MODEL), lambda i, f: (i, 0)),
            pl.BlockSpec((None, 1, D_MODEL), lambda i, f: (l, 0, 0)),
            pl.BlockSpec((None, D_MODEL, tf), lambda i, f: (l, 0, f)),
            pl.BlockSpec((None, D_MODEL, tf), lambda i, f: (l, 0, nft + f)),
            pl.BlockSpec((None, tf, D_MODEL), lambda i, f: (l, f, 0)),
            pl.BlockSpec((1, D_MODEL), lambda i, f: (0, 0)),
        ],
        out_specs=pl.BlockSpec((tm, D_MODEL), lambda i, f: (i, 0)),
        out_shape=jax.ShapeDtypeStruct((m, D_MODEL), F32),
        scratch_shapes=[pltpu.VMEM((tm, D_MODEL), BF16)],
        compiler_params=_params(("parallel", "arbitrary"), 56),
        name="ffn",
    )(h, nw, w_gate_up, w_gate_up, w_down, nf)


def _lane_vecs(x):
    depth, n = x.shape
    return jnp.zeros((depth, 1, LANES), F32).at[:, 0, :n].set(x)


def kernel(x_prompt, x_sample, state_delta, state_conv, state_pool, norm_mix, w_in, conv_w, a_log, dt_bias,
           dn_norm, w_pool, pool_scale, w_out, norm_ffn, w_gate_up, w_down, norm_final):
    batch, seq, _ = x_prompt.shape
    nseq = x_sample.shape[0]
    depth = w_in.shape[0]
    rows_p = batch * seq
    h = jnp.concatenate([x_prompt.reshape(rows_p, D_MODEL), x_sample.reshape(nseq, D_MODEL)], axis=0)

    o1 = 4 * DN_WIDTH
    w_a = w_in.astype(BF16)
    w_b = w_a[:, :, o1 + 2 * DN_HEADS:]
    w_gate = jnp.zeros((depth, D_MODEL, GATE_COLS), BF16)
    w_gate = w_gate.at[:, :, :DN_HEADS].set(w_a[:, :, o1:o1 + DN_HEADS])
    w_gate = w_gate.at[:, :, LANES:LANES + DN_HEADS].set(w_a[:, :, o1 + DN_HEADS:o1 + 2 * DN_HEADS])
    w_out_b = w_out.astype(BF16)
    w_gu_b = w_gate_up.astype(BF16)
    w_down_b = w_down.astype(BF16)
    w_pool_b = w_pool.astype(BF16)
    alog, dtb = _lane_vecs(a_log), _lane_vecs(dt_bias)
    dnw = dn_norm.reshape(depth, 1, HEAD_DIM)
    scale = pool_scale.reshape(depth, 1, POOL_WIDTH)
    nmix = norm_mix.reshape(depth, 1, D_MODEL)
    nffn = norm_ffn.reshape(depth, 1, D_MODEL)
    nfin = norm_final.reshape(1, D_MODEL)

    delta_p, conv_p, pool_p = [], [], []
    sample_states = None
    pool_states = None
    for l in range(depth):
        proj, gate = _inproj(h, nmix, w_a, w_b, w_gate, l)

        beta, gc = _gates(gate, alog, dtb, l, batch, seq)
        u, wq, kdt, qk, gl = _gdn_prep(proj, beta, gc, conv_w, l, batch, seq)
        dn, s_p = _gdn_scan(u, wq, kdt, qk, gl, proj, dnw, l, batch, seq)
        dn, *sample_states = _gdn_sample(proj, gate, state_conv, state_delta, conv_w, alog, dtb, dnw, dn,
                                         sample_states, l, rows_p)
        pool = _pool_prompt(proj, w_pool_b, scale, l, batch, seq)
        pool, pool_states = _pool_sample(proj, state_pool, w_pool_b, scale, pool, pool_states, l, rows_p)

        delta_p.append(s_p)
        conv_p.append(jnp.stack([proj[(b + 1) * seq - (CONV_W - 1):(b + 1) * seq, :3 * DN_WIDTH]
                                 for b in range(batch)]))
        pool_p.append(jnp.stack([proj[(b + 1) * seq - POOL_BUF:(b + 1) * seq, o1:] for b in range(batch)]))

        h = _outproj(dn, pool, w_out_b, h, l)
        h = _ffn(h, nffn, w_gu_b, w_down_b, nfin, l, l == depth - 1)

    delta_s, conv_s = sample_states
    y_prompt = h[:rows_p].reshape(batch, seq, D_MODEL)
    y_sample = h[rows_p:].reshape(nseq, 1, D_MODEL)
    return (y_prompt, y_sample, jnp.stack(delta_p), jnp.stack(conv_p), jnp.stack(pool_p),
            delta_s, conv_s, pool_states)
```

```python
import functools

import jax
import jax.numpy as jnp
from jax import lax
from jax.experimental import pallas as pl
from jax.experimental.pallas import tpu as pltpu

F32 = jnp.float32
BF16 = jnp.bfloat16

D_MODEL = 2048
DN_WIDTH = 1024
DN_HEADS = 8
HEAD_DIM = 128
CONV_W = 4
CHUNK = 128
POOL_WIDTH = 1024
POOL_WINDOWS = (2, 4, 8, 16)
POOL_GROUP = 256
POOL_BUF = 15
D_FF = 5632
EPS = 1e-6
PAST_LEN = 16384

MAIN_COLS = 4 * DN_WIDTH + POOL_WIDTH
GATE_COLS = 256
LANES = 128
SUBLANES = 8

ROW_TILES = (1040, 640, 512, 256, 128)
INPROJ_COL_TILE = 1024
OUTPROJ_ROW_TILE = 640
OUTPROJ_COL_TILE = 2048
FFN_TILE = 512
CONV_ROWS = 256
PREP_ROWS = 1024
PREP_UNROLL = 8
SCAN_HEADS = 8
SCAN_ROWS = 512
POOL_ROWS = 512
SAMPLE_BLOCK = 16
SAMPLE_UNROLL = 2
POOL_SAMPLE_BLOCK = 32
MIB = 1024 * 1024


def _params(semantics, vmem_mib):
    return pltpu.CompilerParams(dimension_semantics=semantics, vmem_limit_bytes=vmem_mib * MIB)


def _silu(x):
    return x * jax.nn.sigmoid(x)


def _softplus(x):
    return jnp.maximum(x, 0.0) + jnp.log1p(jnp.exp(-jnp.abs(x)))


def _rms(x, w):
    return x * lax.rsqrt(jnp.mean(x * x, axis=-1, keepdims=True) + EPS) * w


def _l2(x):
    return x * lax.rsqrt(jnp.sum(x * x, axis=-1, keepdims=True) + EPS)


def _dot(a, b):
    return jnp.dot(a, b, preferred_element_type=F32)


def _row_tile(m, largest=ROW_TILES[0]):
    return next(t for t in ROW_TILES if t <= largest and m % t == 0)


def _any_spec():
    return pl.BlockSpec(memory_space=pl.ANY)


def _inproj_body(h_ref, nw_ref, wa_ref, wb_ref, wg_ref, proj_ref, gate_ref, xn_ref, *, n_a):
    j = pl.program_id(1)

    @pl.when(j == 0)
    def _():
        xn = _rms(h_ref[...], nw_ref[...]).astype(BF16)
        xn_ref[...] = xn
        gate_ref[...] = _dot(xn, wg_ref[...])

    @pl.when(j < n_a)
    def _():
        proj_ref[...] = _dot(xn_ref[...], wa_ref[...])

    @pl.when(j >= n_a)
    def _():
        proj_ref[...] = _dot(xn_ref[...], wb_ref[...])


def _inproj(h, nw, w_a, w_b, w_gate, l):
    m = h.shape[0]
    tm, tn = _row_tile(m), INPROJ_COL_TILE
    n_a = (MAIN_COLS - w_b.shape[-1]) // tn
    return pl.pallas_call(
        functools.partial(_inproj_body, n_a=n_a),
        grid=(m // tm, MAIN_COLS // tn),
        in_specs=[
            pl.BlockSpec((tm, D_MODEL), lambda i, j: (i, 0)),
            pl.BlockSpec((None, 1, D_MODEL), lambda i, j: (l, 0, 0)),
            pl.BlockSpec((None, D_MODEL, tn), lambda i, j: (l, 0, jnp.minimum(j, n_a - 1))),
            pl.BlockSpec((None, D_MODEL, tn), lambda i, j: (l, 0, jnp.maximum(j - n_a, 0))),
            pl.BlockSpec((None, D_MODEL, GATE_COLS), lambda i, j: (l, 0, 0)),
        ],
        out_specs=[
            pl.BlockSpec((tm, tn), lambda i, j: (i, j)),
            pl.BlockSpec((tm, GATE_COLS), lambda i, j: (i, 0)),
        ],
        out_shape=[
            jax.ShapeDtypeStruct((m, MAIN_COLS), F32),
            jax.ShapeDtypeStruct((m, GATE_COLS), F32),
        ],
        scratch_shapes=[pltpu.VMEM((tm, D_MODEL), BF16)],
        compiler_params=_params(("parallel", "arbitrary"), 54),
        name="inproj",
    )(h, nw, w_a, w_b, w_gate)


def _gates_body(gate_ref, alog_ref, dtb_ref, beta_ref, gc_ref):
    beta_ref[...] = jax.nn.sigmoid(gate_ref[:, 0:LANES])
    g = -jnp.exp(alog_ref[...]) * _softplus(gate_ref[:, LANES:2 * LANES] + dtb_ref[...])
    pos = lax.broadcasted_iota(jnp.int32, g.shape, 0) & (CHUNK - 1)
    sh = 1
    while sh < CHUNK:
        g = g + jnp.where(pos >= sh, pltpu.roll(g, sh, axis=0), 0.0)
        sh *= 2
    gc_ref[...] = g


def _gates(gate, alog, dtb, l, batch, seq):
    vec = pl.BlockSpec((None, 1, LANES), lambda b: (l, 0, 0))
    return pl.pallas_call(
        _gates_body,
        grid=(batch,),
        in_specs=[pl.BlockSpec((seq, GATE_COLS), lambda b: (b, 0)), vec, vec],
        out_specs=[pl.BlockSpec((seq, LANES), lambda b: (b, 0))] * 2,
        out_shape=[jax.ShapeDtypeStruct((batch * seq, LANES), F32)] * 2,
        compiler_params=_params(("parallel",), 32),
        name="gates",
    )(gate, alog, dtb)


def _gdn_prep_body(q_ref, k_ref, v_ref, qh_ref, kh_ref, vh_ref, cwq_ref, cwk_ref, cwv_ref, beta_ref, gc_ref,
                   u_ref, wq_ref, kdt_ref, qk_ref, gl_ref, qs, ks, vs):
    head = pl.program_id(1)
    r = pl.program_id(2)
    rp = q_ref.shape[0]
    rb = min(CONV_ROWS, rp)

    def conv_silu(x_ref, halo_ref, cw_ref, i):
        cw = cw_ref[...]
        if i == 0:
            halo = jnp.where(r > 0, halo_ref[...], 0.0)
            ext = jnp.concatenate([halo, x_ref[0:rb, :]], axis=0)
            off = SUBLANES - (CONV_W - 1)
            taps = [ext[off + t:off + t + rb] for t in range(CONV_W)]
        else:
            off = i * rb - (CONV_W - 1)
            taps = [x_ref[off + t:off + t + rb, :] for t in range(CONV_W)]
        y = taps[0] * cw[0:1]
        for t in range(1, CONV_W):
            y = y + taps[t] * cw[t:t + 1]
        return _silu(y)

    for i in range(rp // rb):
        rows = slice(i * rb, (i + 1) * rb)
        qs[rows, :] = _l2(conv_silu(q_ref, qh_ref, cwq_ref, i)) * (HEAD_DIM ** -0.5)
        ks[rows, :] = _l2(conv_silu(k_ref, kh_ref, cwk_ref, i))
        vs[rows, :] = conv_silu(v_ref, vh_ref, cwv_ref, i)

    head_lane = lax.broadcasted_iota(jnp.int32, (CHUNK, LANES), 1) == head
    ri = lax.broadcasted_iota(jnp.int32, (CHUNK, CHUNK), 0)
    ci = lax.broadcasted_iota(jnp.int32, (CHUNK, CHUNK), 1)
    tril = ri >= ci
    strict = ri > ci
    half = CHUNK // 2
    same_half = (ri >= half) == (ci >= half)
    lower_left = (ri >= half) & (ci < half)
    eye = (ri == ci).astype(F32)

    def lane_bcast(x):
        return jnp.broadcast_to(jnp.sum(jnp.where(head_lane, x, 0.0), axis=1, keepdims=True), (CHUNK, LANES))

    n_chunks = rp // CHUNK
    unroll = min(PREP_UNROLL, n_chunks)
    nt_dims = (((1,), (1,)), ((), ()))

    def group(g, carry):
        js = [g * unroll + t for t in range(unroll)]
        rows = [pl.ds(pl.multiple_of(j * CHUNK, CHUNK), CHUNK) for j in js]
        q = [qs[r, :] for r in rows]
        k = [ks[r, :] for r in rows]
        bb = [lane_bcast(beta_ref[r, :]) for r in rows]
        gc = [lane_bcast(gc_ref[r, :]) for r in rows]
        decay = [jnp.where(tril, jnp.exp(jnp.where(tril, x - x.T, 0.0)), 0.0) for x in gc]
        kb = [a * b for a, b in zip(k, bb)]
        a1 = [lax.dot_general(jnp.concatenate([a, b], axis=0), c, nt_dims, preferred_element_type=F32)
              for a, b, c in zip(q, kb, k)]
        for r, a, d in zip(rows, a1, decay):
            qk_ref[r, :] = (a[:CHUNK] * d).astype(BF16)
        kk = [a[CHUNK:] * d for a, d in zip(a1, decay)]
        m_d = [jnp.where(strict & same_half, x, 0.0) for x in kk]
        c = [jnp.where(lower_left, x, 0.0) for x in kk]
        inv = [eye - x for x in m_d]
        mp = m_d
        sh = 1
        while 4 * sh < CHUNK:
            sh *= 2
            mp = [_dot(x, x) for x in mp]
            inv = [a + _dot(a, x) for a, x in zip(inv, mp)]
        dc = [_dot(a, x) for a, x in zip(inv, c)]
        inv = [a - _dot(x, a) for a, x in zip(inv, dc)]
        eg = [jnp.exp(x) for x in gc]
        uw = [_dot(a, jnp.concatenate([vs[r, :] * b, c * e], axis=1))
              for a, r, b, c, e in zip(inv, rows, bb, kb, eg)]
        for j, r, x, a, e, kk, gcx in zip(js, rows, uw, q, eg, k, gc):
            u_ref[r, :] = x[:, :HEAD_DIM]
            wq_ref[pl.ds(pl.multiple_of(j * 2 * CHUNK, 2 * CHUNK), 2 * CHUNK), :] = jnp.concatenate(
                [x[:, HEAD_DIM:], a * e], axis=0).astype(BF16)
            glast = gcx[CHUNK - 1:CHUNK, :]
            kdt_ref[:, r] = (kk * jnp.exp(glast - gcx)).T.astype(BF16)
            gl_ref[pl.ds(pl.multiple_of(j * SUBLANES, SUBLANES), SUBLANES), :] = jnp.broadcast_to(
                jnp.exp(glast), (SUBLANES, LANES))
        return carry

    lax.fori_loop(0, n_chunks // unroll, group, 0)


def _gdn_prep(proj, beta, gc, conv_w, l, batch, seq):
    nh = DN_HEADS
    rp = min(PREP_ROWS, seq)
    nr = seq // rp
    n_chunks = seq // CHUNK
    col = lambda base: (lambda b, h, r: (b * nr + r, base + h))
    halo = lambda base: (lambda b, h, r: (jnp.maximum((b * nr + r) * (rp // SUBLANES) - 1, 0), base + h))
    cwcol = lambda base: (lambda b, h, r: (l, 0, base + h))
    gate_spec = pl.BlockSpec((rp, LANES), lambda b, h, r: (b * nr + r, 0))
    per_head = lambda rows, cols: pl.BlockSpec((None, None, rows, cols), lambda b, h, r: (b, h, r, 0))
    return pl.pallas_call(
        _gdn_prep_body,
        grid=(batch, nh, nr),
        in_specs=[
            pl.BlockSpec((rp, HEAD_DIM), col(0)),
            pl.BlockSpec((rp, HEAD_DIM), col(nh)),
            pl.BlockSpec((rp, HEAD_DIM), col(2 * nh)),
            pl.BlockSpec((SUBLANES, HEAD_DIM), halo(0)),
            pl.BlockSpec((SUBLANES, HEAD_DIM), halo(nh)),
            pl.BlockSpec((SUBLANES, HEAD_DIM), halo(2 * nh)),
            pl.BlockSpec((None, CONV_W, HEAD_DIM), cwcol(0)),
            pl.BlockSpec((None, CONV_W, HEAD_DIM), cwcol(nh)),
            pl.BlockSpec((None, CONV_W, HEAD_DIM), cwcol(2 * nh)),
            gate_spec, gate_spec,
        ],
        out_specs=[
            per_head(rp, HEAD_DIM),
            per_head(2 * rp, HEAD_DIM),
            pl.BlockSpec((None, None, HEAD_DIM, rp), lambda b, h, r: (b, h, 0, r)),
            per_head(rp, HEAD_DIM),
            per_head((rp // CHUNK) * SUBLANES, LANES),
        ],
        out_shape=[
            jax.ShapeDtypeStruct((batch, nh, seq, HEAD_DIM), F32),
            jax.ShapeDtypeStruct((batch, nh, 2 * seq, HEAD_DIM), BF16),
            jax.ShapeDtypeStruct((batch, nh, HEAD_DIM, seq), BF16),
            jax.ShapeDtypeStruct((batch, nh, seq, CHUNK), BF16),
            jax.ShapeDtypeStruct((batch, nh, n_chunks * SUBLANES, LANES), F32),
        ],
        scratch_shapes=[pltpu.VMEM((rp, HEAD_DIM), F32)] * 3,
        compiler_params=_params(("parallel", "parallel", "parallel"), 32),
        name="gdn_prep",
    )(proj, proj, proj, proj, proj, proj, conv_w, conv_w, conv_w, beta, gc)


def _gdn_scan_body(u_ref, wq_ref, kdt_ref, qk_ref, gl_ref, z_ref, dnw_ref, o_ref, s_out_ref, st_s):
    hs, rows_per_step = u_ref.shape[0], u_ref.shape[1]
    step = pl.program_id(2)

    @pl.when(step == 0)
    def _():
        st_s[...] = jnp.zeros(st_s.shape, F32)

    def chunk(c, carry):
        rows = pl.ds(pl.multiple_of(c * CHUNK, CHUNK), CHUNK)
        wrows = pl.ds(pl.multiple_of(c * 2 * CHUNK, 2 * CHUNK), 2 * CHUNK)
        grows = pl.ds(pl.multiple_of(c * SUBLANES, SUBLANES), SUBLANES)
        heads = range(hs)
        st = [st_s[hh] for hh in heads]
        r = [_dot(wq_ref[hh, wrows, :], st[hh].astype(BF16)) for hh in heads]
        vnew = [(u_ref[hh, rows, :] - r[hh][:CHUNK]).astype(BF16) for hh in heads]
        for hh in heads:
            st_s[hh] = st[hh] * gl_ref[hh, grows, :][0:1, :] + _dot(kdt_ref[hh, :, rows], vnew[hh])
        o = [r[hh][CHUNK:] + _dot(qk_ref[hh, rows, :], vnew[hh]) for hh in heads]
        for hh in heads:
            lo = hh * HEAD_DIM
            o_ref[rows, lo:lo + HEAD_DIM] = (
                _rms(o[hh], dnw_ref[...]) * _silu(z_ref[rows, lo:lo + HEAD_DIM])).astype(BF16)
        return carry

    lax.fori_loop(0, rows_per_step // CHUNK, chunk, 0)

    @pl.when(step == pl.num_programs(2) - 1)
    def _():
        s_out_ref[...] = st_s[...]


def _gdn_scan(u, wq, kdt, qk, gl, proj, dnw, l, batch, seq):
    m = proj.shape[0]
    hs = SCAN_HEADS
    rs = min(SCAN_ROWS, seq)
    nr = seq // rs
    zcol = (3 * DN_WIDTH) // (hs * HEAD_DIM)
    per_group = lambda rows, cols: pl.BlockSpec((None, hs, rows, cols), lambda b, g, r: (b, g, r, 0))
    return pl.pallas_call(
        _gdn_scan_body,
        grid=(batch, DN_HEADS // hs, nr),
        in_specs=[
            per_group(rs, HEAD_DIM),
            per_group(2 * rs, HEAD_DIM),
            pl.BlockSpec((None, hs, HEAD_DIM, rs), lambda b, g, r: (b, g, 0, r)),
            per_group(rs, CHUNK),
            per_group((rs // CHUNK) * SUBLANES, LANES),
            pl.BlockSpec((rs, hs * HEAD_DIM), lambda b, g, r: (b * nr + r, zcol + g)),
            pl.BlockSpec((None, 1, HEAD_DIM), lambda b, g, r: (l, 0, 0)),
        ],
        out_specs=[
            pl.BlockSpec((rs, hs * HEAD_DIM), lambda b, g, r: (b * nr + r, g)),
            pl.BlockSpec((None, hs, HEAD_DIM, HEAD_DIM), lambda b, g, r: (b, g, 0, 0)),
        ],
        out_shape=[
            jax.ShapeDtypeStruct((m, DN_WIDTH), BF16),
            jax.ShapeDtypeStruct((batch, DN_HEADS, HEAD_DIM, HEAD_DIM), F32),
        ],
        scratch_shapes=[pltpu.VMEM((hs, HEAD_DIM, HEAD_DIM), F32)],
        compiler_params=_params(("parallel", "parallel", "arbitrary"), 40),
        name="gdn_scan",
    )(u, wq, kdt, qk, gl, proj, dnw)


def _gdn_sample_body(qkv_ref, z_ref, gate_ref, cs_ref, s_ref, cw_ref, alog_ref, dtb_ref, dnw_ref, *rest,
                     n_alias):
    dn_ref, sn_ref, cn_ref, q_s, k_s, v_s, eg_s, beta_s, o_s = rest[n_alias:]
    nb = qkv_ref.shape[0]
    x = qkv_ref[...]
    c0, c1, c2 = cs_ref[:, 0, :], cs_ref[:, 1, :], cs_ref[:, 2, :]
    cw = cw_ref[...]
    y = c0 * cw[0:1] + c1 * cw[1:2] + c2 * cw[2:3] + x * cw[3:4]
    y = _silu(y)
    cn_ref[:, 0, :] = c1
    cn_ref[:, 1, :] = c2
    cn_ref[:, 2, :] = x
    for h in range(DN_HEADS):
        lo = h * HEAD_DIM
        q_s[h] = _l2(y[:, lo:lo + HEAD_DIM]) * (HEAD_DIM ** -0.5)
        k_s[h] = _l2(y[:, DN_WIDTH + lo:DN_WIDTH + lo + HEAD_DIM])
        v_s[h] = y[:, 2 * DN_WIDTH + lo:2 * DN_WIDTH + lo + HEAD_DIM]
    beta_s[...] = jax.nn.sigmoid(gate_ref[:, 0:LANES])
    eg_s[...] = jnp.exp(-jnp.exp(alog_ref[...]) * _softplus(gate_ref[:, LANES:2 * LANES] + dtb_ref[...]))

    pad7 = jnp.zeros((SUBLANES - 1, HEAD_DIM), F32)
    pad6 = jnp.zeros((SUBLANES - 2, HEAD_DIM), F32)

    tn_dims = (((0,), (0,)), ((), ()))

    def per_samples(i, carry):
        units = [(i * SAMPLE_UNROLL + t, h) for t in range(SAMPLE_UNROLL) for h in range(DN_HEADS)]
        k = [k_s[h, pl.ds(b, 1), :] for b, h in units]
        q = [q_s[h, pl.ds(b, 1), :] for b, h in units]
        st = [s_ref[b, h] * eg_s[pl.ds(b, 1), :][:, h:h + 1] for b, h in units]
        r = [_dot(jnp.concatenate([kk, qq, pad6], axis=0), s) for kk, qq, s in zip(k, q, st)]
        dv = [(v_s[h, pl.ds(b, 1), :] - x[0:1]) * beta_s[pl.ds(b, 1), :][:, h:h + 1]
              for (b, h), x in zip(units, r)]
        outer = [lax.dot_general(jnp.concatenate([kk, pad7], axis=0), jnp.concatenate([d, pad7], axis=0),
                                 tn_dims, preferred_element_type=F32) for kk, d in zip(k, dv)]
        for (b, h), s, x, o, kk, qq, d in zip(units, st, outer, r, k, q, dv):
            sn_ref[b, h] = s + x
            o_s[h, pl.ds(b, 1), :] = o[1:2] + jnp.sum(qq * kk, axis=-1, keepdims=True) * d
        return carry

    lax.fori_loop(0, nb // SAMPLE_UNROLL, per_samples, 0)

    z = z_ref[...]
    for h in range(DN_HEADS):
        lo = h * HEAD_DIM
        dn_ref[:, lo:lo + HEAD_DIM] = (_rms(o_s[h], dnw_ref[...]) * _silu(z[:, lo:lo + HEAD_DIM])).astype(BF16)


def _gdn_sample(proj, gate, state_conv, state_delta, conv_w, alog, dtb, dnw, dn_buf, prev, l, row0):
    depth, nseq = state_delta.shape[:2]
    nb = SAMPLE_BLOCK
    base = row0 // nb
    vec = pl.BlockSpec((None, 1, LANES), lambda i: (l, 0, 0))
    delta_spec = pl.BlockSpec((None, nb, DN_HEADS, HEAD_DIM, HEAD_DIM), lambda i: (l, i, 0, 0, 0))
    conv_spec = pl.BlockSpec((None, nb, CONV_W - 1, 3 * DN_WIDTH), lambda i: (l, i, 0, 0))
    inputs = [proj, proj, gate, state_conv, state_delta, conv_w, alog, dtb, dnw, dn_buf]
    in_specs = [
        pl.BlockSpec((nb, 3 * DN_WIDTH), lambda i: (base + i, 0)),
        pl.BlockSpec((nb, DN_WIDTH), lambda i: (base + i, 3)),
        pl.BlockSpec((nb, GATE_COLS), lambda i: (base + i, 0)),
        conv_spec, delta_spec,
        pl.BlockSpec((None, CONV_W, 3 * DN_WIDTH), lambda i: (l, 0, 0)),
        vec, vec, vec,
        _any_spec(),
    ]
    aliases = {9: 0}
    if prev is not None:
        inputs += list(prev)
        in_specs += [_any_spec(), _any_spec()]
        aliases.update({10: 1, 11: 2})
    return pl.pallas_call(
        functools.partial(_gdn_sample_body, n_alias=len(inputs) - 9),
        grid=(nseq // nb,),
        in_specs=in_specs,
        out_specs=[pl.BlockSpec((nb, DN_WIDTH), lambda i: (base + i, 0)), delta_spec, conv_spec],
        out_shape=[
            jax.ShapeDtypeStruct(dn_buf.shape, dn_buf.dtype),
            jax.ShapeDtypeStruct(state_delta.shape, F32),
            jax.ShapeDtypeStruct(state_conv.shape, F32),
        ],
        scratch_shapes=[
            pltpu.VMEM((DN_HEADS, nb, HEAD_DIM), F32),
            pltpu.VMEM((DN_HEADS, nb, HEAD_DIM), F32),
            pltpu.VMEM((DN_HEADS, nb, HEAD_DIM), F32),
            pltpu.VMEM((nb, LANES), F32),
            pltpu.VMEM((nb, LANES), F32),
            pltpu.VMEM((DN_HEADS, nb, HEAD_DIM), F32),
        ],
        input_output_aliases=aliases,
        compiler_params=_params(("parallel",), 48),
        name="gdn_sample",
    )(*inputs)


def _pool_matmul(d, w_ref, sc_ref, o_ref, gi):
    lo = gi * POOL_GROUP
    y = _dot(d.astype(BF16), w_ref[gi]) * sc_ref[:, lo:lo + POOL_GROUP]
    o_ref[:, lo:lo + POOL_GROUP] = y.astype(BF16)


def _pool_prompt_body(p_ref, prev_ref, w_ref, sc_ref, o_ref, *, tiles_per_seq):
    rt = p_ref.shape[0]
    hist = prev_ref.shape[0]
    t = pl.program_id(0) % tiles_per_seq
    cur = p_ref[...]
    prev = jnp.where(t == 0, 0.0, prev_ref[...])
    pos = t * rt + lax.broadcasted_iota(jnp.int32, (rt, POOL_GROUP), 0)
    for gi, win in enumerate(POOL_WINDOWS):
        lo = gi * POOL_GROUP
        x = cur[:, lo:lo + POOL_GROUP]
        s = jnp.concatenate([prev[:, lo:lo + POOL_GROUP], x], axis=0)
        sh = 1
        while sh < win:
            s = s + pltpu.roll(s, sh, axis=0)
            sh *= 2
        cnt = jnp.minimum(pos + 1, win).astype(F32)
        _pool_matmul(s[hist:] / cnt - x, w_ref, sc_ref, o_ref, gi)


def _pool_prompt(proj, w_pool, scale, l, batch, seq):
    m = proj.shape[0]
    rt = min(POOL_ROWS, seq)
    hist = POOL_BUF + 1
    tiles_per_seq = seq // rt
    pcol = (4 * DN_WIDTH) // POOL_WIDTH
    ng = len(POOL_WINDOWS)
    return pl.pallas_call(
        functools.partial(_pool_prompt_body, tiles_per_seq=tiles_per_seq),
        grid=(batch * tiles_per_seq,),
        in_specs=[
            pl.BlockSpec((rt, POOL_WIDTH), lambda r: (r, pcol)),
            pl.BlockSpec((hist, POOL_WIDTH), lambda r: (jnp.maximum(r * (rt // hist) - 1, 0), pcol)),
            pl.BlockSpec((None, ng, POOL_GROUP, POOL_GROUP), lambda r: (l, 0, 0, 0)),
            pl.BlockSpec((None, 1, POOL_WIDTH), lambda r: (l, 0, 0)),
        ],
        out_specs=pl.BlockSpec((rt, POOL_WIDTH), lambda r: (r, 0)),
        out_shape=jax.ShapeDtypeStruct((m, POOL_WIDTH), BF16),
        compiler_params=_params(("parallel",), 32),
        name="pool_prompt",
    )(proj, proj, w_pool, scale)


def _pool_sample_body(p_ref, buf_ref, w_ref, sc_ref, *rest):
    o_ref, nb_ref = rest[-2:]
    cur = p_ref[...]
    for gi, win in enumerate(POOL_WINDOWS):
        lo = gi * POOL_GROUP
        x = cur[:, lo:lo + POOL_GROUP]
        s = x
        for j in range(1, win):
            s = s + buf_ref[:, POOL_BUF - j, lo:lo + POOL_GROUP]
        cnt = float(min(PAST_LEN + 1, win))
        _pool_matmul(s / cnt - x, w_ref, sc_ref, o_ref, gi)
    for j in range(POOL_BUF - 1):
        nb_ref[:, j, :] = buf_ref[:, j + 1, :]
    nb_ref[:, POOL_BUF - 1, :] = cur


def _pool_sample(proj, state_pool, w_pool, scale, pool_buf, prev, l, row0):
    nseq = state_pool.shape[1]
    nb = POOL_SAMPLE_BLOCK
    base = row0 // nb
    pcol = (4 * DN_WIDTH) // POOL_WIDTH
    ng = len(POOL_WINDOWS)
    state_spec = pl.BlockSpec((None, nb, POOL_BUF, POOL_WIDTH), lambda i: (l, i, 0, 0))
    inputs = [proj, state_pool, w_pool, scale, pool_buf]
    in_specs = [
        pl.BlockSpec((nb, POOL_WIDTH), lambda i: (base + i, pcol)),
        state_spec,
        pl.BlockSpec((None, ng, POOL_GROUP, POOL_GROUP), lambda i: (l, 0, 0, 0)),
        pl.BlockSpec((None, 1, POOL_WIDTH), lambda i: (l, 0, 0)),
        _any_spec(),
    ]
    aliases = {4: 0}
    if prev is not None:
        inputs.append(prev)
        in_specs.append(_any_spec())
        aliases[5] = 1
    return pl.pallas_call(
        _pool_sample_body,
        grid=(nseq // nb,),
        in_specs=in_specs,
        out_specs=[pl.BlockSpec((nb, POOL_WIDTH), lambda i: (base + i, 0)), state_spec],
        out_shape=[
            jax.ShapeDtypeStruct(pool_buf.shape, pool_buf.dtype),
            jax.ShapeDtypeStruct(state_pool.shape, F32),
        ],
        input_output_aliases=aliases,
        compiler_params=_params(("parallel",), 32),
        name="pool_sample",
    )(*inputs)


def _outproj_body(dn_ref, pool_ref, w1_ref, w2_ref, h_ref, o_ref):
    o_ref[...] = h_ref[...] + _dot(dn_ref[...], w1_ref[...]) + _dot(pool_ref[...], w2_ref[...])


def _outproj(dn, pool, w_out, h, l):
    m = h.shape[0]
    tm, tn = _row_tile(m, OUTPROJ_ROW_TILE), OUTPROJ_COL_TILE
    return pl.pallas_call(
        _outproj_body,
        grid=(m // tm, D_MODEL // tn),
        in_specs=[
            pl.BlockSpec((tm, DN_WIDTH), lambda i, j: (i, 0)),
            pl.BlockSpec((tm, POOL_WIDTH), lambda i, j: (i, 0)),
            pl.BlockSpec((None, DN_WIDTH, tn), lambda i, j: (l, 0, j)),
            pl.BlockSpec((None, POOL_WIDTH, tn), lambda i, j: (l, 1, j)),
            pl.BlockSpec((tm, tn), lambda i, j: (i, j)),
        ],
        out_specs=pl.BlockSpec((tm, tn), lambda i, j: (i, j)),
        out_shape=jax.ShapeDtypeStruct((m, D_MODEL), F32),
        compiler_params=_params(("parallel", "parallel"), 54),
        name="outproj",
    )(dn, pool, w_out, w_out, h)


def _ffn_body(h_ref, nw_ref, wg_ref, wu_ref, wd_ref, nf_ref, o_ref, xn_ref, *, final_norm):
    f = pl.program_id(1)

    @pl.when(f == 0)
    def _():
        x = h_ref[...]
        xn_ref[...] = _rms(x, nw_ref[...]).astype(BF16)
        o_ref[...] = x

    xn = xn_ref[...]
    act = (_silu(_dot(xn, wg_ref[...])) * _dot(xn, wu_ref[...])).astype(BF16)
    o_ref[...] += _dot(act, wd_ref[...])

    if final_norm:
        @pl.when(f == pl.num_programs(1) - 1)
        def _():
            o_ref[...] = _rms(o_ref[...], nf_ref[...])


def _ffn(h, nw, w_gate_up, w_down, nf, l, final_norm):
    m = h.shape[0]
    tm, tf = _row_tile(m), FFN_TILE
    nft = D_FF // tf
    return pl.pallas_call(
        functools.partial(_ffn_body, final_norm=final_norm),
        grid=(m // tm, nft),
        in_specs=[
            pl.BlockSpec((tm, D_MODEL), lambda i, f: (i, 0)),
            pl.BlockSpec((None, 1, D_MODEL), lambda i, f: (l, 0, 0)),
            pl.BlockSpec((None, D_MODEL, tf), lambda i, f: (l, 0, f)),
            pl.BlockSpec((None, D_MODEL, tf), lambda i, f: (l, 0, nft + f)),
            pl.BlockSpec((None, tf, D_MODEL), lambda i, f: (l, f, 0)),
            pl.BlockSpec((1, D_MODEL), lambda i, f: (0, 0)),
        ],
        out_specs=pl.BlockSpec((tm, D_MODEL), lambda i, f: (i, 0)),
        out_shape=jax.ShapeDtypeStruct((m, D_MODEL), F32),
        scratch_shapes=[pltpu.VMEM((tm, D_MODEL), BF16)],
        compiler_params=_params(("parallel", "arbitrary"), 56),
        name="ffn",
    )(h, nw, w_gate_up, w_gate_up, w_down, nf)


def _lane_vecs(x):
    depth, n = x.shape
    return jnp.zeros((depth, 1, LANES), F32).at[:, 0, :n].set(x)


def kernel(x_prompt, x_sample, state_delta, state_conv, state_pool, norm_mix, w_in, conv_w, a_log, dt_bias,
           dn_norm, w_pool, pool_scale, w_out, norm_ffn, w_gate_up, w_down, norm_final):
    batch, seq, _ = x_prompt.shape
    nseq = x_sample.shape[0]
    depth = w_in.shape[0]
    rows_p = batch * seq
    h = jnp.concatenate([x_prompt.reshape(rows_p, D_MODEL), x_sample.reshape(nseq, D_MODEL)], axis=0)

    o1 = 4 * DN_WIDTH
    w_a = w_in.astype(BF16)
    w_b = w_a[:, :, o1 + 2 * DN_HEADS:]
    w_gate = jnp.zeros((depth, D_MODEL, GATE_COLS), BF16)
    w_gate = w_gate.at[:, :, :DN_HEADS].set(w_a[:, :, o1:o1 + DN_HEADS])
    w_gate = w_gate.at[:, :, LANES:LANES + DN_HEADS].set(w_a[:, :, o1 + DN_HEADS:o1 + 2 * DN_HEADS])
    w_out_b = w_out.astype(BF16)
    w_gu_b = w_gate_up.astype(BF16)
    w_down_b = w_down.astype(BF16)
    w_pool_b = w_pool.astype(BF16)
    alog, dtb = _lane_vecs(a_log), _lane_vecs(dt_bias)
    dnw = dn_norm.reshape(depth, 1, HEAD_DIM)
    scale = pool_scale.reshape(depth, 1, POOL_WIDTH)
    nmix = norm_mix.reshape(depth, 1, D_MODEL)
    nffn = norm_ffn.reshape(depth, 1, D_MODEL)
    nfin = norm_final.reshape(1, D_MODEL)

    delta_p, conv_p, pool_p = [], [], []
    sample_states = None
    pool_states = None
    for l in range(depth):
        proj, gate = _inproj(h, nmix, w_a, w_b, w_gate, l)

        beta, gc = _gates(gate, alog, dtb, l, batch, seq)
        u, wq, kdt, qk, gl = _gdn_prep(proj, beta, gc, conv_w, l, batch, seq)
        dn, s_p = _gdn_scan(u, wq, kdt, qk, gl, proj, dnw, l, batch, seq)
        dn, *sample_states = _gdn_sample(proj, gate, state_conv, state_delta, conv_w, alog, dtb, dnw, dn,
                                         sample_states, l, rows_p)
        pool = _pool_prompt(proj, w_pool_b, scale, l, batch, seq)
        pool, pool_states = _pool_sample(proj, state_pool, w_pool_b, scale, pool, pool_states, l, rows_p)

        delta_p.append(s_p)
        conv_p.append(jnp.stack([proj[(b + 1) * seq - (CONV_W - 1):(b + 1) * seq, :3 * DN_WIDTH]
                                 for b in range(batch)]))
        pool_p.append(jnp.stack([proj[(b + 1) * seq - POOL_BUF:(b + 1) * seq, o1:] for b in range(batch)]))

        h = _outproj(dn, pool, w_out_b, h, l)
        h = _ffn(h, nffn, w_gu_b, w_down_b, nfin, l, l == depth - 1)

    delta_s, conv_s = sample_states
    y_prompt = h[:rows_p].reshape(batch, seq, D_MODEL)
    y_sample = h[rows_p:].reshape(nseq, 1, D_MODEL)
    return (y_prompt, y_sample, jnp.stack(delta_p), jnp.stack(conv_p), jnp.stack(pool_p),
            delta_s, conv_s, pool_states)
```

```python
import functools

import jax
import jax.numpy as jnp
from jax import lax
from jax.experimental import pallas as pl
from jax.experimental.pallas import tpu as pltpu

F32 = jnp.float32
BF16 = jnp.bfloat16

D_MODEL = 2048
DN_WIDTH = 1024
DN_HEADS = 8
HEAD_DIM = 128
CONV_W = 4
CHUNK = 128
POOL_WIDTH = 1024
POOL_WINDOWS = (2, 4, 8, 16)
POOL_GROUP = 256
POOL_BUF = 15
D_FF = 5632
EPS = 1e-6
PAST_LEN = 16384

MAIN_COLS = 4 * DN_WIDTH + POOL_WIDTH
GATE_COLS = 256
LANES = 128
SUBLANES = 8

ROW_TILES = (1040, 640, 512, 256, 128)
INPROJ_COL_TILE = 1024
OUTPROJ_ROW_TILE = 640
OUTPROJ_COL_TILE = 2048
FFN_TILE = 512
CONV_ROWS = 256
PREP_ROWS = 2048
PREP_UNROLL = 16
SCAN_HEADS = 8
SCAN_ROWS = 512
POOL_ROWS = 512
SAMPLE_BLOCK = 16
SAMPLE_UNROLL = 2
POOL_SAMPLE_BLOCK = 32
MIB = 1024 * 1024


def _params(semantics, vmem_mib):
    return pltpu.CompilerParams(dimension_semantics=semantics, vmem_limit_bytes=vmem_mib * MIB)


def _silu(x):
    return x * jax.nn.sigmoid(x)


def _softplus(x):
    return jnp.maximum(x, 0.0) + jnp.log1p(jnp.exp(-jnp.abs(x)))


def _rms(x, w):
    return x * lax.rsqrt(jnp.mean(x * x, axis=-1, keepdims=True) + EPS) * w


def _l2(x):
    return x * lax.rsqrt(jnp.sum(x * x, axis=-1, keepdims=True) + EPS)


def _dot(a, b):
    return jnp.dot(a, b, preferred_element_type=F32)


def _row_tile(m, largest=ROW_TILES[0]):
    return next(t for t in ROW_TILES if t <= largest and m % t == 0)


def _any_spec():
    return pl.BlockSpec(memory_space=pl.ANY)


def _inproj_body(h_ref, nw_ref, wa_ref, wb_ref, wg_ref, proj_ref, gate_ref, xn_ref, *, n_a):
    j = pl.program_id(1)

    @pl.when(j == 0)
    def _():
        xn = _rms(h_ref[...], nw_ref[...]).astype(BF16)
        xn_ref[...] = xn
        gate_ref[...] = _dot(xn, wg_ref[...])

    @pl.when(j < n_a)
    def _():
        proj_ref[...] = _dot(xn_ref[...], wa_ref[...])

    @pl.when(j >= n_a)
    def _():
        proj_ref[...] = _dot(xn_ref[...], wb_ref[...])


def _inproj(h, nw, w_a, w_b, w_gate, l):
    m = h.shape[0]
    tm, tn = _row_tile(m), INPROJ_COL_TILE
    n_a = (MAIN_COLS - w_b.shape[-1]) // tn
    return pl.pallas_call(
        functools.partial(_inproj_body, n_a=n_a),
        grid=(m // tm, MAIN_COLS // tn),
        in_specs=[
            pl.BlockSpec((tm, D_MODEL), lambda i, j: (i, 0)),
            pl.BlockSpec((None, 1, D_MODEL), lambda i, j: (l, 0, 0)),
            pl.BlockSpec((None, D_MODEL, tn), lambda i, j: (l, 0, jnp.minimum(j, n_a - 1))),
            pl.BlockSpec((None, D_MODEL, tn), lambda i, j: (l, 0, jnp.maximum(j - n_a, 0))),
            pl.BlockSpec((None, D_MODEL, GATE_COLS), lambda i, j: (l, 0, 0)),
        ],
        out_specs=[
            pl.BlockSpec((tm, tn), lambda i, j: (i, j)),
            pl.BlockSpec((tm, GATE_COLS), lambda i, j: (i, 0)),
        ],
        out_shape=[
            jax.ShapeDtypeStruct((m, MAIN_COLS), F32),
            jax.ShapeDtypeStruct((m, GATE_COLS), F32),
        ],
        scratch_shapes=[pltpu.VMEM((tm, D_MODEL), BF16)],
        compiler_params=_params(("parallel", "arbitrary"), 54),
        name="inproj",
    )(h, nw, w_a, w_b, w_gate)


def _gates_body(gate_ref, alog_ref, dtb_ref, beta_ref, gc_ref):
    beta_ref[...] = jax.nn.sigmoid(gate_ref[:, 0:LANES])
    g = -jnp.exp(alog_ref[...]) * _softplus(gate_ref[:, LANES:2 * LANES] + dtb_ref[...])
    pos = lax.broadcasted_iota(jnp.int32, g.shape, 0) & (CHUNK - 1)
    sh = 1
    while sh < CHUNK:
        g = g + jnp.where(pos >= sh, pltpu.roll(g, sh, axis=0), 0.0)
        sh *= 2
    gc_ref[...] = g


def _gates(gate, alog, dtb, l, batch, seq):
    vec = pl.BlockSpec((None, 1, LANES), lambda b: (l, 0, 0))
    return pl.pallas_call(
        _gates_body,
        grid=(batch,),
        in_specs=[pl.BlockSpec((seq, GATE_COLS), lambda b: (b, 0)), vec, vec],
        out_specs=[pl.BlockSpec((seq, LANES), lambda b: (b, 0))] * 2,
        out_shape=[jax.ShapeDtypeStruct((batch * seq, LANES), F32)] * 2,
        compiler_params=_params(("parallel",), 32),
        name="gates",
    )(gate, alog, dtb)


def _gdn_prep_body(q_ref, k_ref, v_ref, qh_ref, kh_ref, vh_ref, cwq_ref, cwk_ref, cwv_ref, beta_ref, gc_ref,
                   u_ref, wq_ref, kdt_ref, qk_ref, gl_ref, qs, ks, vs):
    head = pl.program_id(1)
    r = pl.program_id(2)
    rp = q_ref.shape[0]
    rb = min(CONV_ROWS, rp)

    def conv_silu(x_ref, halo_ref, cw_ref, i):
        cw = cw_ref[...]
        if i == 0:
            halo = jnp.where(r > 0, halo_ref[...], 0.0)
            ext = jnp.concatenate([halo, x_ref[0:rb, :]], axis=0)
            off = SUBLANES - (CONV_W - 1)
            taps = [ext[off + t:off + t + rb] for t in range(CONV_W)]
        else:
            off = i * rb - (CONV_W - 1)
            taps = [x_ref[off + t:off + t + rb, :] for t in range(CONV_W)]
        y = taps[0] * cw[0:1]
        for t in range(1, CONV_W):
            y = y + taps[t] * cw[t:t + 1]
        return _silu(y)

    def normalise(i):
        rows = slice(i * rb, (i + 1) * rb)
        qs[rows, :] = _l2(conv_silu(q_ref, qh_ref, cwq_ref, i)) * (HEAD_DIM ** -0.5)
        ks[rows, :] = _l2(conv_silu(k_ref, kh_ref, cwk_ref, i))
        vs[rows, :] = conv_silu(v_ref, vh_ref, cwv_ref, i)

    head_lane = lax.broadcasted_iota(jnp.int32, (CHUNK, LANES), 1) == head
    ri = lax.broadcasted_iota(jnp.int32, (CHUNK, CHUNK), 0)
    ci = lax.broadcasted_iota(jnp.int32, (CHUNK, CHUNK), 1)
    tril = ri >= ci
    strict = ri > ci
    half = CHUNK // 2
    same_half = (ri >= half) == (ci >= half)
    lower_left = (ri >= half) & (ci < half)
    eye = (ri == ci).astype(F32)
    nt_dims = (((1,), (1,)), ((), ()))

    def lane_bcast(x):
        return jnp.broadcast_to(jnp.sum(jnp.where(head_lane, x, 0.0), axis=1, keepdims=True), (CHUNK, LANES))

    def group(js):
        rows = [slice(j * CHUNK, (j + 1) * CHUNK) for j in js]
        q = [qs[r, :] for r in rows]
        k = [ks[r, :] for r in rows]
        bb = [lane_bcast(beta_ref[r, :]) for r in rows]
        gc = [lane_bcast(gc_ref[r, :]) for r in rows]
        decay = [jnp.where(tril, jnp.exp(jnp.where(tril, x - x.T, 0.0)), 0.0) for x in gc]
        kb = [a * b for a, b in zip(k, bb)]
        a1 = [lax.dot_general(jnp.concatenate([a, b], axis=0), x, nt_dims, preferred_element_type=F32)
              for a, b, x in zip(q, kb, k)]
        for r, a, d in zip(rows, a1, decay):
            qk_ref[r, :] = (a[:CHUNK] * d).astype(BF16)
        kbk = [a[CHUNK:] * d for a, d in zip(a1, decay)]
        m_d = [jnp.where(strict & same_half, x, 0.0) for x in kbk]
        corner = [jnp.where(lower_left, x, 0.0) for x in kbk]
        inv = [eye - x for x in m_d]
        mp = m_d
        sh = 1
        while 4 * sh < CHUNK:
            sh *= 2
            mp = [_dot(x, x) for x in mp]
            inv = [a + _dot(a, x) for a, x in zip(inv, mp)]
        dc = [_dot(a, x) for a, x in zip(inv, corner)]
        inv = [a - _dot(x, a) for a, x in zip(inv, dc)]
        eg = [jnp.exp(x) for x in gc]
        uw = [_dot(a, jnp.concatenate([vs[r, :] * b, x * e], axis=1))
              for a, r, b, x, e in zip(inv, rows, bb, kb, eg)]
        for j, r, x, a, e, kj, gcx in zip(js, rows, uw, q, eg, k, gc):
            u_ref[r, :] = x[:, :HEAD_DIM]
            wq_ref[2 * j * CHUNK:2 * (j + 1) * CHUNK, :] = jnp.concatenate(
                [x[:, HEAD_DIM:], a * e], axis=0).astype(BF16)
            glast = gcx[CHUNK - 1:CHUNK, :]
            kdt_ref[:, r] = (kj * jnp.exp(glast - gcx)).T.astype(BF16)
            gl_ref[j * SUBLANES:(j + 1) * SUBLANES, :] = jnp.broadcast_to(jnp.exp(glast), (SUBLANES, LANES))

    n_chunks = rp // CHUNK
    unroll = min(PREP_UNROLL, n_chunks)
    blocks_per_part = max(unroll * CHUNK // rb, 1)
    for part in range(n_chunks // unroll):
        for i in range(part * blocks_per_part, min((part + 1) * blocks_per_part, rp // rb)):
            normalise(i)
        group([part * unroll + t for t in range(unroll)])


def _gdn_prep(proj, beta, gc, conv_w, l, batch, seq):
    nh = DN_HEADS
    rp = min(PREP_ROWS, seq)
    nr = seq // rp
    n_chunks = seq // CHUNK
    col = lambda base: (lambda b, h, r: (b * nr + r, base + h))
    halo = lambda base: (lambda b, h, r: (jnp.maximum((b * nr + r) * (rp // SUBLANES) - 1, 0), base + h))
    cwcol = lambda base: (lambda b, h, r: (l, 0, base + h))
    gate_spec = pl.BlockSpec((rp, LANES), lambda b, h, r: (b * nr + r, 0))
    per_head = lambda rows, cols: pl.BlockSpec((None, None, rows, cols), lambda b, h, r: (b, h, r, 0))
    return pl.pallas_call(
        _gdn_prep_body,
        grid=(batch, nh, nr),
        in_specs=[
            pl.BlockSpec((rp, HEAD_DIM), col(0)),
            pl.BlockSpec((rp, HEAD_DIM), col(nh)),
            pl.BlockSpec((rp, HEAD_DIM), col(2 * nh)),
            pl.BlockSpec((SUBLANES, HEAD_DIM), halo(0)),
            pl.BlockSpec((SUBLANES, HEAD_DIM), halo(nh)),
            pl.BlockSpec((SUBLANES, HEAD_DIM), halo(2 * nh)),
            pl.BlockSpec((None, CONV_W, HEAD_DIM), cwcol(0)),
            pl.BlockSpec((None, CONV_W, HEAD_DIM), cwcol(nh)),
            pl.BlockSpec((None, CONV_W, HEAD_DIM), cwcol(2 * nh)),
            gate_spec, gate_spec,
        ],
        out_specs=[
            per_head(rp, HEAD_DIM),
            per_head(2 * rp, HEAD_DIM),
            pl.BlockSpec((None, None, HEAD_DIM, rp), lambda b, h, r: (b, h, 0, r)),
            per_head(rp, HEAD_DIM),
            per_head((rp // CHUNK) * SUBLANES, LANES),
        ],
        out_shape=[
            jax.ShapeDtypeStruct((batch, nh, seq, HEAD_DIM), F32),
            jax.ShapeDtypeStruct((batch, nh, 2 * seq, HEAD_DIM), BF16),
            jax.ShapeDtypeStruct((batch, nh, HEAD_DIM, seq), BF16),
            jax.ShapeDtypeStruct((batch, nh, seq, CHUNK), BF16),
            jax.ShapeDtypeStruct((batch, nh, n_chunks * SUBLANES, LANES), F32),
        ],
        scratch_shapes=[pltpu.VMEM((rp, HEAD_DIM), F32)] * 3,
        compiler_params=_params(("parallel", "parallel", "parallel"), 32),
        name="gdn_prep",
    )(proj, proj, proj, proj, proj, proj, conv_w, conv_w, conv_w, beta, gc)


def _gdn_scan_body(u_ref, wq_ref, kdt_ref, qk_ref, gl_ref, z_ref, dnw_ref, o_ref, s_out_ref, st_s):
    hs, rows_per_step = u_ref.shape[0], u_ref.shape[1]
    step = pl.program_id(2)

    @pl.when(step == 0)
    def _():
        st_s[...] = jnp.zeros(st_s.shape, F32)

    def chunk(c, carry):
        rows = pl.ds(pl.multiple_of(c * CHUNK, CHUNK), CHUNK)
        wrows = pl.ds(pl.multiple_of(c * 2 * CHUNK, 2 * CHUNK), 2 * CHUNK)
        grows = pl.ds(pl.multiple_of(c * SUBLANES, SUBLANES), SUBLANES)
        heads = range(hs)
        st = [st_s[hh] for hh in heads]
        r = [_dot(wq_ref[hh, wrows, :], st[hh].astype(BF16)) for hh in heads]
        vnew = [(u_ref[hh, rows, :] - r[hh][:CHUNK]).astype(BF16) for hh in heads]
        for hh in heads:
            st_s[hh] = st[hh] * gl_ref[hh, grows, :][0:1, :] + _dot(kdt_ref[hh, :, rows], vnew[hh])
        o = [r[hh][CHUNK:] + _dot(qk_ref[hh, rows, :], vnew[hh]) for hh in heads]
        for hh in heads:
            lo = hh * HEAD_DIM
            o_ref[rows, lo:lo + HEAD_DIM] = (
                _rms(o[hh], dnw_ref[...]) * _silu(z_ref[rows, lo:lo + HEAD_DIM])).astype(BF16)
        return carry

    lax.fori_loop(0, rows_per_step // CHUNK, chunk, 0)

    @pl.when(step == pl.num_programs(2) - 1)
    def _():
        s_out_ref[...] = st_s[...]


def _gdn_scan(u, wq, kdt, qk, gl, proj, dnw, l, batch, seq):
    m = proj.shape[0]
    hs = SCAN_HEADS
    rs = min(SCAN_ROWS, seq)
    nr = seq // rs
    zcol = (3 * DN_WIDTH) // (hs * HEAD_DIM)
    per_group = lambda rows, cols: pl.BlockSpec((None, hs, rows, cols), lambda b, g, r: (b, g, r, 0))
    return pl.pallas_call(
        _gdn_scan_body,
        grid=(batch, DN_HEADS // hs, nr),
        in_specs=[
            per_group(rs, HEAD_DIM),
            per_group(2 * rs, HEAD_DIM),
            pl.BlockSpec((None, hs, HEAD_DIM, rs), lambda b, g, r: (b, g, 0, r)),
            per_group(rs, CHUNK),
            per_group((rs // CHUNK) * SUBLANES, LANES),
            pl.BlockSpec((rs, hs * HEAD_DIM), lambda b, g, r: (b * nr + r, zcol + g)),
            pl.BlockSpec((None, 1, HEAD_DIM), lambda b, g, r: (l, 0, 0)),
        ],
        out_specs=[
            pl.BlockSpec((rs, hs * HEAD_DIM), lambda b, g, r: (b * nr + r, g)),
            pl.BlockSpec((None, hs, HEAD_DIM, HEAD_DIM), lambda b, g, r: (b, g, 0, 0)),
        ],
        out_shape=[
            jax.ShapeDtypeStruct((m, DN_WIDTH), BF16),
            jax.ShapeDtypeStruct((batch, DN_HEADS, HEAD_DIM, HEAD_DIM), F32),
        ],
        scratch_shapes=[pltpu.VMEM((hs, HEAD_DIM, HEAD_DIM), F32)],
        compiler_params=_params(("parallel", "parallel", "arbitrary"), 40),
        name="gdn_scan",
    )(u, wq, kdt, qk, gl, proj, dnw)


def _gdn_sample_body(qkv_ref, z_ref, gate_ref, cs_ref, s_ref, cw_ref, alog_ref, dtb_ref, dnw_ref, *rest,
                     n_alias):
    dn_ref, sn_ref, cn_ref, q_s, k_s, v_s, eg_s, beta_s, o_s = rest[n_alias:]
    nb = qkv_ref.shape[0]
    x = qkv_ref[...]
    c0, c1, c2 = cs_ref[:, 0, :], cs_ref[:, 1, :], cs_ref[:, 2, :]
    cw = cw_ref[...]
    y = c0 * cw[0:1] + c1 * cw[1:2] + c2 * cw[2:3] + x * cw[3:4]
    y = _silu(y)
    cn_ref[:, 0, :] = c1
    cn_ref[:, 1, :] = c2
    cn_ref[:, 2, :] = x
    for h in range(DN_HEADS):
        lo = h * HEAD_DIM
        q_s[h] = _l2(y[:, lo:lo + HEAD_DIM]) * (HEAD_DIM ** -0.5)
        k_s[h] = _l2(y[:, DN_WIDTH + lo:DN_WIDTH + lo + HEAD_DIM])
        v_s[h] = y[:, 2 * DN_WIDTH + lo:2 * DN_WIDTH + lo + HEAD_DIM]
    beta_s[...] = jax.nn.sigmoid(gate_ref[:, 0:LANES])
    eg_s[...] = jnp.exp(-jnp.exp(alog_ref[...]) * _softplus(gate_ref[:, LANES:2 * LANES] + dtb_ref[...]))

    pad7 = jnp.zeros((SUBLANES - 1, HEAD_DIM), F32)
    pad6 = jnp.zeros((SUBLANES - 2, HEAD_DIM), F32)

    tn_dims = (((0,), (0,)), ((), ()))

    def per_samples(i, carry):
        units = [(i * SAMPLE_UNROLL + t, h) for t in range(SAMPLE_UNROLL) for h in range(DN_HEADS)]
        k = [k_s[h, pl.ds(b, 1), :] for b, h in units]
        q = [q_s[h, pl.ds(b, 1), :] for b, h in units]
        st = [s_ref[b, h] * eg_s[pl.ds(b, 1), :][:, h:h + 1] for b, h in units]
        r = [_dot(jnp.concatenate([kk, qq, pad6], axis=0), s) for kk, qq, s in zip(k, q, st)]
        dv = [(v_s[h, pl.ds(b, 1), :] - x[0:1]) * beta_s[pl.ds(b, 1), :][:, h:h + 1]
              for (b, h), x in zip(units, r)]
        outer = [lax.dot_general(jnp.concatenate([kk, pad7], axis=0), jnp.concatenate([d, pad7], axis=0),
                                 tn_dims, preferred_element_type=F32) for kk, d in zip(k, dv)]
        for (b, h), s, x, o, kk, qq, d in zip(units, st, outer, r, k, q, dv):
            sn_ref[b, h] = s + x
            o_s[h, pl.ds(b, 1), :] = o[1:2] + jnp.sum(qq * kk, axis=-1, keepdims=True) * d
        return carry

    lax.fori_loop(0, nb // SAMPLE_UNROLL, per_samples, 0)

    z = z_ref[...]
    for h in range(DN_HEADS):
        lo = h * HEAD_DIM
        dn_ref[:, lo:lo + HEAD_DIM] = (_rms(o_s[h], dnw_ref[...]) * _silu(z[:, lo:lo + HEAD_DIM])).astype(BF16)


def _gdn_sample(proj, gate, state_conv, state_delta, conv_w, alog, dtb, dnw, dn_buf, prev, l, row0):
    depth, nseq = state_delta.shape[:2]
    nb = SAMPLE_BLOCK
    base = row0 // nb
    vec = pl.BlockSpec((None, 1, LANES), lambda i: (l, 0, 0))
    delta_spec = pl.BlockSpec((None, nb, DN_HEADS, HEAD_DIM, HEAD_DIM), lambda i: (l, i, 0, 0, 0))
    conv_spec = pl.BlockSpec((None, nb, CONV_W - 1, 3 * DN_WIDTH), lambda i: (l, i, 0, 0))
    inputs = [proj, proj, gate, state_conv, state_delta, conv_w, alog, dtb, dnw, dn_buf]
    in_specs = [
        pl.BlockSpec((nb, 3 * DN_WIDTH), lambda i: (base + i, 0)),
        pl.BlockSpec((nb, DN_WIDTH), lambda i: (base + i, 3)),
        pl.BlockSpec((nb, GATE_COLS), lambda i: (base + i, 0)),
        conv_spec, delta_spec,
        pl.BlockSpec((None, CONV_W, 3 * DN_WIDTH), lambda i: (l, 0, 0)),
        vec, vec, vec,
        _any_spec(),
    ]
    aliases = {9: 0}
    if prev is not None:
        inputs += list(prev)
        in_specs += [_any_spec(), _any_spec()]
        aliases.update({10: 1, 11: 2})
    return pl.pallas_call(
        functools.partial(_gdn_sample_body, n_alias=len(inputs) - 9),
        grid=(nseq // nb,),
        in_specs=in_specs,
        out_specs=[pl.BlockSpec((nb, DN_WIDTH), lambda i: (base + i, 0)), delta_spec, conv_spec],
        out_shape=[
            jax.ShapeDtypeStruct(dn_buf.shape, dn_buf.dtype),
            jax.ShapeDtypeStruct(state_delta.shape, F32),
            jax.ShapeDtypeStruct(state_conv.shape, F32),
        ],
        scratch_shapes=[
            pltpu.VMEM((DN_HEADS, nb, HEAD_DIM), F32),
            pltpu.VMEM((DN_HEADS, nb, HEAD_DIM), F32),
            pltpu.VMEM((DN_HEADS, nb, HEAD_DIM), F32),
            pltpu.VMEM((nb, LANES), F32),
            pltpu.VMEM((nb, LANES), F32),
            pltpu.VMEM((DN_HEADS, nb, HEAD_DIM), F32),
        ],
        input_output_aliases=aliases,
        compiler_params=_params(("parallel",), 48),
        name="gdn_sample",
    )(*inputs)


def _pool_matmul(d, w_ref, sc_ref, o_ref, gi):
    lo = gi * POOL_GROUP
    y = _dot(d.astype(BF16), w_ref[gi]) * sc_ref[:, lo:lo + POOL_GROUP]
    o_ref[:, lo:lo + POOL_GROUP] = y.astype(BF16)


def _pool_prompt_body(p_ref, prev_ref, w_ref, sc_ref, o_ref, *, tiles_per_seq):
    rt = p_ref.shape[0]
    hist = prev_ref.shape[0]
    t = pl.program_id(0) % tiles_per_seq
    cur = p_ref[...]
    prev = jnp.where(t == 0, 0.0, prev_ref[...])
    pos = t * rt + lax.broadcasted_iota(jnp.int32, (rt, POOL_GROUP), 0)
    for gi, win in enumerate(POOL_WINDOWS):
        lo = gi * POOL_GROUP
        x = cur[:, lo:lo + POOL_GROUP]
        s = jnp.concatenate([prev[:, lo:lo + POOL_GROUP], x], axis=0)
        sh = 1
        while sh < win:
            s = s + pltpu.roll(s, sh, axis=0)
            sh *= 2
        cnt = jnp.minimum(pos + 1, win).astype(F32)
        _pool_matmul(s[hist:] / cnt - x, w_ref, sc_ref, o_ref, gi)


def _pool_prompt(proj, w_pool, scale, l, batch, seq):
    m = proj.shape[0]
    rt = min(POOL_ROWS, seq)
    hist = POOL_BUF + 1
    tiles_per_seq = seq // rt
    pcol = (4 * DN_WIDTH) // POOL_WIDTH
    ng = len(POOL_WINDOWS)
    return pl.pallas_call(
        functools.partial(_pool_prompt_body, tiles_per_seq=tiles_per_seq),
        grid=(batch * tiles_per_seq,),
        in_specs=[
            pl.BlockSpec((rt, POOL_WIDTH), lambda r: (r, pcol)),
            pl.BlockSpec((hist, POOL_WIDTH), lambda r: (jnp.maximum(r * (rt // hist) - 1, 0), pcol)),
            pl.BlockSpec((None, ng, POOL_GROUP, POOL_GROUP), lambda r: (l, 0, 0, 0)),
            pl.BlockSpec((None, 1, POOL_WIDTH), lambda r: (l, 0, 0)),
        ],
        out_specs=pl.BlockSpec((rt, POOL_WIDTH), lambda r: (r, 0)),
        out_shape=jax.ShapeDtypeStruct((m, POOL_WIDTH), BF16),
        compiler_params=_params(("parallel",), 32),
        name="pool_prompt",
    )(proj, proj, w_pool, scale)


def _pool_sample_body(p_ref, buf_ref, w_ref, sc_ref, *rest):
    o_ref, nb_ref = rest[-2:]
    cur = p_ref[...]
    for gi, win in enumerate(POOL_WINDOWS):
        lo = gi * POOL_GROUP
        x = cur[:, lo:lo + POOL_GROUP]
        s = x
        for j in range(1, win):
            s = s + buf_ref[:, POOL_BUF - j, lo:lo + POOL_GROUP]
        cnt = float(min(PAST_LEN + 1, win))
        _pool_matmul(s / cnt - x, w_ref, sc_ref, o_ref, gi)
    for j in range(POOL_BUF - 1):
        nb_ref[:, j, :] = buf_ref[:, j + 1, :]
    nb_ref[:, POOL_BUF - 1, :] = cur


def _pool_sample(proj, state_pool, w_pool, scale, pool_buf, prev, l, row0):
    nseq = state_pool.shape[1]
    nb = POOL_SAMPLE_BLOCK
    base = row0 // nb
    pcol = (4 * DN_WIDTH) // POOL_WIDTH
    ng = len(POOL_WINDOWS)
    state_spec = pl.BlockSpec((None, nb, POOL_BUF, POOL_WIDTH), lambda i: (l, i, 0, 0))
    inputs = [proj, state_pool, w_pool, scale, pool_buf]
    in_specs = [
        pl.BlockSpec((nb, POOL_WIDTH), lambda i: (base + i, pcol)),
        state_spec,
        pl.BlockSpec((None, ng, POOL_GROUP, POOL_GROUP), lambda i: (l, 0, 0, 0)),
        pl.BlockSpec((None, 1, POOL_WIDTH), lambda i: (l, 0, 0)),
        _any_spec(),
    ]
    aliases = {4: 0}
    if prev is not None:
        inputs.append(prev)
        in_specs.append(_any_spec())
        aliases[5] = 1
    return pl.pallas_call(
        _pool_sample_body,
        grid=(nseq // nb,),
        in_specs=in_specs,
        out_specs=[pl.BlockSpec((nb, POOL_WIDTH), lambda i: (base + i, 0)), state_spec],
        out_shape=[
            jax.ShapeDtypeStruct(pool_buf.shape, pool_buf.dtype),
            jax.ShapeDtypeStruct(state_pool.shape, F32),
        ],
        input_output_aliases=aliases,
        compiler_params=_params(("parallel",), 32),
        name="pool_sample",
    )(*inputs)


def _outproj_body(dn_ref, pool_ref, w1_ref, w2_ref, h_ref, o_ref):
    o_ref[...] = h_ref[...] + _dot(dn_ref[...], w1_ref[...]) + _dot(pool_ref[...], w2_ref[...])


def _outproj(dn, pool, w_out, h, l):
    m = h.shape[0]
    tm, tn = _row_tile(m, OUTPROJ_ROW_TILE), OUTPROJ_COL_TILE
    return pl.pallas_call(
        _outproj_body,
        grid=(m // tm, D_MODEL // tn),
        in_specs=[
            pl.BlockSpec((tm, DN_WIDTH), lambda i, j: (i, 0)),
            pl.BlockSpec((tm, POOL_WIDTH), lambda i, j: (i, 0)),
            pl.BlockSpec((None, DN_WIDTH, tn), lambda i, j: (l, 0, j)),
            pl.BlockSpec((None, POOL_WIDTH, tn), lambda i, j: (l, 1, j)),
            pl.BlockSpec((tm, tn), lambda i, j: (i, j)),
        ],
        out_specs=pl.BlockSpec((tm, tn), lambda i, j: (i, j)),
        out_shape=jax.ShapeDtypeStruct((m, D_MODEL), F32),
        compiler_params=_params(("parallel", "parallel"), 54),
        name="outproj",
    )(dn, pool, w_out, w_out, h)


def _ffn_body(h_ref, nw_ref, wg_ref, wu_ref, wd_ref, nf_ref, o_ref, *rest, final_norm, sample_row0):
    xn_ref = rest[-1]
    f = pl.program_id(1)
    last = pl.num_programs(1) - 1

    @pl.when(f == 0)
    def _():
        x = h_ref[...]
        xn_ref[...] = _rms(x, nw_ref[...]).astype(BF16)
        o_ref[...] = x

    xn = xn_ref[...]
    act = (_silu(_dot(xn, wg_ref[...])) * _dot(xn, wu_ref[...])).astype(BF16)
    o_ref[...] += _dot(act, wd_ref[...])

    if final_norm:
        @pl.when(f == last)
        def _():
            o_ref[...] = _rms(o_ref[...], nf_ref[...])

    if sample_row0 is not None:
        @pl.when((f == last) & (pl.program_id(0) == pl.num_programs(0) - 1))
        def _():
            rest[0][...] = o_ref[sample_row0:, :]


def _ffn(h, nw, w_gate_up, w_down, nf, l, final_norm, split_rows=None):
    m = h.shape[0]
    tm, tf = _row_tile(m), FFN_TILE
    nft = D_FF // tf
    out_spec = pl.BlockSpec((tm, D_MODEL), lambda i, f: (i, 0))
    if split_rows is None:
        out_specs, out_shape, sample_row0 = out_spec, jax.ShapeDtypeStruct((m, D_MODEL), F32), None
    else:
        sample_row0 = split_rows - (m // tm - 1) * tm
        assert 0 < sample_row0 < tm, (m, tm, split_rows)
        out_specs = [out_spec, pl.BlockSpec((m - split_rows, D_MODEL), lambda i, f: (0, 0))]
        out_shape = [jax.ShapeDtypeStruct((split_rows, D_MODEL), F32),
                     jax.ShapeDtypeStruct((m - split_rows, D_MODEL), F32)]
    return pl.pallas_call(
        functools.partial(_ffn_body, final_norm=final_norm, sample_row0=sample_row0),
        grid=(m // tm, nft),
        in_specs=[
            pl.BlockSpec((tm, D_MODEL), lambda i, f: (i, 0)),
            pl.BlockSpec((None, 1, D_MODEL), lambda i, f: (l, 0, 0)),
            pl.BlockSpec((None, D_MODEL, tf), lambda i, f: (l, 0, f)),
            pl.BlockSpec((None, D_MODEL, tf), lambda i, f: (l, 0, nft + f)),
            pl.BlockSpec((None, tf, D_MODEL), lambda i, f: (l, f, 0)),
            pl.BlockSpec((1, D_MODEL), lambda i, f: (0, 0)),
        ],
        out_specs=out_specs,
        out_shape=out_shape,
        scratch_shapes=[pltpu.VMEM((tm, D_MODEL), BF16)],
        compiler_params=_params(("parallel", "arbitrary"), 56),
        name="ffn",
    )(h, nw, w_gate_up, w_gate_up, w_down, nf)


def _lane_vecs(x):
    depth, n = x.shape
    return jnp.zeros((depth, 1, LANES), F32).at[:, 0, :n].set(x)


def kernel(x_prompt, x_sample, state_delta, state_conv, state_pool, norm_mix, w_in, conv_w, a_log, dt_bias,
           dn_norm, w_pool, pool_scale, w_out, norm_ffn, w_gate_up, w_down, norm_final):
    batch, seq, _ = x_prompt.shape
    nseq = x_sample.shape[0]
    depth = w_in.shape[0]
    rows_p = batch * seq
    h = jnp.concatenate([x_prompt.reshape(rows_p, D_MODEL), x_sample.reshape(nseq, D_MODEL)], axis=0)

    o1 = 4 * DN_WIDTH
    w_a = w_in.astype(BF16)
    w_b = w_a[:, :, o1 + 2 * DN_HEADS:]
    w_gate = jnp.zeros((depth, D_MODEL, GATE_COLS), BF16)
    w_gate = w_gate.at[:, :, :DN_HEADS].set(w_a[:, :, o1:o1 + DN_HEADS])
    w_gate = w_gate.at[:, :, LANES:LANES + DN_HEADS].set(w_a[:, :, o1 + DN_HEADS:o1 + 2 * DN_HEADS])
    w_out_b = w_out.astype(BF16)
    w_gu_b = w_gate_up.astype(BF16)
    w_down_b = w_down.astype(BF16)
    w_pool_b = w_pool.astype(BF16)
    alog, dtb = _lane_vecs(a_log), _lane_vecs(dt_bias)
    dnw = dn_norm.reshape(depth, 1, HEAD_DIM)
    scale = pool_scale.reshape(depth, 1, POOL_WIDTH)
    nmix = norm_mix.reshape(depth, 1, D_MODEL)
    nffn = norm_ffn.reshape(depth, 1, D_MODEL)
    nfin = norm_final.reshape(1, D_MODEL)

    delta_p, conv_p, pool_p = [], [], []
    sample_states = None
    pool_states = None
    for l in range(depth):
        proj, gate = _inproj(h, nmix, w_a, w_b, w_gate, l)

        beta, gc = _gates(gate, alog, dtb, l, batch, seq)
        u, wq, kdt, qk, gl = _gdn_prep(proj, beta, gc, conv_w, l, batch, seq)
        dn, s_p = _gdn_scan(u, wq, kdt, qk, gl, proj, dnw, l, batch, seq)
        dn, *sample_states = _gdn_sample(proj, gate, state_conv, state_delta, conv_w, alog, dtb, dnw, dn,
                                         sample_states, l, rows_p)
        pool = _pool_prompt(proj, w_pool_b, scale, l, batch, seq)
        pool, pool_states = _pool_sample(proj, state_pool, w_pool_b, scale, pool, pool_states, l, rows_p)

        delta_p.append(s_p)
        conv_p.append(jnp.stack([proj[(b + 1) * seq - (CONV_W - 1):(b + 1) * seq, :3 * DN_WIDTH]
                                 for b in range(batch)]))
        pool_p.append(jnp.stack([proj[(b + 1) * seq - POOL_BUF:(b + 1) * seq, o1:] for b in range(batch)]))

        h = _outproj(dn, pool, w_out_b, h, l)
        if l < depth - 1:
            h = _ffn(h, nffn, w_gu_b, w_down_b, nfin, l, False)
        elif rows_p % _row_tile(h.shape[0]) == 0:
            h = _ffn(h, nffn, w_gu_b, w_down_b, nfin, l, True)
            y_p, y_s = h[:rows_p], h[rows_p:]
        else:
            y_p, y_s = _ffn(h, nffn, w_gu_b, w_down_b, nfin, l, True, split_rows=rows_p)

    delta_s, conv_s = sample_states
    y_prompt = y_p.reshape(batch, seq, D_MODEL)
    y_sample = y_s.reshape(nseq, 1, D_MODEL)
    return (y_prompt, y_sample, jnp.stack(delta_p), jnp.stack(conv_p), jnp.stack(pool_p),
            delta_s, conv_s, pool_states)
```

```python
import functools

import jax
import jax.numpy as jnp
from jax import lax
from jax.experimental import pallas as pl
from jax.experimental.pallas import tpu as pltpu

F32 = jnp.float32
BF16 = jnp.bfloat16

D_MODEL = 2048
DN_WIDTH = 1024
DN_HEADS = 8
HEAD_DIM = 128
CONV_W = 4
CHUNK = 128
POOL_WIDTH = 1024
POOL_WINDOWS = (2, 4, 8, 16)
POOL_GROUP = 256
POOL_BUF = 15
D_FF = 5632
EPS = 1e-6
PAST_LEN = 16384

MAIN_COLS = 4 * DN_WIDTH + POOL_WIDTH
GATE_COLS = 256
LANES = 128
SUBLANES = 8

ROW_TILES = (1040, 640, 512, 256, 128)
INPROJ_COL_TILE = 1024
OUTPROJ_ROW_TILE = 640
OUTPROJ_COL_TILE = 2048
FFN_ROW_TILE = 640
FFN_TILE = 512
CONV_ROWS = 256
PREP_ROWS = 2048
PREP_UNROLL = 16
SCAN_HEADS = 8
SCAN_ROWS = 512
POOL_ROWS = 512
SAMPLE_BLOCK = 16
SAMPLE_UNROLL = 2
POOL_SAMPLE_BLOCK = 32
MIB = 1024 * 1024


def _params(semantics, vmem_mib):
    return pltpu.CompilerParams(dimension_semantics=semantics, vmem_limit_bytes=vmem_mib * MIB)


def _silu(x):
    return x * jax.nn.sigmoid(x)


def _softplus(x):
    return jnp.maximum(x, 0.0) + jnp.log1p(jnp.exp(-jnp.abs(x)))


def _rms(x, w):
    return x * lax.rsqrt(jnp.mean(x * x, axis=-1, keepdims=True) + EPS) * w


def _l2(x):
    return x * lax.rsqrt(jnp.sum(x * x, axis=-1, keepdims=True) + EPS)


def _dot(a, b):
    return jnp.dot(a, b, preferred_element_type=F32)


def _row_tile(m, largest=ROW_TILES[0]):
    return next(t for t in ROW_TILES if t <= largest and m % t == 0)


def _any_spec():
    return pl.BlockSpec(memory_space=pl.ANY)


def _inproj_body(h_ref, nw_ref, wa_ref, wb_ref, wg_ref, proj_ref, gate_ref, xn_ref, *, n_a):
    j = pl.program_id(1)

    @pl.when(j == 0)
    def _():
        xn = _rms(h_ref[...], nw_ref[...]).astype(BF16)
        xn_ref[...] = xn
        gate_ref[...] = _dot(xn, wg_ref[...])

    @pl.when(j < n_a)
    def _():
        proj_ref[...] = _dot(xn_ref[...], wa_ref[...])

    @pl.when(j >= n_a)
    def _():
        proj_ref[...] = _dot(xn_ref[...], wb_ref[...])


def _inproj(h, nw, w_a, w_b, w_gate, l):
    m = h.shape[0]
    tm, tn = _row_tile(m), INPROJ_COL_TILE
    n_a = (MAIN_COLS - w_b.shape[-1]) // tn
    return pl.pallas_call(
        functools.partial(_inproj_body, n_a=n_a),
        grid=(m // tm, MAIN_COLS // tn),
        in_specs=[
            pl.BlockSpec((tm, D_MODEL), lambda i, j: (i, 0)),
            pl.BlockSpec((None, 1, D_MODEL), lambda i, j: (l, 0, 0)),
            pl.BlockSpec((None, D_MODEL, tn), lambda i, j: (l, 0, jnp.minimum(j, n_a - 1))),
            pl.BlockSpec((None, D_MODEL, tn), lambda i, j: (l, 0, jnp.maximum(j - n_a, 0))),
            pl.BlockSpec((None, D_MODEL, GATE_COLS), lambda i, j: (l, 0, 0)),
        ],
        out_specs=[
            pl.BlockSpec((tm, tn), lambda i, j: (i, j)),
            pl.BlockSpec((tm, GATE_COLS), lambda i, j: (i, 0)),
        ],
        out_shape=[
            jax.ShapeDtypeStruct((m, MAIN_COLS), F32),
            jax.ShapeDtypeStruct((m, GATE_COLS), F32),
        ],
        scratch_shapes=[pltpu.VMEM((tm, D_MODEL), BF16)],
        compiler_params=_params(("parallel", "arbitrary"), 54),
        name="inproj",
    )(h, nw, w_a, w_b, w_gate)


def _gates_body(gate_ref, alog_ref, dtb_ref, beta_ref, gc_ref):
    beta_ref[...] = jax.nn.sigmoid(gate_ref[:, 0:LANES])
    g = -jnp.exp(alog_ref[...]) * _softplus(gate_ref[:, LANES:2 * LANES] + dtb_ref[...])
    pos = lax.broadcasted_iota(jnp.int32, g.shape, 0) & (CHUNK - 1)
    sh = 1
    while sh < CHUNK:
        g = g + jnp.where(pos >= sh, pltpu.roll(g, sh, axis=0), 0.0)
        sh *= 2
    gc_ref[...] = g


def _gates(gate, alog, dtb, l, batch, seq):
    vec = pl.BlockSpec((None, 1, LANES), lambda b: (l, 0, 0))
    return pl.pallas_call(
        _gates_body,
        grid=(batch,),
        in_specs=[pl.BlockSpec((seq, GATE_COLS), lambda b: (b, 0)), vec, vec],
        out_specs=[pl.BlockSpec((seq, LANES), lambda b: (b, 0))] * 2,
        out_shape=[jax.ShapeDtypeStruct((batch * seq, LANES), F32)] * 2,
        compiler_params=_params(("parallel",), 32),
        name="gates",
    )(gate, alog, dtb)


def _gdn_prep_body(q_ref, k_ref, v_ref, qh_ref, kh_ref, vh_ref, cwq_ref, cwk_ref, cwv_ref, beta_ref, gc_ref,
                   u_ref, wq_ref, kdt_ref, qk_ref, gl_ref, qs, ks, vs):
    head = pl.program_id(1)
    r = pl.program_id(2)
    rp = q_ref.shape[0]
    rb = min(CONV_ROWS, rp)

    def conv_silu(x_ref, halo_ref, cw_ref, i):
        cw = cw_ref[...]
        if i == 0:
            halo = jnp.where(r > 0, halo_ref[...], 0.0)
            ext = jnp.concatenate([halo, x_ref[0:rb, :]], axis=0)
            off = SUBLANES - (CONV_W - 1)
            taps = [ext[off + t:off + t + rb] for t in range(CONV_W)]
        else:
            off = i * rb - (CONV_W - 1)
            taps = [x_ref[off + t:off + t + rb, :] for t in range(CONV_W)]
        y = taps[0] * cw[0:1]
        for t in range(1, CONV_W):
            y = y + taps[t] * cw[t:t + 1]
        return _silu(y)

    def normalise(i):
        rows = slice(i * rb, (i + 1) * rb)
        qs[rows, :] = _l2(conv_silu(q_ref, qh_ref, cwq_ref, i)) * (HEAD_DIM ** -0.5)
        ks[rows, :] = _l2(conv_silu(k_ref, kh_ref, cwk_ref, i))
        vs[rows, :] = conv_silu(v_ref, vh_ref, cwv_ref, i)

    head_lane = lax.broadcasted_iota(jnp.int32, (CHUNK, LANES), 1) == head
    ri = lax.broadcasted_iota(jnp.int32, (CHUNK, CHUNK), 0)
    ci = lax.broadcasted_iota(jnp.int32, (CHUNK, CHUNK), 1)
    tril = ri >= ci
    strict = ri > ci
    half = CHUNK // 2
    same_half = (ri >= half) == (ci >= half)
    lower_left = (ri >= half) & (ci < half)
    eye = (ri == ci).astype(F32)
    nt_dims = (((1,), (1,)), ((), ()))

    def lane_bcast(x):
        return jnp.broadcast_to(jnp.sum(jnp.where(head_lane, x, 0.0), axis=1, keepdims=True), (CHUNK, LANES))

    def group(js):
        rows = [slice(j * CHUNK, (j + 1) * CHUNK) for j in js]
        q = [qs[r, :] for r in rows]
        k = [ks[r, :] for r in rows]
        bb = [lane_bcast(beta_ref[r, :]) for r in rows]
        gc = [lane_bcast(gc_ref[r, :]) for r in rows]
        decay = [jnp.where(tril, jnp.exp(jnp.where(tril, x - x.T, 0.0)), 0.0) for x in gc]
        kb = [a * b for a, b in zip(k, bb)]
        a1 = [lax.dot_general(jnp.concatenate([a, b], axis=0), x, nt_dims, preferred_element_type=F32)
              for a, b, x in zip(q, kb, k)]
        for r, a, d in zip(rows, a1, decay):
            qk_ref[r, :] = (a[:CHUNK] * d).astype(BF16)
        kbk = [a[CHUNK:] * d for a, d in zip(a1, decay)]
        m_d = [jnp.where(strict & same_half, x, 0.0) for x in kbk]
        corner = [jnp.where(lower_left, x, 0.0) for x in kbk]
        inv = [eye - x for x in m_d]
        mp = m_d
        sh = 1
        while 4 * sh < CHUNK:
            sh *= 2
            mp = [_dot(x, x) for x in mp]
            inv = [a + _dot(a, x) for a, x in zip(inv, mp)]
        dc = [_dot(a, x) for a, x in zip(inv, corner)]
        inv = [a - _dot(x, a) for a, x in zip(inv, dc)]
        eg = [jnp.exp(x) for x in gc]
        uw = [_dot(a, jnp.concatenate([vs[r, :] * b, x * e], axis=1))
              for a, r, b, x, e in zip(inv, rows, bb, kb, eg)]
        for j, r, x, a, e, kj, gcx in zip(js, rows, uw, q, eg, k, gc):
            u_ref[r, :] = x[:, :HEAD_DIM]
            wq_ref[2 * j * CHUNK:2 * (j + 1) * CHUNK, :] = jnp.concatenate(
                [x[:, HEAD_DIM:], a * e], axis=0).astype(BF16)
            glast = gcx[CHUNK - 1:CHUNK, :]
            kdt_ref[:, r] = (kj * jnp.exp(glast - gcx)).T.astype(BF16)
            gl_ref[j * SUBLANES:(j + 1) * SUBLANES, :] = jnp.broadcast_to(jnp.exp(glast), (SUBLANES, LANES))

    n_chunks = rp // CHUNK
    unroll = min(PREP_UNROLL, n_chunks)
    blocks_per_part = max(unroll * CHUNK // rb, 1)
    for part in range(n_chunks // unroll):
        for i in range(part * blocks_per_part, min((part + 1) * blocks_per_part, rp // rb)):
            normalise(i)
        group([part * unroll + t for t in range(unroll)])


def _gdn_prep(proj, beta, gc, conv_w, l, batch, seq):
    nh = DN_HEADS
    rp = min(PREP_ROWS, seq)
    nr = seq // rp
    n_chunks = seq // CHUNK
    col = lambda base: (lambda b, h, r: (b * nr + r, base + h))
    halo = lambda base: (lambda b, h, r: (jnp.maximum((b * nr + r) * (rp // SUBLANES) - 1, 0), base + h))
    cwcol = lambda base: (lambda b, h, r: (l, 0, base + h))
    gate_spec = pl.BlockSpec((rp, LANES), lambda b, h, r: (b * nr + r, 0))
    per_head = lambda rows, cols: pl.BlockSpec((None, None, rows, cols), lambda b, h, r: (b, h, r, 0))
    return pl.pallas_call(
        _gdn_prep_body,
        grid=(batch, nh, nr),
        in_specs=[
            pl.BlockSpec((rp, HEAD_DIM), col(0)),
            pl.BlockSpec((rp, HEAD_DIM), col(nh)),
            pl.BlockSpec((rp, HEAD_DIM), col(2 * nh)),
            pl.BlockSpec((SUBLANES, HEAD_DIM), halo(0)),
            pl.BlockSpec((SUBLANES, HEAD_DIM), halo(nh)),
            pl.BlockSpec((SUBLANES, HEAD_DIM), halo(2 * nh)),
            pl.BlockSpec((None, CONV_W, HEAD_DIM), cwcol(0)),
            pl.BlockSpec((None, CONV_W, HEAD_DIM), cwcol(nh)),
            pl.BlockSpec((None, CONV_W, HEAD_DIM), cwcol(2 * nh)),
            gate_spec, gate_spec,
        ],
        out_specs=[
            per_head(rp, HEAD_DIM),
            per_head(2 * rp, HEAD_DIM),
            pl.BlockSpec((None, None, HEAD_DIM, rp), lambda b, h, r: (b, h, 0, r)),
            per_head(rp, HEAD_DIM),
            per_head((rp // CHUNK) * SUBLANES, LANES),
        ],
        out_shape=[
            jax.ShapeDtypeStruct((batch, nh, seq, HEAD_DIM), F32),
            jax.ShapeDtypeStruct((batch, nh, 2 * seq, HEAD_DIM), BF16),
            jax.ShapeDtypeStruct((batch, nh, HEAD_DIM, seq), BF16),
            jax.ShapeDtypeStruct((batch, nh, seq, CHUNK), BF16),
            jax.ShapeDtypeStruct((batch, nh, n_chunks * SUBLANES, LANES), F32),
        ],
        scratch_shapes=[pltpu.VMEM((rp, HEAD_DIM), F32)] * 3,
        compiler_params=_params(("parallel", "parallel", "parallel"), 32),
        name="gdn_prep",
    )(proj, proj, proj, proj, proj, proj, conv_w, conv_w, conv_w, beta, gc)


def _gdn_scan_body(u_ref, wq_ref, kdt_ref, qk_ref, gl_ref, z_ref, dnw_ref, o_ref, s_out_ref, st_s):
    hs, rows_per_step = u_ref.shape[0], u_ref.shape[1]
    step = pl.program_id(2)

    @pl.when(step == 0)
    def _():
        st_s[...] = jnp.zeros(st_s.shape, F32)

    def chunk(c, carry):
        rows = pl.ds(pl.multiple_of(c * CHUNK, CHUNK), CHUNK)
        wrows = pl.ds(pl.multiple_of(c * 2 * CHUNK, 2 * CHUNK), 2 * CHUNK)
        grows = pl.ds(pl.multiple_of(c * SUBLANES, SUBLANES), SUBLANES)
        heads = range(hs)
        st = [st_s[hh] for hh in heads]
        r = [_dot(wq_ref[hh, wrows, :], st[hh].astype(BF16)) for hh in heads]
        vnew = [(u_ref[hh, rows, :] - r[hh][:CHUNK]).astype(BF16) for hh in heads]
        for hh in heads:
            st_s[hh] = st[hh] * gl_ref[hh, grows, :][0:1, :] + _dot(kdt_ref[hh, :, rows], vnew[hh])
        o = [r[hh][CHUNK:] + _dot(qk_ref[hh, rows, :], vnew[hh]) for hh in heads]
        for hh in heads:
            lo = hh * HEAD_DIM
            o_ref[rows, lo:lo + HEAD_DIM] = (
                _rms(o[hh], dnw_ref[...]) * _silu(z_ref[rows, lo:lo + HEAD_DIM])).astype(BF16)
        return carry

    lax.fori_loop(0, rows_per_step // CHUNK, chunk, 0)

    @pl.when(step == pl.num_programs(2) - 1)
    def _():
        s_out_ref[...] = st_s[...]


def _gdn_scan(u, wq, kdt, qk, gl, proj, dnw, l, batch, seq):
    m = proj.shape[0]
    hs = SCAN_HEADS
    rs = min(SCAN_ROWS, seq)
    nr = seq // rs
    zcol = (3 * DN_WIDTH) // (hs * HEAD_DIM)
    per_group = lambda rows, cols: pl.BlockSpec((None, hs, rows, cols), lambda b, g, r: (b, g, r, 0))
    return pl.pallas_call(
        _gdn_scan_body,
        grid=(batch, DN_HEADS // hs, nr),
        in_specs=[
            per_group(rs, HEAD_DIM),
            per_group(2 * rs, HEAD_DIM),
            pl.BlockSpec((None, hs, HEAD_DIM, rs), lambda b, g, r: (b, g, 0, r)),
            per_group(rs, CHUNK),
            per_group((rs // CHUNK) * SUBLANES, LANES),
            pl.BlockSpec((rs, hs * HEAD_DIM), lambda b, g, r: (b * nr + r, zcol + g)),
            pl.BlockSpec((None, 1, HEAD_DIM), lambda b, g, r: (l, 0, 0)),
        ],
        out_specs=[
            pl.BlockSpec((rs, hs * HEAD_DIM), lambda b, g, r: (b * nr + r, g)),
            pl.BlockSpec((None, hs, HEAD_DIM, HEAD_DIM), lambda b, g, r: (b, g, 0, 0)),
        ],
        out_shape=[
            jax.ShapeDtypeStruct((m, DN_WIDTH), BF16),
            jax.ShapeDtypeStruct((batch, DN_HEADS, HEAD_DIM, HEAD_DIM), F32),
        ],
        scratch_shapes=[pltpu.VMEM((hs, HEAD_DIM, HEAD_DIM), F32)],
        compiler_params=_params(("parallel", "parallel", "arbitrary"), 40),
        name="gdn_scan",
    )(u, wq, kdt, qk, gl, proj, dnw)


def _gdn_sample_body(qkv_ref, z_ref, gate_ref, cs_ref, s_ref, cw_ref, alog_ref, dtb_ref, dnw_ref, *rest,
                     n_alias):
    dn_ref, sn_ref, cn_ref, q_s, k_s, v_s, eg_s, beta_s, o_s = rest[n_alias:]
    nb = qkv_ref.shape[0]
    x = qkv_ref[...]
    c0, c1, c2 = cs_ref[0], cs_ref[1], cs_ref[2]
    cw = cw_ref[...]
    y = c0 * cw[0:1] + c1 * cw[1:2] + c2 * cw[2:3] + x * cw[3:4]
    y = _silu(y)
    cn_ref[0] = c1
    cn_ref[1] = c2
    cn_ref[2] = x
    for h in range(DN_HEADS):
        lo = h * HEAD_DIM
        q_s[h] = _l2(y[:, lo:lo + HEAD_DIM]) * (HEAD_DIM ** -0.5)
        k_s[h] = _l2(y[:, DN_WIDTH + lo:DN_WIDTH + lo + HEAD_DIM])
        v_s[h] = y[:, 2 * DN_WIDTH + lo:2 * DN_WIDTH + lo + HEAD_DIM]
    beta_s[...] = jax.nn.sigmoid(gate_ref[:, 0:LANES])
    eg_s[...] = jnp.exp(-jnp.exp(alog_ref[...]) * _softplus(gate_ref[:, LANES:2 * LANES] + dtb_ref[...]))

    pad7 = jnp.zeros((SUBLANES - 1, HEAD_DIM), F32)
    pad6 = jnp.zeros((SUBLANES - 2, HEAD_DIM), F32)

    tn_dims = (((0,), (0,)), ((), ()))

    def per_samples(i, carry):
        units = [(i * SAMPLE_UNROLL + t, h) for t in range(SAMPLE_UNROLL) for h in range(DN_HEADS)]
        k = [k_s[h, pl.ds(b, 1), :] for b, h in units]
        q = [q_s[h, pl.ds(b, 1), :] for b, h in units]
        st = [s_ref[b, h] * eg_s[pl.ds(b, 1), :][:, h:h + 1] for b, h in units]
        r = [_dot(jnp.concatenate([kk, qq, pad6], axis=0), s) for kk, qq, s in zip(k, q, st)]
        dv = [(v_s[h, pl.ds(b, 1), :] - x[0:1]) * beta_s[pl.ds(b, 1), :][:, h:h + 1]
              for (b, h), x in zip(units, r)]
        outer = [lax.dot_general(jnp.concatenate([kk, pad7], axis=0), jnp.concatenate([d, pad7], axis=0),
                                 tn_dims, preferred_element_type=F32) for kk, d in zip(k, dv)]
        for (b, h), s, x, o, kk, qq, d in zip(units, st, outer, r, k, q, dv):
            sn_ref[b, h] = s + x
            o_s[h, pl.ds(b, 1), :] = o[1:2] + jnp.sum(qq * kk, axis=-1, keepdims=True) * d
        return carry

    lax.fori_loop(0, nb // SAMPLE_UNROLL, per_samples, 0)

    z = z_ref[...]
    for h in range(DN_HEADS):
        lo = h * HEAD_DIM
        dn_ref[:, lo:lo + HEAD_DIM] = (_rms(o_s[h], dnw_ref[...]) * _silu(z[:, lo:lo + HEAD_DIM])).astype(BF16)


def _gdn_sample(proj, gate, state_conv, state_delta, conv_w, alog, dtb, dnw, dn_buf, prev, l, row0):
    depth, nseq = state_delta.shape[:2]
    nb = SAMPLE_BLOCK
    base = row0 // nb
    vec = pl.BlockSpec((None, 1, LANES), lambda i: (l, 0, 0))
    delta_spec = pl.BlockSpec((None, nb, DN_HEADS, HEAD_DIM, HEAD_DIM), lambda i: (l, i, 0, 0, 0))
    conv_spec = pl.BlockSpec((None, CONV_W - 1, nb, 3 * DN_WIDTH), lambda i: (l, 0, i, 0))
    inputs = [proj, proj, gate, state_conv, state_delta, conv_w, alog, dtb, dnw, dn_buf]
    in_specs = [
        pl.BlockSpec((nb, 3 * DN_WIDTH), lambda i: (base + i, 0)),
        pl.BlockSpec((nb, DN_WIDTH), lambda i: (base + i, 3)),
        pl.BlockSpec((nb, GATE_COLS), lambda i: (base + i, 0)),
        conv_spec, delta_spec,
        pl.BlockSpec((None, CONV_W, 3 * DN_WIDTH), lambda i: (l, 0, 0)),
        vec, vec, vec,
        _any_spec(),
    ]
    aliases = {9: 0}
    if prev is not None:
        inputs += list(prev)
        in_specs += [_any_spec(), _any_spec()]
        aliases.update({10: 1, 11: 2})
    return pl.pallas_call(
        functools.partial(_gdn_sample_body, n_alias=len(inputs) - 9),
        grid=(nseq // nb,),
        in_specs=in_specs,
        out_specs=[pl.BlockSpec((nb, DN_WIDTH), lambda i: (base + i, 0)), delta_spec, conv_spec],
        out_shape=[
            jax.ShapeDtypeStruct(dn_buf.shape, dn_buf.dtype),
            jax.ShapeDtypeStruct(state_delta.shape, F32),
            jax.ShapeDtypeStruct(state_conv.shape, F32),
        ],
        scratch_shapes=[
            pltpu.VMEM((DN_HEADS, nb, HEAD_DIM), F32),
            pltpu.VMEM((DN_HEADS, nb, HEAD_DIM), F32),
            pltpu.VMEM((DN_HEADS, nb, HEAD_DIM), F32),
            pltpu.VMEM((nb, LANES), F32),
            pltpu.VMEM((nb, LANES), F32),
            pltpu.VMEM((DN_HEADS, nb, HEAD_DIM), F32),
        ],
        input_output_aliases=aliases,
        compiler_params=_params(("parallel",), 48),
        name="gdn_sample",
    )(*inputs)


def _pool_matmul(d, w_ref, sc_ref, o_ref, gi):
    lo = gi * POOL_GROUP
    y = _dot(d.astype(BF16), w_ref[gi]) * sc_ref[:, lo:lo + POOL_GROUP]
    o_ref[:, lo:lo + POOL_GROUP] = y.astype(BF16)


def _pool_prompt_body(p_ref, prev_ref, w_ref, sc_ref, o_ref, *, tiles_per_seq):
    rt = p_ref.shape[0]
    hist = prev_ref.shape[0]
    t = pl.program_id(0) % tiles_per_seq
    cur = p_ref[...]
    prev = jnp.where(t == 0, 0.0, prev_ref[...])
    pos = t * rt + lax.broadcasted_iota(jnp.int32, (rt, POOL_GROUP), 0)
    for gi, win in enumerate(POOL_WINDOWS):
        lo = gi * POOL_GROUP
        x = cur[:, lo:lo + POOL_GROUP]
        s = jnp.concatenate([prev[:, lo:lo + POOL_GROUP], x], axis=0)
        sh = 1
        while sh < win:
            s = s + pltpu.roll(s, sh, axis=0)
            sh *= 2
        cnt = jnp.minimum(pos + 1, win).astype(F32)
        _pool_matmul(s[hist:] / cnt - x, w_ref, sc_ref, o_ref, gi)


def _pool_prompt(proj, w_pool, scale, l, batch, seq):
    m = proj.shape[0]
    rt = min(POOL_ROWS, seq)
    hist = POOL_BUF + 1
    tiles_per_seq = seq // rt
    pcol = (4 * DN_WIDTH) // POOL_WIDTH
    ng = len(POOL_WINDOWS)
    return pl.pallas_call(
        functools.partial(_pool_prompt_body, tiles_per_seq=tiles_per_seq),
        grid=(batch * tiles_per_seq,),
        in_specs=[
            pl.BlockSpec((rt, POOL_WIDTH), lambda r: (r, pcol)),
            pl.BlockSpec((hist, POOL_WIDTH), lambda r: (jnp.maximum(r * (rt // hist) - 1, 0), pcol)),
            pl.BlockSpec((None, ng, POOL_GROUP, POOL_GROUP), lambda r: (l, 0, 0, 0)),
            pl.BlockSpec((None, 1, POOL_WIDTH), lambda r: (l, 0, 0)),
        ],
        out_specs=pl.BlockSpec((rt, POOL_WIDTH), lambda r: (r, 0)),
        out_shape=jax.ShapeDtypeStruct((m, POOL_WIDTH), BF16),
        compiler_params=_params(("parallel",), 32),
        name="pool_prompt",
    )(proj, proj, w_pool, scale)


def _pool_sample_body(p_ref, buf_ref, w_ref, sc_ref, *rest):
    o_ref, nb_ref = rest[-2:]
    cur = p_ref[...]
    for gi, win in enumerate(POOL_WINDOWS):
        lo = gi * POOL_GROUP
        x = cur[:, lo:lo + POOL_GROUP]
        s = x
        for j in range(1, win):
            s = s + buf_ref[POOL_BUF - j, :, lo:lo + POOL_GROUP]
        cnt = float(min(PAST_LEN + 1, win))
        _pool_matmul(s / cnt - x, w_ref, sc_ref, o_ref, gi)
    for j in range(POOL_BUF - 1):
        nb_ref[j] = buf_ref[j + 1]
    nb_ref[POOL_BUF - 1] = cur


def _pool_sample(proj, state_pool, w_pool, scale, pool_buf, prev, l, row0):
    nseq = state_pool.shape[2]
    nb = POOL_SAMPLE_BLOCK
    base = row0 // nb
    pcol = (4 * DN_WIDTH) // POOL_WIDTH
    ng = len(POOL_WINDOWS)
    state_spec = pl.BlockSpec((None, POOL_BUF, nb, POOL_WIDTH), lambda i: (l, 0, i, 0))
    inputs = [proj, state_pool, w_pool, scale, pool_buf]
    in_specs = [
        pl.BlockSpec((nb, POOL_WIDTH), lambda i: (base + i, pcol)),
        state_spec,
        pl.BlockSpec((None, ng, POOL_GROUP, POOL_GROUP), lambda i: (l, 0, 0, 0)),
        pl.BlockSpec((None, 1, POOL_WIDTH), lambda i: (l, 0, 0)),
        _any_spec(),
    ]
    aliases = {4: 0}
    if prev is not None:
        inputs.append(prev)
        in_specs.append(_any_spec())
        aliases[5] = 1
    return pl.pallas_call(
        _pool_sample_body,
        grid=(nseq // nb,),
        in_specs=in_specs,
        out_specs=[pl.BlockSpec((nb, POOL_WIDTH), lambda i: (base + i, 0)), state_spec],
        out_shape=[
            jax.ShapeDtypeStruct(pool_buf.shape, pool_buf.dtype),
            jax.ShapeDtypeStruct(state_pool.shape, F32),
        ],
        input_output_aliases=aliases,
        compiler_params=_params(("parallel",), 32),
        name="pool_sample",
    )(*inputs)


def _outproj_body(dn_ref, pool_ref, w1_ref, w2_ref, h_ref, o_ref):
    o_ref[...] = h_ref[...] + _dot(dn_ref[...], w1_ref[...]) + _dot(pool_ref[...], w2_ref[...])


def _outproj(dn, pool, w_out, h, l):
    m = h.shape[0]
    tm, tn = _row_tile(m, OUTPROJ_ROW_TILE), OUTPROJ_COL_TILE
    return pl.pallas_call(
        _outproj_body,
        grid=(m // tm, D_MODEL // tn),
        in_specs=[
            pl.BlockSpec((tm, DN_WIDTH), lambda i, j: (i, 0)),
            pl.BlockSpec((tm, POOL_WIDTH), lambda i, j: (i, 0)),
            pl.BlockSpec((None, DN_WIDTH, tn), lambda i, j: (l, 0, j)),
            pl.BlockSpec((None, POOL_WIDTH, tn), lambda i, j: (l, 1, j)),
            pl.BlockSpec((tm, tn), lambda i, j: (i, j)),
        ],
        out_specs=pl.BlockSpec((tm, tn), lambda i, j: (i, j)),
        out_shape=jax.ShapeDtypeStruct((m, D_MODEL), F32),
        compiler_params=_params(("parallel", "parallel"), 54),
        name="outproj",
    )(dn, pool, w_out, w_out, h)


def _ffn_body(h_ref, nw_ref, wg_ref, wu_ref, wd_ref, nf_ref, *rest, final_norm, sample_row0, n_cast):
    cast_src, rest = rest[:n_cast], rest[n_cast:]
    o_ref, xn_ref = rest[0], rest[-1]
    cast_dst = rest[len(rest) - 1 - n_cast:len(rest) - 1]
    f = pl.program_id(1)
    last = pl.num_programs(1) - 1

    @pl.when(f == 0)
    def _():
        x = h_ref[...]
        xn_ref[...] = _rms(x, nw_ref[...]).astype(BF16)
        o_ref[...] = x

    xn = xn_ref[...]
    act = (_silu(_dot(xn, wg_ref[...])) * _dot(xn, wu_ref[...])).astype(BF16)
    o_ref[...] += _dot(act, wd_ref[...])
    for src, dst in zip(cast_src, cast_dst):
        dst[...] = src[...].astype(BF16)

    if final_norm:
        @pl.when(f == last)
        def _():
            o_ref[...] = _rms(o_ref[...], nf_ref[...])

    if sample_row0 is not None:
        @pl.when((f == last) & (pl.program_id(0) == pl.num_programs(0) - 1))
        def _():
            rest[1][...] = o_ref[sample_row0:, :]


def _ffn_cast_plan(m):
    steps = (m // _row_tile(m, FFN_ROW_TILE)) * (D_FF // FFN_TILE)
    n_units = 2 * D_FF // LANES
    for per_step in range(1, n_units + 1):
        if n_units % per_step == 0 and per_step * steps >= n_units:
            return per_step, n_units // per_step
    return None


def _ffn(h, nw, w_gate_up, w_down, nf, l, final_norm, split_rows=None, next_weights=None):
    m = h.shape[0]
    tm, tf = _row_tile(m, FFN_ROW_TILE), FFN_TILE
    nft = D_FF // tf
    out_spec = pl.BlockSpec((tm, D_MODEL), lambda i, f: (i, 0))
    if split_rows is None:
        out_specs, out_shape, sample_row0 = [out_spec], [jax.ShapeDtypeStruct((m, D_MODEL), F32)], None
    else:
        sample_row0 = split_rows - (m // tm - 1) * tm
        assert 0 < sample_row0 < tm, (m, tm, split_rows)
        out_specs = [out_spec, pl.BlockSpec((m - split_rows, D_MODEL), lambda i, f: (0, 0))]
        out_shape = [jax.ShapeDtypeStruct((split_rows, D_MODEL), F32),
                     jax.ShapeDtypeStruct((m - split_rows, D_MODEL), F32)]
    inputs = [h, nw, w_gate_up, w_gate_up, w_down, nf]
    in_specs = [
        pl.BlockSpec((tm, D_MODEL), lambda i, f: (i, 0)),
        pl.BlockSpec((None, 1, D_MODEL), lambda i, f: (l, 0, 0)),
        pl.BlockSpec((D_MODEL, tf), lambda i, f: (0, f)),
        pl.BlockSpec((D_MODEL, tf), lambda i, f: (0, nft + f)),
        pl.BlockSpec((tf, D_MODEL), lambda i, f: (f, 0)),
        pl.BlockSpec((1, D_MODEL), lambda i, f: (0, 0)),
    ]
    n_cast = 0
    if next_weights is not None:
        per_step, n_blocks = _ffn_cast_plan(m)
        cols, rows = per_step * LANES, per_step * (D_FF // (2 * D_FF // LANES))
        blk = lambda i, f: jnp.minimum(i * nft + f, n_blocks - 1)
        n_cast = 2
        inputs += list(next_weights)
        in_specs += [pl.BlockSpec((None, D_MODEL, cols), lambda i, f: (l + 1, 0, blk(i, f))),
                     pl.BlockSpec((None, rows, D_MODEL), lambda i, f: (l + 1, blk(i, f), 0))]
        out_specs += [pl.BlockSpec((D_MODEL, cols), lambda i, f: (0, blk(i, f))),
                      pl.BlockSpec((rows, D_MODEL), lambda i, f: (blk(i, f), 0))]
        out_shape += [jax.ShapeDtypeStruct((D_MODEL, 2 * D_FF), BF16), jax.ShapeDtypeStruct((D_FF, D_MODEL), BF16)]
    out = pl.pallas_call(
        functools.partial(_ffn_body, final_norm=final_norm, sample_row0=sample_row0, n_cast=n_cast),
        grid=(m // tm, nft),
        in_specs=in_specs,
        out_specs=out_specs,
        out_shape=out_shape,
        scratch_shapes=[pltpu.VMEM((tm, D_MODEL), BF16)],
        compiler_params=_params(("parallel", "arbitrary"), 48),
        name="ffn",
    )(*inputs)
    return out[0] if len(out) == 1 else out


def _lane_vecs(x):
    depth, n = x.shape
    return jnp.zeros((depth, 1, LANES), F32).at[:, 0, :n].set(x)


def kernel(x_prompt, x_sample, state_delta, state_conv, state_pool, norm_mix, w_in, conv_w, a_log, dt_bias,
           dn_norm, w_pool, pool_scale, w_out, norm_ffn, w_gate_up, w_down, norm_final):
    batch, seq, _ = x_prompt.shape
    nseq = x_sample.shape[0]
    depth = w_in.shape[0]
    rows_p = batch * seq
    h = jnp.concatenate([x_prompt.reshape(rows_p, D_MODEL), x_sample.reshape(nseq, D_MODEL)], axis=0)

    o1 = 4 * DN_WIDTH
    w_a = w_in.astype(BF16)
    w_b = w_a[:, :, o1 + 2 * DN_HEADS:]
    lane_pad = ((0, 0), (0, 0), (0, LANES - DN_HEADS))
    w_gate = jnp.concatenate([jnp.pad(w_a[:, :, o1:o1 + DN_HEADS], lane_pad),
                              jnp.pad(w_a[:, :, o1 + DN_HEADS:o1 + 2 * DN_HEADS], lane_pad)], axis=-1)
    w_out_b = w_out.astype(BF16)
    w_pool_b = w_pool.astype(BF16)
    alog, dtb = _lane_vecs(a_log), _lane_vecs(dt_bias)
    dnw = dn_norm.reshape(depth, 1, HEAD_DIM)
    scale = pool_scale.reshape(depth, 1, POOL_WIDTH)
    nmix = norm_mix.reshape(depth, 1, D_MODEL)
    nffn = norm_ffn.reshape(depth, 1, D_MODEL)
    nfin = norm_final.reshape(1, D_MODEL)
    conv_state = jnp.transpose(state_conv, (0, 2, 1, 3))
    pool_state = jnp.transpose(state_pool, (0, 2, 1, 3))
    hide_casts = _ffn_cast_plan(rows_p + nseq) is not None
    w_gu_l, w_down_l = w_gate_up[0].astype(BF16), w_down[0].astype(BF16)

    delta_p, conv_p, pool_p = [], [], []
    sample_states = None
    pool_states = None
    for l in range(depth):
        proj, gate = _inproj(h, nmix, w_a, w_b, w_gate, l)

        beta, gc = _gates(gate, alog, dtb, l, batch, seq)
        u, wq, kdt, qk, gl = _gdn_prep(proj, beta, gc, conv_w, l, batch, seq)
        dn, s_p = _gdn_scan(u, wq, kdt, qk, gl, proj, dnw, l, batch, seq)
        dn, *sample_states = _gdn_sample(proj, gate, conv_state, state_delta, conv_w, alog, dtb, dnw, dn,
                                         sample_states, l, rows_p)
        pool = _pool_prompt(proj, w_pool_b, scale, l, batch, seq)
        pool, pool_states = _pool_sample(proj, pool_state, w_pool_b, scale, pool, pool_states, l, rows_p)

        delta_p.append(s_p)
        conv_p.append(jnp.stack([proj[(b + 1) * seq - (CONV_W - 1):(b + 1) * seq, :3 * DN_WIDTH]
                                 for b in range(batch)]))
        pool_p.append(jnp.stack([proj[(b + 1) * seq - POOL_BUF:(b + 1) * seq, o1:] for b in range(batch)]))

        h = _outproj(dn, pool, w_out_b, h, l)
        if l < depth - 1:
            if hide_casts:
                h, w_gu_l, w_down_l = _ffn(h, nffn, w_gu_l, w_down_l, nfin, l, False,
                                           next_weights=(w_gate_up, w_down))
            else:
                h = _ffn(h, nffn, w_gu_l, w_down_l, nfin, l, False)
                w_gu_l, w_down_l = w_gate_up[l + 1].astype(BF16), w_down[l + 1].astype(BF16)
        elif rows_p % _row_tile(h.shape[0], FFN_ROW_TILE) == 0:
            h = _ffn(h, nffn, w_gu_l, w_down_l, nfin, l, True)
            y_p, y_s = h[:rows_p], h[rows_p:]
        else:
            y_p, y_s = _ffn(h, nffn, w_gu_l, w_down_l, nfin, l, True, split_rows=rows_p)

    delta_s, conv_s = sample_states
    y_prompt = y_p.reshape(batch, seq, D_MODEL)
    y_sample = y_s.reshape(nseq, 1, D_MODEL)
    return (y_prompt, y_sample, jnp.stack(delta_p), jnp.stack(conv_p), jnp.stack(pool_p),
            delta_s, jnp.transpose(conv_s, (0, 2, 1, 3)), jnp.transpose(pool_states, (0, 2, 1, 3)))
```

```python
import functools

import jax
import jax.numpy as jnp
from jax import lax
from jax.experimental import pallas as pl
from jax.experimental.pallas import tpu as pltpu

F32 = jnp.float32
BF16 = jnp.bfloat16

D_MODEL = 2048
DN_WIDTH = 1024
DN_HEADS = 8
HEAD_DIM = 128
CONV_W = 4
CHUNK = 128
POOL_WIDTH = 1024
POOL_WINDOWS = (2, 4, 8, 16)
POOL_GROUP = 256
POOL_BUF = 15
D_FF = 5632
EPS = 1e-6
PAST_LEN = 16384

MAIN_COLS = 4 * DN_WIDTH + POOL_WIDTH
GATE_COLS = 256
LANES = 128
SUBLANES = 8

ROW_TILES = (1040, 640, 512, 256, 128)
INPROJ_COL_TILE = 1024
OUTPROJ_ROW_TILE = 640
OUTPROJ_COL_TILE = 2048
FFN_ROW_TILE = 640
FFN_TILE = 512
CONV_ROWS = 256
PREP_ROWS = 2048
PREP_UNROLL = 16
SCAN_HEADS = 8
SCAN_ROWS = 512
POOL_ROWS = 512
SAMPLE_BLOCK = 16
SAMPLE_UNROLL = 2
POOL_SAMPLE_BLOCK = 32
MIB = 1024 * 1024


def _params(semantics, vmem_mib):
    return pltpu.CompilerParams(dimension_semantics=semantics, vmem_limit_bytes=vmem_mib * MIB)


def _silu(x):
    return x * jax.nn.sigmoid(x)


def _softplus(x):
    return jnp.maximum(x, 0.0) + jnp.log1p(jnp.exp(-jnp.abs(x)))


def _rms(x, w):
    return x * lax.rsqrt(jnp.mean(x * x, axis=-1, keepdims=True) + EPS) * w


def _l2(x):
    return x * lax.rsqrt(jnp.sum(x * x, axis=-1, keepdims=True) + EPS)


def _dot(a, b):
    return jnp.dot(a, b, preferred_element_type=F32)


def _row_tile(m, largest=ROW_TILES[0]):
    return next(t for t in ROW_TILES if t <= largest and m % t == 0)


def _any_spec():
    return pl.BlockSpec(memory_space=pl.ANY)


def _inproj_body(h_ref, nw_ref, wa_ref, wb_ref, wg_ref, proj_ref, gate_ref, xn_ref, *, n_a):
    j = pl.program_id(1)

    @pl.when(j == 0)
    def _():
        xn = _rms(h_ref[...], nw_ref[...]).astype(BF16)
        xn_ref[...] = xn
        gate_ref[...] = _dot(xn, wg_ref[...])

    @pl.when(j < n_a)
    def _():
        proj_ref[...] = _dot(xn_ref[...], wa_ref[...])

    @pl.when(j >= n_a)
    def _():
        proj_ref[...] = _dot(xn_ref[...], wb_ref[...])


def _inproj(h, nw, w_a, w_b, w_gate, l):
    m = h.shape[0]
    tm, tn = _row_tile(m), INPROJ_COL_TILE
    n_a = (MAIN_COLS - w_b.shape[-1]) // tn
    return pl.pallas_call(
        functools.partial(_inproj_body, n_a=n_a),
        grid=(m // tm, MAIN_COLS // tn),
        in_specs=[
            pl.BlockSpec((tm, D_MODEL), lambda i, j: (i, 0)),
            pl.BlockSpec((None, 1, D_MODEL), lambda i, j: (l, 0, 0)),
            pl.BlockSpec((None, D_MODEL, tn), lambda i, j: (l, 0, jnp.minimum(j, n_a - 1))),
            pl.BlockSpec((None, D_MODEL, tn), lambda i, j: (l, 0, jnp.maximum(j - n_a, 0))),
            pl.BlockSpec((None, D_MODEL, GATE_COLS), lambda i, j: (l, 0, 0)),
        ],
        out_specs=[
            pl.BlockSpec((tm, tn), lambda i, j: (i, j)),
            pl.BlockSpec((tm, GATE_COLS), lambda i, j: (i, 0)),
        ],
        out_shape=[
            jax.ShapeDtypeStruct((m, MAIN_COLS), F32),
            jax.ShapeDtypeStruct((m, GATE_COLS), F32),
        ],
        scratch_shapes=[pltpu.VMEM((tm, D_MODEL), BF16)],
        compiler_params=_params(("parallel", "arbitrary"), 54),
        name="inproj",
    )(h, nw, w_a, w_b, w_gate)


def _gates_body(gate_ref, alog_ref, dtb_ref, beta_ref, gc_ref):
    beta_ref[...] = jax.nn.sigmoid(gate_ref[:, 0:LANES])
    g = -jnp.exp(alog_ref[...]) * _softplus(gate_ref[:, LANES:2 * LANES] + dtb_ref[...])
    pos = lax.broadcasted_iota(jnp.int32, g.shape, 0) & (CHUNK - 1)
    sh = 1
    while sh < CHUNK:
        g = g + jnp.where(pos >= sh, pltpu.roll(g, sh, axis=0), 0.0)
        sh *= 2
    gc_ref[...] = g


def _gates(gate, alog, dtb, l, batch, seq):
    vec = pl.BlockSpec((None, 1, LANES), lambda b: (l, 0, 0))
    return pl.pallas_call(
        _gates_body,
        grid=(batch,),
        in_specs=[pl.BlockSpec((seq, GATE_COLS), lambda b: (b, 0)), vec, vec],
        out_specs=[pl.BlockSpec((seq, LANES), lambda b: (b, 0))] * 2,
        out_shape=[jax.ShapeDtypeStruct((batch * seq, LANES), F32)] * 2,
        compiler_params=_params(("parallel",), 32),
        name="gates",
    )(gate, alog, dtb)


def _gdn_prep_body(q_ref, k_ref, v_ref, qh_ref, kh_ref, vh_ref, cwq_ref, cwk_ref, cwv_ref, beta_ref, gc_ref,
                   u_ref, wq_ref, kdt_ref, qk_ref, gl_ref, qs, ks, vs):
    head = pl.program_id(1)
    r = pl.program_id(2)
    rp = q_ref.shape[0]
    rb = min(CONV_ROWS, rp)

    def conv_silu(x_ref, halo_ref, cw_ref, i):
        cw = cw_ref[...]
        if i == 0:
            halo = jnp.where(r > 0, halo_ref[...], 0.0)
            ext = jnp.concatenate([halo, x_ref[0:rb, :]], axis=0)
            off = SUBLANES - (CONV_W - 1)
            taps = [ext[off + t:off + t + rb] for t in range(CONV_W)]
        else:
            off = i * rb - (CONV_W - 1)
            taps = [x_ref[off + t:off + t + rb, :] for t in range(CONV_W)]
        y = taps[0] * cw[0:1]
        for t in range(1, CONV_W):
            y = y + taps[t] * cw[t:t + 1]
        return _silu(y)

    def normalise(i):
        rows = slice(i * rb, (i + 1) * rb)
        qs[rows, :] = _l2(conv_silu(q_ref, qh_ref, cwq_ref, i)) * (HEAD_DIM ** -0.5)
        ks[rows, :] = _l2(conv_silu(k_ref, kh_ref, cwk_ref, i))
        vs[rows, :] = conv_silu(v_ref, vh_ref, cwv_ref, i)

    head_lane = lax.broadcasted_iota(jnp.int32, (CHUNK, LANES), 1) == head
    ri = lax.broadcasted_iota(jnp.int32, (CHUNK, CHUNK), 0)
    ci = lax.broadcasted_iota(jnp.int32, (CHUNK, CHUNK), 1)
    tril = ri >= ci
    strict = ri > ci
    half = CHUNK // 2
    same_half = (ri >= half) == (ci >= half)
    lower_left = (ri >= half) & (ci < half)
    eye = (ri == ci).astype(F32)
    nt_dims = (((1,), (1,)), ((), ()))

    def lane_bcast(x):
        return jnp.broadcast_to(jnp.sum(jnp.where(head_lane, x, 0.0), axis=1, keepdims=True), (CHUNK, LANES))

    def group(js):
        rows = [slice(j * CHUNK, (j + 1) * CHUNK) for j in js]
        q = [qs[r, :] for r in rows]
        k = [ks[r, :] for r in rows]
        bb = [lane_bcast(beta_ref[r, :]) for r in rows]
        gc = [lane_bcast(gc_ref[r, :]) for r in rows]
        decay = [jnp.where(tril, jnp.exp(jnp.where(tril, x - x.T, 0.0)), 0.0) for x in gc]
        kb = [a * b for a, b in zip(k, bb)]
        a1 = [lax.dot_general(jnp.concatenate([a, b], axis=0), x, nt_dims, preferred_element_type=F32)
              for a, b, x in zip(q, kb, k)]
        for r, a, d in zip(rows, a1, decay):
            qk_ref[r, :] = (a[:CHUNK] * d).astype(BF16)
        kbk = [a[CHUNK:] * d for a, d in zip(a1, decay)]
        m_d = [jnp.where(strict & same_half, x, 0.0) for x in kbk]
        corner = [jnp.where(lower_left, x, 0.0) for x in kbk]
        inv = [eye - x for x in m_d]
        mp = m_d
        sh = 1
        while 4 * sh < CHUNK:
            sh *= 2
            mp = [_dot(x, x) for x in mp]
            inv = [a + _dot(a, x) for a, x in zip(inv, mp)]
        dc = [_dot(a, x) for a, x in zip(inv, corner)]
        inv = [a - _dot(x, a) for a, x in zip(inv, dc)]
        eg = [jnp.exp(x) for x in gc]
        uw = [_dot(a, jnp.concatenate([vs[r, :] * b, x * e], axis=1))
              for a, r, b, x, e in zip(inv, rows, bb, kb, eg)]
        for j, r, x, a, e, kj, gcx in zip(js, rows, uw, q, eg, k, gc):
            u_ref[r, :] = x[:, :HEAD_DIM]
            wq_ref[2 * j * CHUNK:2 * (j + 1) * CHUNK, :] = jnp.concatenate(
                [x[:, HEAD_DIM:], a * e], axis=0).astype(BF16)
            glast = gcx[CHUNK - 1:CHUNK, :]
            kdt_ref[:, r] = (kj * jnp.exp(glast - gcx)).T.astype(BF16)
            gl_ref[j * SUBLANES:(j + 1) * SUBLANES, :] = jnp.broadcast_to(jnp.exp(glast), (SUBLANES, LANES))

    n_chunks = rp // CHUNK
    unroll = min(PREP_UNROLL, n_chunks)
    blocks_per_part = max(unroll * CHUNK // rb, 1)
    for part in range(n_chunks // unroll):
        for i in range(part * blocks_per_part, min((part + 1) * blocks_per_part, rp // rb)):
            normalise(i)
        group([part * unroll + t for t in range(unroll)])


def _gdn_prep(proj, beta, gc, conv_w, l, batch, seq):
    nh = DN_HEADS
    rp = min(PREP_ROWS, seq)
    nr = seq // rp
    n_chunks = seq // CHUNK
    col = lambda base: (lambda b, h, r: (b * nr + r, base + h))
    halo = lambda base: (lambda b, h, r: (jnp.maximum((b * nr + r) * (rp // SUBLANES) - 1, 0), base + h))
    cwcol = lambda base: (lambda b, h, r: (l, 0, base + h))
    gate_spec = pl.BlockSpec((rp, LANES), lambda b, h, r: (b * nr + r, 0))
    per_head = lambda rows, cols: pl.BlockSpec((None, None, rows, cols), lambda b, h, r: (b, h, r, 0))
    return pl.pallas_call(
        _gdn_prep_body,
        grid=(batch, nh, nr),
        in_specs=[
            pl.BlockSpec((rp, HEAD_DIM), col(0)),
            pl.BlockSpec((rp, HEAD_DIM), col(nh)),
            pl.BlockSpec((rp, HEAD_DIM), col(2 * nh)),
            pl.BlockSpec((SUBLANES, HEAD_DIM), halo(0)),
            pl.BlockSpec((SUBLANES, HEAD_DIM), halo(nh)),
            pl.BlockSpec((SUBLANES, HEAD_DIM), halo(2 * nh)),
            pl.BlockSpec((None, CONV_W, HEAD_DIM), cwcol(0)),
            pl.BlockSpec((None, CONV_W, HEAD_DIM), cwcol(nh)),
            pl.BlockSpec((None, CONV_W, HEAD_DIM), cwcol(2 * nh)),
            gate_spec, gate_spec,
        ],
        out_specs=[
            per_head(rp, HEAD_DIM),
            per_head(2 * rp, HEAD_DIM),
            pl.BlockSpec((None, None, HEAD_DIM, rp), lambda b, h, r: (b, h, 0, r)),
            per_head(rp, HEAD_DIM),
            per_head((rp // CHUNK) * SUBLANES, LANES),
        ],
        out_shape=[
            jax.ShapeDtypeStruct((batch, nh, seq, HEAD_DIM), F32),
            jax.ShapeDtypeStruct((batch, nh, 2 * seq, HEAD_DIM), BF16),
            jax.ShapeDtypeStruct((batch, nh, HEAD_DIM, seq), BF16),
            jax.ShapeDtypeStruct((batch, nh, seq, CHUNK), BF16),
            jax.ShapeDtypeStruct((batch, nh, n_chunks * SUBLANES, LANES), F32),
        ],
        scratch_shapes=[pltpu.VMEM((rp, HEAD_DIM), F32)] * 3,
        compiler_params=_params(("parallel", "parallel", "parallel"), 32),
        name="gdn_prep",
    )(proj, proj, proj, proj, proj, proj, conv_w, conv_w, conv_w, beta, gc)


def _gdn_scan_body(u_ref, wq_ref, kdt_ref, qk_ref, gl_ref, z_ref, dnw_ref, o_ref, s_out_ref, st_s):
    hs, rows_per_step = u_ref.shape[0], u_ref.shape[1]
    step = pl.program_id(2)

    @pl.when(step == 0)
    def _():
        st_s[...] = jnp.zeros(st_s.shape, F32)

    def chunk(c, carry):
        rows = pl.ds(pl.multiple_of(c * CHUNK, CHUNK), CHUNK)
        wrows = pl.ds(pl.multiple_of(c * 2 * CHUNK, 2 * CHUNK), 2 * CHUNK)
        grows = pl.ds(pl.multiple_of(c * SUBLANES, SUBLANES), SUBLANES)
        heads = range(hs)
        st = [st_s[hh] for hh in heads]
        r = [_dot(wq_ref[hh, wrows, :], st[hh].astype(BF16)) for hh in heads]
        vnew = [(u_ref[hh, rows, :] - r[hh][:CHUNK]).astype(BF16) for hh in heads]
        for hh in heads:
            st_s[hh] = st[hh] * gl_ref[hh, grows, :][0:1, :] + _dot(kdt_ref[hh, :, rows], vnew[hh])
        o = [r[hh][CHUNK:] + _dot(qk_ref[hh, rows, :], vnew[hh]) for hh in heads]
        for hh in heads:
            lo = hh * HEAD_DIM
            o_ref[rows, lo:lo + HEAD_DIM] = (
                _rms(o[hh], dnw_ref[...]) * _silu(z_ref[rows, lo:lo + HEAD_DIM])).astype(BF16)
        return carry

    lax.fori_loop(0, rows_per_step // CHUNK, chunk, 0)

    @pl.when(step == pl.num_programs(2) - 1)
    def _():
        s_out_ref[...] = st_s[...]


def _gdn_scan(u, wq, kdt, qk, gl, proj, dnw, l, batch, seq):
    hs = SCAN_HEADS
    rs = min(SCAN_ROWS, seq)
    nr = seq // rs
    zcol = (3 * DN_WIDTH) // (hs * HEAD_DIM)
    per_group = lambda rows, cols: pl.BlockSpec((None, hs, rows, cols), lambda b, g, r: (b, g, r, 0))
    return pl.pallas_call(
        _gdn_scan_body,
        grid=(batch, DN_HEADS // hs, nr),
        in_specs=[
            per_group(rs, HEAD_DIM),
            per_group(2 * rs, HEAD_DIM),
            pl.BlockSpec((None, hs, HEAD_DIM, rs), lambda b, g, r: (b, g, 0, r)),
            per_group(rs, CHUNK),
            per_group((rs // CHUNK) * SUBLANES, LANES),
            pl.BlockSpec((rs, hs * HEAD_DIM), lambda b, g, r: (b * nr + r, zcol + g)),
            pl.BlockSpec((None, 1, HEAD_DIM), lambda b, g, r: (l, 0, 0)),
        ],
        out_specs=[
            pl.BlockSpec((rs, hs * HEAD_DIM), lambda b, g, r: (b * nr + r, g)),
            pl.BlockSpec((None, hs, HEAD_DIM, HEAD_DIM), lambda b, g, r: (b, g, 0, 0)),
        ],
        out_shape=[
            jax.ShapeDtypeStruct((batch * seq, DN_WIDTH), BF16),
            jax.ShapeDtypeStruct((batch, DN_HEADS, HEAD_DIM, HEAD_DIM), F32),
        ],
        scratch_shapes=[pltpu.VMEM((hs, HEAD_DIM, HEAD_DIM), F32)],
        compiler_params=_params(("parallel", "parallel", "arbitrary"), 40),
        name="gdn_scan",
    )(u, wq, kdt, qk, gl, proj, dnw)


def _gdn_sample_body(qkv_ref, z_ref, gate_ref, cs_ref, s_ref, cw_ref, alog_ref, dtb_ref, dnw_ref, *rest):
    dn_ref, sn_ref, cn_ref, q_s, k_s, v_s, eg_s, beta_s, o_s = rest[-9:]
    nb = qkv_ref.shape[0]
    x = qkv_ref[...]
    c0, c1, c2 = cs_ref[0], cs_ref[1], cs_ref[2]
    cw = cw_ref[...]
    y = c0 * cw[0:1] + c1 * cw[1:2] + c2 * cw[2:3] + x * cw[3:4]
    y = _silu(y)
    cn_ref[0] = c1
    cn_ref[1] = c2
    cn_ref[2] = x
    for h in range(DN_HEADS):
        lo = h * HEAD_DIM
        q_s[h] = _l2(y[:, lo:lo + HEAD_DIM]) * (HEAD_DIM ** -0.5)
        k_s[h] = _l2(y[:, DN_WIDTH + lo:DN_WIDTH + lo + HEAD_DIM])
        v_s[h] = y[:, 2 * DN_WIDTH + lo:2 * DN_WIDTH + lo + HEAD_DIM]
    beta_s[...] = jax.nn.sigmoid(gate_ref[:, 0:LANES])
    eg_s[...] = jnp.exp(-jnp.exp(alog_ref[...]) * _softplus(gate_ref[:, LANES:2 * LANES] + dtb_ref[...]))

    pad7 = jnp.zeros((SUBLANES - 1, HEAD_DIM), F32)
    pad6 = jnp.zeros((SUBLANES - 2, HEAD_DIM), F32)

    tn_dims = (((0,), (0,)), ((), ()))

    def per_samples(i, carry):
        units = [(i * SAMPLE_UNROLL + t, h) for t in range(SAMPLE_UNROLL) for h in range(DN_HEADS)]
        k = [k_s[h, pl.ds(b, 1), :] for b, h in units]
        q = [q_s[h, pl.ds(b, 1), :] for b, h in units]
        st = [s_ref[b, h] * eg_s[pl.ds(b, 1), :][:, h:h + 1] for b, h in units]
        r = [_dot(jnp.concatenate([kk, qq, pad6], axis=0), s) for kk, qq, s in zip(k, q, st)]
        dv = [(v_s[h, pl.ds(b, 1), :] - x[0:1]) * beta_s[pl.ds(b, 1), :][:, h:h + 1]
              for (b, h), x in zip(units, r)]
        outer = [lax.dot_general(jnp.concatenate([kk, pad7], axis=0), jnp.concatenate([d, pad7], axis=0),
                                 tn_dims, preferred_element_type=F32) for kk, d in zip(k, dv)]
        for (b, h), s, x, o, kk, qq, d in zip(units, st, outer, r, k, q, dv):
            sn_ref[b, h] = s + x
            o_s[h, pl.ds(b, 1), :] = o[1:2] + jnp.sum(qq * kk, axis=-1, keepdims=True) * d
        return carry

    lax.fori_loop(0, nb // SAMPLE_UNROLL, per_samples, 0)

    z = z_ref[...]
    for h in range(DN_HEADS):
        lo = h * HEAD_DIM
        dn_ref[:, lo:lo + HEAD_DIM] = (_rms(o_s[h], dnw_ref[...]) * _silu(z[:, lo:lo + HEAD_DIM])).astype(BF16)


def _gdn_sample(proj, gate, state_conv, state_delta, conv_w, alog, dtb, dnw, prev, l, row0):
    depth, nseq = state_delta.shape[:2]
    nb = SAMPLE_BLOCK
    base = row0 // nb
    vec = pl.BlockSpec((None, 1, LANES), lambda i: (l, 0, 0))
    delta_spec = pl.BlockSpec((None, nb, DN_HEADS, HEAD_DIM, HEAD_DIM), lambda i: (l, i, 0, 0, 0))
    conv_spec = pl.BlockSpec((None, CONV_W - 1, nb, 3 * DN_WIDTH), lambda i: (l, 0, i, 0))
    inputs = [proj, proj, gate, state_conv, state_delta, conv_w, alog, dtb, dnw]
    in_specs = [
        pl.BlockSpec((nb, 3 * DN_WIDTH), lambda i: (base + i, 0)),
        pl.BlockSpec((nb, DN_WIDTH), lambda i: (base + i, 3)),
        pl.BlockSpec((nb, GATE_COLS), lambda i: (base + i, 0)),
        conv_spec, delta_spec,
        pl.BlockSpec((None, CONV_W, 3 * DN_WIDTH), lambda i: (l, 0, 0)),
        vec, vec, vec,
    ]
    aliases = {}
    if prev is not None:
        inputs += list(prev)
        in_specs += [_any_spec(), _any_spec()]
        aliases = {9: 1, 10: 2}
    return pl.pallas_call(
        _gdn_sample_body,
        grid=(nseq // nb,),
        in_specs=in_specs,
        out_specs=[pl.BlockSpec((nb, DN_WIDTH), lambda i: (i, 0)), delta_spec, conv_spec],
        out_shape=[
            jax.ShapeDtypeStruct((nseq, DN_WIDTH), BF16),
            jax.ShapeDtypeStruct(state_delta.shape, F32),
            jax.ShapeDtypeStruct(state_conv.shape, F32),
        ],
        scratch_shapes=[
            pltpu.VMEM((DN_HEADS, nb, HEAD_DIM), F32),
            pltpu.VMEM((DN_HEADS, nb, HEAD_DIM), F32),
            pltpu.VMEM((DN_HEADS, nb, HEAD_DIM), F32),
            pltpu.VMEM((nb, LANES), F32),
            pltpu.VMEM((nb, LANES), F32),
            pltpu.VMEM((DN_HEADS, nb, HEAD_DIM), F32),
        ],
        input_output_aliases=aliases,
        compiler_params=_params(("parallel",), 48),
        name="gdn_sample",
    )(*inputs)


def _pool_matmul(d, w_ref, sc_ref, o_ref, gi):
    lo = gi * POOL_GROUP
    y = _dot(d.astype(BF16), w_ref[gi]) * sc_ref[:, lo:lo + POOL_GROUP]
    o_ref[:, lo:lo + POOL_GROUP] = y.astype(BF16)


def _pool_prompt_body(p_ref, prev_ref, w_ref, sc_ref, o_ref, *, tiles_per_seq):
    rt = p_ref.shape[0]
    hist = prev_ref.shape[0]
    t = pl.program_id(0) % tiles_per_seq
    cur = p_ref[...]
    prev = jnp.where(t == 0, 0.0, prev_ref[...])
    pos = t * rt + lax.broadcasted_iota(jnp.int32, (rt, POOL_GROUP), 0)
    for gi, win in enumerate(POOL_WINDOWS):
        lo = gi * POOL_GROUP
        x = cur[:, lo:lo + POOL_GROUP]
        s = jnp.concatenate([prev[:, lo:lo + POOL_GROUP], x], axis=0)
        sh = 1
        while sh < win:
            s = s + pltpu.roll(s, sh, axis=0)
            sh *= 2
        cnt = jnp.minimum(pos + 1, win).astype(F32)
        _pool_matmul(s[hist:] / cnt - x, w_ref, sc_ref, o_ref, gi)


def _pool_prompt(proj, w_pool, scale, l, batch, seq):
    rt = min(POOL_ROWS, seq)
    hist = POOL_BUF + 1
    tiles_per_seq = seq // rt
    pcol = (4 * DN_WIDTH) // POOL_WIDTH
    ng = len(POOL_WINDOWS)
    return pl.pallas_call(
        functools.partial(_pool_prompt_body, tiles_per_seq=tiles_per_seq),
        grid=(batch * tiles_per_seq,),
        in_specs=[
            pl.BlockSpec((rt, POOL_WIDTH), lambda r: (r, pcol)),
            pl.BlockSpec((hist, POOL_WIDTH), lambda r: (jnp.maximum(r * (rt // hist) - 1, 0), pcol)),
            pl.BlockSpec((None, ng, POOL_GROUP, POOL_GROUP), lambda r: (l, 0, 0, 0)),
            pl.BlockSpec((None, 1, POOL_WIDTH), lambda r: (l, 0, 0)),
        ],
        out_specs=pl.BlockSpec((rt, POOL_WIDTH), lambda r: (r, 0)),
        out_shape=jax.ShapeDtypeStruct((batch * seq, POOL_WIDTH), BF16),
        compiler_params=_params(("parallel",), 32),
        name="pool_prompt",
    )(proj, proj, w_pool, scale)


def _pool_sample_body(p_ref, buf_ref, w_ref, sc_ref, *rest):
    o_ref, nb_ref = rest[-2:]
    cur = p_ref[...]
    for gi, win in enumerate(POOL_WINDOWS):
        lo = gi * POOL_GROUP
        x = cur[:, lo:lo + POOL_GROUP]
        s = x
        for j in range(1, win):
            s = s + buf_ref[POOL_BUF - j, :, lo:lo + POOL_GROUP]
        cnt = float(min(PAST_LEN + 1, win))
        _pool_matmul(s / cnt - x, w_ref, sc_ref, o_ref, gi)
    for j in range(POOL_BUF - 1):
        nb_ref[j] = buf_ref[j + 1]
    nb_ref[POOL_BUF - 1] = cur


def _pool_sample(proj, state_pool, w_pool, scale, prev, l, row0):
    nseq = state_pool.shape[2]
    nb = POOL_SAMPLE_BLOCK
    base = row0 // nb
    pcol = (4 * DN_WIDTH) // POOL_WIDTH
    ng = len(POOL_WINDOWS)
    state_spec = pl.BlockSpec((None, POOL_BUF, nb, POOL_WIDTH), lambda i: (l, 0, i, 0))
    inputs = [proj, state_pool, w_pool, scale]
    in_specs = [
        pl.BlockSpec((nb, POOL_WIDTH), lambda i: (base + i, pcol)),
        state_spec,
        pl.BlockSpec((None, ng, POOL_GROUP, POOL_GROUP), lambda i: (l, 0, 0, 0)),
        pl.BlockSpec((None, 1, POOL_WIDTH), lambda i: (l, 0, 0)),
    ]
    aliases = {}
    if prev is not None:
        inputs.append(prev)
        in_specs.append(_any_spec())
        aliases = {4: 1}
    return pl.pallas_call(
        _pool_sample_body,
        grid=(nseq // nb,),
        in_specs=in_specs,
        out_specs=[pl.BlockSpec((nb, POOL_WIDTH), lambda i: (i, 0)), state_spec],
        out_shape=[
            jax.ShapeDtypeStruct((nseq, POOL_WIDTH), BF16),
            jax.ShapeDtypeStruct(state_pool.shape, F32),
        ],
        input_output_aliases=aliases,
        compiler_params=_params(("parallel",), 32),
        name="pool_sample",
    )(*inputs)


def _outproj_body(dn_ref, pool_ref, dns_ref, pools_ref, w1_ref, w2_ref, h_ref, o_ref, *, sample_row0):
    def project(dn, pool):
        o_ref[...] = h_ref[...] + _dot(dn, w1_ref[...]) + _dot(pool, w2_ref[...])

    is_last = pl.program_id(0) == pl.num_programs(0) - 1

    @pl.when(jnp.logical_not(is_last))
    def _():
        project(dn_ref[...], pool_ref[...])

    @pl.when(is_last)
    def _():
        project(jnp.concatenate([dn_ref[:sample_row0, :], dns_ref[...]], axis=0),
                jnp.concatenate([pool_ref[:sample_row0, :], pools_ref[...]], axis=0))


def _outproj(dn_p, dn_s, pool_p, pool_s, w_out, h, l):
    m = h.shape[0]
    rows_p, nseq = dn_p.shape[0], dn_s.shape[0]
    tm, tn = _row_tile(m, OUTPROJ_ROW_TILE), OUTPROJ_COL_TILE
    sample_row0 = rows_p - (m // tm - 1) * tm
    assert 0 < sample_row0 and sample_row0 + nseq == tm, (m, tm, rows_p, nseq)
    return pl.pallas_call(
        functools.partial(_outproj_body, sample_row0=sample_row0),
        grid=(m // tm, D_MODEL // tn),
        in_specs=[
            pl.BlockSpec((tm, DN_WIDTH), lambda i, j: (i, 0)),
            pl.BlockSpec((tm, POOL_WIDTH), lambda i, j: (i, 0)),
            pl.BlockSpec((nseq, DN_WIDTH), lambda i, j: (0, 0)),
            pl.BlockSpec((nseq, POOL_WIDTH), lambda i, j: (0, 0)),
            pl.BlockSpec((None, DN_WIDTH, tn), lambda i, j: (l, 0, j)),
            pl.BlockSpec((None, POOL_WIDTH, tn), lambda i, j: (l, 1, j)),
            pl.BlockSpec((tm, tn), lambda i, j: (i, j)),
        ],
        out_specs=pl.BlockSpec((tm, tn), lambda i, j: (i, j)),
        out_shape=jax.ShapeDtypeStruct((m, D_MODEL), F32),
        compiler_params=_params(("parallel", "parallel"), 54),
        name="outproj",
    )(dn_p, pool_p, dn_s, pool_s, w_out, w_out, h)


def _ffn_body(h_ref, nw_ref, wg_ref, wu_ref, wd_ref, nf_ref, *rest, final_norm, sample_row0, n_cast):
    cast_src, rest = rest[:n_cast], rest[n_cast:]
    o_ref, xn_ref = rest[0], rest[-1]
    cast_dst = rest[len(rest) - 1 - n_cast:len(rest) - 1]
    f = pl.program_id(1)
    last = pl.num_programs(1) - 1

    @pl.when(f == 0)
    def _():
        x = h_ref[...]
        xn_ref[...] = _rms(x, nw_ref[...]).astype(BF16)
        o_ref[...] = x

    xn = xn_ref[...]
    act = (_silu(_dot(xn, wg_ref[...])) * _dot(xn, wu_ref[...])).astype(BF16)
    o_ref[...] += _dot(act, wd_ref[...])
    for src, dst in zip(cast_src, cast_dst):
        dst[...] = src[...].astype(BF16)

    if final_norm:
        @pl.when(f == last)
        def _():
            o_ref[...] = _rms(o_ref[...], nf_ref[...])

    if sample_row0 is not None:
        @pl.when((f == last) & (pl.program_id(0) == pl.num_programs(0) - 1))
        def _():
            rest[1][...] = o_ref[sample_row0:, :]


def _ffn_cast_plan(m):
    steps = (m // _row_tile(m, FFN_ROW_TILE)) * (D_FF // FFN_TILE)
    n_units = 2 * D_FF // LANES
    for per_step in range(1, n_units + 1):
        if n_units % per_step == 0 and per_step * steps >= n_units:
            return per_step, n_units // per_step
    return None


def _ffn(h, nw, w_gate_up, w_down, nf, l, final_norm, split_rows=None, next_weights=None):
    m = h.shape[0]
    tm, tf = _row_tile(m, FFN_ROW_TILE), FFN_TILE
    nft = D_FF // tf
    out_spec = pl.BlockSpec((tm, D_MODEL), lambda i, f: (i, 0))
    if split_rows is None:
        out_specs, out_shape, sample_row0 = [out_spec], [jax.ShapeDtypeStruct((m, D_MODEL), F32)], None
    else:
        sample_row0 = split_rows - (m // tm - 1) * tm
        assert 0 < sample_row0 < tm, (m, tm, split_rows)
        out_specs = [out_spec, pl.BlockSpec((m - split_rows, D_MODEL), lambda i, f: (0, 0))]
        out_shape = [jax.ShapeDtypeStruct((split_rows, D_MODEL), F32),
                     jax.ShapeDtypeStruct((m - split_rows, D_MODEL), F32)]
    inputs = [h, nw, w_gate_up, w_gate_up, w_down, nf]
    in_specs = [
        pl.BlockSpec((tm, D_MODEL), lambda i, f: (i, 0)),
        pl.BlockSpec((None, 1, D_MODEL), lambda i, f: (l, 0, 0)),
        pl.BlockSpec((D_MODEL, tf), lambda i, f: (0, f)),
        pl.BlockSpec((D_MODEL, tf), lambda i, f: (0, nft + f)),
        pl.BlockSpec((tf, D_MODEL), lambda i, f: (f, 0)),
        pl.BlockSpec((1, D_MODEL), lambda i, f: (0, 0)),
    ]
    n_cast = 0
    if next_weights is not None:
        per_step, n_blocks = _ffn_cast_plan(m)
        cols, rows = per_step * LANES, per_step * (D_FF // (2 * D_FF // LANES))
        blk = lambda i, f: jnp.minimum(i * nft + f, n_blocks - 1)
        n_cast = 2
        inputs += list(next_weights)
        in_specs += [pl.BlockSpec((None, D_MODEL, cols), lambda i, f: (l + 1, 0, blk(i, f))),
                     pl.BlockSpec((None, rows, D_MODEL), lambda i, f: (l + 1, blk(i, f), 0))]
        out_specs += [pl.BlockSpec((D_MODEL, cols), lambda i, f: (0, blk(i, f))),
                      pl.BlockSpec((rows, D_MODEL), lambda i, f: (blk(i, f), 0))]
        out_shape += [jax.ShapeDtypeStruct((D_MODEL, 2 * D_FF), BF16), jax.ShapeDtypeStruct((D_FF, D_MODEL), BF16)]
    out = pl.pallas_call(
        functools.partial(_ffn_body, final_norm=final_norm, sample_row0=sample_row0, n_cast=n_cast),
        grid=(m // tm, nft),
        in_specs=in_specs,
        out_specs=out_specs,
        out_shape=out_shape,
        scratch_shapes=[pltpu.VMEM((tm, D_MODEL), BF16)],
        compiler_params=_params(("parallel", "arbitrary"), 48),
        name="ffn",
    )(*inputs)
    return out[0] if len(out) == 1 else out


def _lane_vecs(x):
    depth, n = x.shape
    return jnp.zeros((depth, 1, LANES), F32).at[:, 0, :n].set(x)


def kernel(x_prompt, x_sample, state_delta, state_conv, state_pool, norm_mix, w_in, conv_w, a_log, dt_bias,
           dn_norm, w_pool, pool_scale, w_out, norm_ffn, w_gate_up, w_down, norm_final):
    batch, seq, _ = x_prompt.shape
    nseq = x_sample.shape[0]
    depth = w_in.shape[0]
    rows_p = batch * seq
    h = jnp.concatenate([x_prompt.reshape(rows_p, D_MODEL), x_sample.reshape(nseq, D_MODEL)], axis=0)

    o1 = 4 * DN_WIDTH
    w_a = w_in.astype(BF16)
    w_b = w_a[:, :, o1 + 2 * DN_HEADS:]
    lane_pad = ((0, 0), (0, 0), (0, LANES - DN_HEADS))
    w_gate = jnp.concatenate([jnp.pad(w_a[:, :, o1:o1 + DN_HEADS], lane_pad),
                              jnp.pad(w_a[:, :, o1 + DN_HEADS:o1 + 2 * DN_HEADS], lane_pad)], axis=-1)
    w_out_b = w_out.astype(BF16)
    w_pool_b = w_pool.astype(BF16)
    alog, dtb = _lane_vecs(a_log), _lane_vecs(dt_bias)
    dnw = dn_norm.reshape(depth, 1, HEAD_DIM)
    scale = pool_scale.reshape(depth, 1, POOL_WIDTH)
    nmix = norm_mix.reshape(depth, 1, D_MODEL)
    nffn = norm_ffn.reshape(depth, 1, D_MODEL)
    nfin = norm_final.reshape(1, D_MODEL)
    conv_state = jnp.transpose(state_conv, (0, 2, 1, 3))
    pool_state = jnp.transpose(state_pool, (0, 2, 1, 3))
    hide_casts = _ffn_cast_plan(rows_p + nseq) is not None
    w_gu_l, w_down_l = w_gate_up[0].astype(BF16), w_down[0].astype(BF16)

    delta_p, conv_p, poolbuf_p = [], [], []
    sample_states = None
    pool_states = None
    for l in range(depth):
        proj, gate = _inproj(h, nmix, w_a, w_b, w_gate, l)

        beta, gc = _gates(gate, alog, dtb, l, batch, seq)
        u, wq, kdt, qk, gl = _gdn_prep(proj, beta, gc, conv_w, l, batch, seq)
        dn_p, s_p = _gdn_scan(u, wq, kdt, qk, gl, proj, dnw, l, batch, seq)
        dn_s, *sample_states = _gdn_sample(proj, gate, conv_state, state_delta, conv_w, alog, dtb, dnw,
                                           sample_states, l, rows_p)
        pool_p = _pool_prompt(proj, w_pool_b, scale, l, batch, seq)
        pool_s, pool_states = _pool_sample(proj, pool_state, w_pool_b, scale, pool_states, l, rows_p)

        delta_p.append(s_p)
        conv_p.append(jnp.stack([proj[(b + 1) * seq - (CONV_W - 1):(b + 1) * seq, :3 * DN_WIDTH]
                                 for b in range(batch)]))
        poolbuf_p.append(jnp.stack([proj[(b + 1) * seq - POOL_BUF:(b + 1) * seq, o1:] for b in range(batch)]))

        h = _outproj(dn_p, dn_s, pool_p, pool_s, w_out_b, h, l)
        if l < depth - 1:
            if hide_casts:
                h, w_gu_l, w_down_l = _ffn(h, nffn, w_gu_l, w_down_l, nfin, l, False,
                                           next_weights=(w_gate_up, w_down))
            else:
                h = _ffn(h, nffn, w_gu_l, w_down_l, nfin, l, False)
                w_gu_l, w_down_l = w_gate_up[l + 1].astype(BF16), w_down[l + 1].astype(BF16)
        elif rows_p % _row_tile(h.shape[0], FFN_ROW_TILE) == 0:
            h = _ffn(h, nffn, w_gu_l, w_down_l, nfin, l, True)
            y_p, y_s = h[:rows_p], h[rows_p:]
        else:
            y_p, y_s = _ffn(h, nffn, w_gu_l, w_down_l, nfin, l, True, split_rows=rows_p)

    delta_s, conv_s = sample_states
    y_prompt = y_p.reshape(batch, seq, D_MODEL)
    y_sample = y_s.reshape(nseq, 1, D_MODEL)
    return (y_prompt, y_sample, jnp.stack(delta_p), jnp.stack(conv_p), jnp.stack(poolbuf_p),
            delta_s, jnp.transpose(conv_s, (0, 2, 1, 3)), jnp.transpose(pool_states, (0, 2, 1, 3)))
```

```python
import functools

import jax
import jax.numpy as jnp
from jax import lax
from jax.experimental import pallas as pl
from jax.experimental.pallas import tpu as pltpu

F32 = jnp.float32
BF16 = jnp.bfloat16

D_MODEL = 2048
DN_WIDTH = 1024
DN_HEADS = 8
HEAD_DIM = 128
CONV_W = 4
CHUNK = 128
POOL_WIDTH = 1024
POOL_WINDOWS = (2, 4, 8, 16)
POOL_GROUP = 256
POOL_BUF = 15
D_FF = 5632
EPS = 1e-6
PAST_LEN = 16384

MAIN_COLS = 4 * DN_WIDTH + POOL_WIDTH
GATE_COLS = 256
LANES = 128
SUBLANES = 8

ROW_TILES = (1040, 640, 512, 256, 128)
INPROJ_COL_TILE = 1024
OUTPROJ_ROW_TILE = 640
OUTPROJ_COL_TILE = 2048
FFN_ROW_TILE = 640
FFN_TILE = 512
CONV_ROWS = 256
PREP_ROWS = 2048
PREP_UNROLL = 16
SCAN_HEADS = 8
SCAN_ROWS = 512
POOL_ROWS = 512
SAMPLE_BLOCK = 16
SAMPLE_UNROLL = 2
POOL_SAMPLE_BLOCK = 32
MIB = 1024 * 1024


def _params(semantics, vmem_mib):
    return pltpu.CompilerParams(dimension_semantics=semantics, vmem_limit_bytes=vmem_mib * MIB)


def _silu(x):
    return x * jax.nn.sigmoid(x)


def _softplus(x):
    return jnp.maximum(x, 0.0) + jnp.log1p(jnp.exp(-jnp.abs(x)))


def _rms(x, w):
    return x * lax.rsqrt(jnp.mean(x * x, axis=-1, keepdims=True) + EPS) * w


def _l2(x):
    return x * lax.rsqrt(jnp.sum(x * x, axis=-1, keepdims=True) + EPS)


def _dot(a, b):
    return jnp.dot(a, b, preferred_element_type=F32)


def _row_tile(m, largest=ROW_TILES[0]):
    return next(t for t in ROW_TILES if t <= largest and m % t == 0)


def _any_spec():
    return pl.BlockSpec(memory_space=pl.ANY)


def _stacked_rows(head_ref, tail_ref, tail_row0, is_last, fn):
    if tail_ref is None:
        fn(head_ref[...])
        return

    @pl.when(jnp.logical_not(is_last))
    def _():
        fn(head_ref[...])

    @pl.when(is_last)
    def _():
        fn(jnp.concatenate([head_ref[:tail_row0, :], tail_ref[...]], axis=0))


def _tail_row0(rows_head, rows_tail, tm):
    row0 = rows_head - ((rows_head + rows_tail) // tm - 1) * tm
    assert 0 < row0 and row0 + rows_tail == tm, (rows_head, rows_tail, tm)
    return row0


def _inproj_body(h_ref, *rest, n_a, tail_row0):
    tail_ref = None if tail_row0 is None else rest[0]
    nw_ref, wa_ref, wb_ref, wg_ref, proj_ref, gate_ref, xn_ref = rest[-7:]
    j = pl.program_id(1)
    is_last = pl.program_id(0) == pl.num_programs(0) - 1

    def normalise(x):
        xn = _rms(x, nw_ref[...]).astype(BF16)
        xn_ref[...] = xn
        gate_ref[...] = _dot(xn, wg_ref[...])

    @pl.when(j == 0)
    def _():
        _stacked_rows(h_ref, tail_ref, tail_row0, is_last, normalise)

    @pl.when(j < n_a)
    def _():
        proj_ref[...] = _dot(xn_ref[...], wa_ref[...])

    @pl.when(j >= n_a)
    def _():
        proj_ref[...] = _dot(xn_ref[...], wb_ref[...])


def _inproj(h, h_tail, nw, w_a, w_b, w_gate, l):
    m = h.shape[0] + (0 if h_tail is None else h_tail.shape[0])
    tm, tn = _row_tile(m), INPROJ_COL_TILE
    n_a = (MAIN_COLS - w_b.shape[-1]) // tn
    tail_row0 = None if h_tail is None else _tail_row0(h.shape[0], h_tail.shape[0], tm)
    tail = [] if h_tail is None else [h_tail]
    tail_specs = [] if h_tail is None else [pl.BlockSpec(h_tail.shape, lambda i, j: (0, 0))]
    return pl.pallas_call(
        functools.partial(_inproj_body, n_a=n_a, tail_row0=tail_row0),
        grid=(m // tm, MAIN_COLS // tn),
        in_specs=[pl.BlockSpec((tm, D_MODEL), lambda i, j: (i, 0))] + tail_specs + [
            pl.BlockSpec((None, 1, D_MODEL), lambda i, j: (l, 0, 0)),
            pl.BlockSpec((None, D_MODEL, tn), lambda i, j: (l, 0, jnp.minimum(j, n_a - 1))),
            pl.BlockSpec((None, D_MODEL, tn), lambda i, j: (l, 0, jnp.maximum(j - n_a, 0))),
            pl.BlockSpec((None, D_MODEL, GATE_COLS), lambda i, j: (l, 0, 0)),
        ],
        out_specs=[
            pl.BlockSpec((tm, tn), lambda i, j: (i, j)),
            pl.BlockSpec((tm, GATE_COLS), lambda i, j: (i, 0)),
        ],
        out_shape=[
            jax.ShapeDtypeStruct((m, MAIN_COLS), F32),
            jax.ShapeDtypeStruct((m, GATE_COLS), F32),
        ],
        scratch_shapes=[pltpu.VMEM((tm, D_MODEL), BF16)],
        compiler_params=_params(("parallel", "arbitrary"), 54),
        name="inproj",
    )(h, *tail, nw, w_a, w_b, w_gate)


def _gates_body(gate_ref, alog_ref, dtb_ref, beta_ref, gc_ref):
    beta_ref[...] = jax.nn.sigmoid(gate_ref[:, 0:LANES])
    g = -jnp.exp(alog_ref[...]) * _softplus(gate_ref[:, LANES:2 * LANES] + dtb_ref[...])
    pos = lax.broadcasted_iota(jnp.int32, g.shape, 0) & (CHUNK - 1)
    sh = 1
    while sh < CHUNK:
        g = g + jnp.where(pos >= sh, pltpu.roll(g, sh, axis=0), 0.0)
        sh *= 2
    gc_ref[...] = g


def _gates(gate, alog, dtb, l, batch, seq):
    vec = pl.BlockSpec((None, 1, LANES), lambda b: (l, 0, 0))
    return pl.pallas_call(
        _gates_body,
        grid=(batch,),
        in_specs=[pl.BlockSpec((seq, GATE_COLS), lambda b: (b, 0)), vec, vec],
        out_specs=[pl.BlockSpec((seq, LANES), lambda b: (b, 0))] * 2,
        out_shape=[jax.ShapeDtypeStruct((batch * seq, LANES), F32)] * 2,
        compiler_params=_params(("parallel",), 32),
        name="gates",
    )(gate, alog, dtb)


def _gdn_prep_body(q_ref, k_ref, v_ref, qh_ref, kh_ref, vh_ref, cwq_ref, cwk_ref, cwv_ref, beta_ref, gc_ref,
                   u_ref, wq_ref, kdt_ref, qk_ref, gl_ref, qs, ks, vs):
    head = pl.program_id(1)
    r = pl.program_id(2)
    rp = q_ref.shape[0]
    rb = min(CONV_ROWS, rp)

    def conv_silu(x_ref, halo_ref, cw_ref, i):
        cw = cw_ref[...]
        if i == 0:
            halo = jnp.where(r > 0, halo_ref[...], 0.0)
            ext = jnp.concatenate([halo, x_ref[0:rb, :]], axis=0)
            off = SUBLANES - (CONV_W - 1)
            taps = [ext[off + t:off + t + rb] for t in range(CONV_W)]
        else:
            off = i * rb - (CONV_W - 1)
            taps = [x_ref[off + t:off + t + rb, :] for t in range(CONV_W)]
        y = taps[0] * cw[0:1]
        for t in range(1, CONV_W):
            y = y + taps[t] * cw[t:t + 1]
        return _silu(y)

    def normalise(i):
        rows = slice(i * rb, (i + 1) * rb)
        qs[rows, :] = _l2(conv_silu(q_ref, qh_ref, cwq_ref, i)) * (HEAD_DIM ** -0.5)
        ks[rows, :] = _l2(conv_silu(k_ref, kh_ref, cwk_ref, i))
        vs[rows, :] = conv_silu(v_ref, vh_ref, cwv_ref, i)

    head_lane = lax.broadcasted_iota(jnp.int32, (CHUNK, LANES), 1) == head
    ri = lax.broadcasted_iota(jnp.int32, (CHUNK, CHUNK), 0)
    ci = lax.broadcasted_iota(jnp.int32, (CHUNK, CHUNK), 1)
    tril = ri >= ci
    strict = ri > ci
    half = CHUNK // 2
    same_half = (ri >= half) == (ci >= half)
    lower_left = (ri >= half) & (ci < half)
    eye = (ri == ci).astype(F32)
    nt_dims = (((1,), (1,)), ((), ()))

    def lane_bcast(x):
        return jnp.broadcast_to(jnp.sum(jnp.where(head_lane, x, 0.0), axis=1, keepdims=True), (CHUNK, LANES))

    def group(js):
        rows = [slice(j * CHUNK, (j + 1) * CHUNK) for j in js]
        q = [qs[r, :] for r in rows]
        k = [ks[r, :] for r in rows]
        bb = [lane_bcast(beta_ref[r, :]) for r in rows]
        gc = [lane_bcast(gc_ref[r, :]) for r in rows]
        decay = [jnp.where(tril, jnp.exp(jnp.where(tril, x - x.T, 0.0)), 0.0) for x in gc]
        kb = [a * b for a, b in zip(k, bb)]
        a1 = [lax.dot_general(jnp.concatenate([a, b], axis=0), x, nt_dims, preferred_element_type=F32)
              for a, b, x in zip(q, kb, k)]
        for r, a, d in zip(rows, a1, decay):
            qk_ref[r, :] = (a[:CHUNK] * d).astype(BF16)
        kbk = [a[CHUNK:] * d for a, d in zip(a1, decay)]
        m_d = [jnp.where(strict & same_half, x, 0.0) for x in kbk]
        corner = [jnp.where(lower_left, x, 0.0) for x in kbk]
        inv = [eye - x for x in m_d]
        mp = m_d
        sh = 1
        while 4 * sh < CHUNK:
            sh *= 2
            mp = [_dot(x, x) for x in mp]
            inv = [a + _dot(a, x) for a, x in zip(inv, mp)]
        dc = [_dot(a, x) for a, x in zip(inv, corner)]
        inv = [a - _dot(x, a) for a, x in zip(inv, dc)]
        eg = [jnp.exp(x) for x in gc]
        uw = [_dot(a, jnp.concatenate([vs[r, :] * b, x * e], axis=1))
              for a, r, b, x, e in zip(inv, rows, bb, kb, eg)]
        for j, r, x, a, e, kj, gcx in zip(js, rows, uw, q, eg, k, gc):
            u_ref[r, :] = x[:, :HEAD_DIM]
            wq_ref[2 * j * CHUNK:2 * (j + 1) * CHUNK, :] = jnp.concatenate(
                [x[:, HEAD_DIM:], a * e], axis=0).astype(BF16)
            glast = gcx[CHUNK - 1:CHUNK, :]
            kdt_ref[:, r] = (kj * jnp.exp(glast - gcx)).T.astype(BF16)
            gl_ref[j * SUBLANES:(j + 1) * SUBLANES, :] = jnp.broadcast_to(jnp.exp(glast), (SUBLANES, LANES))

    n_chunks = rp // CHUNK
    unroll = min(PREP_UNROLL, n_chunks)
    blocks_per_part = max(unroll * CHUNK // rb, 1)
    for part in range(n_chunks // unroll):
        for i in range(part * blocks_per_part, min((part + 1) * blocks_per_part, rp // rb)):
            normalise(i)
        group([part * unroll + t for t in range(unroll)])


def _gdn_prep(proj, beta, gc, conv_w, l, batch, seq):
    nh = DN_HEADS
    rp = min(PREP_ROWS, seq)
    nr = seq // rp
    n_chunks = seq // CHUNK
    col = lambda base: (lambda b, h, r: (b * nr + r, base + h))
    halo = lambda base: (lambda b, h, r: (jnp.maximum((b * nr + r) * (rp // SUBLANES) - 1, 0), base + h))
    cwcol = lambda base: (lambda b, h, r: (l, 0, base + h))
    gate_spec = pl.BlockSpec((rp, LANES), lambda b, h, r: (b * nr + r, 0))
    per_head = lambda rows, cols: pl.BlockSpec((None, None, rows, cols), lambda b, h, r: (b, h, r, 0))
    return pl.pallas_call(
        _gdn_prep_body,
        grid=(batch, nh, nr),
        in_specs=[
            pl.BlockSpec((rp, HEAD_DIM), col(0)),
            pl.BlockSpec((rp, HEAD_DIM), col(nh)),
            pl.BlockSpec((rp, HEAD_DIM), col(2 * nh)),
            pl.BlockSpec((SUBLANES, HEAD_DIM), halo(0)),
            pl.BlockSpec((SUBLANES, HEAD_DIM), halo(nh)),
            pl.BlockSpec((SUBLANES, HEAD_DIM), halo(2 * nh)),
            pl.BlockSpec((None, CONV_W, HEAD_DIM), cwcol(0)),
            pl.BlockSpec((None, CONV_W, HEAD_DIM), cwcol(nh)),
            pl.BlockSpec((None, CONV_W, HEAD_DIM), cwcol(2 * nh)),
            gate_spec, gate_spec,
        ],
        out_specs=[
            per_head(rp, HEAD_DIM),
            per_head(2 * rp, HEAD_DIM),
            pl.BlockSpec((None, None, HEAD_DIM, rp), lambda b, h, r: (b, h, 0, r)),
            per_head(rp, HEAD_DIM),
            per_head((rp // CHUNK) * SUBLANES, LANES),
        ],
        out_shape=[
            jax.ShapeDtypeStruct((batch, nh, seq, HEAD_DIM), F32),
            jax.ShapeDtypeStruct((batch, nh, 2 * seq, HEAD_DIM), BF16),
            jax.ShapeDtypeStruct((batch, nh, HEAD_DIM, seq), BF16),
            jax.ShapeDtypeStruct((batch, nh, seq, CHUNK), BF16),
            jax.ShapeDtypeStruct((batch, nh, n_chunks * SUBLANES, LANES), F32),
        ],
        scratch_shapes=[pltpu.VMEM((rp, HEAD_DIM), F32)] * 3,
        compiler_params=_params(("parallel", "parallel", "parallel"), 32),
        name="gdn_prep",
    )(proj, proj, proj, proj, proj, proj, conv_w, conv_w, conv_w, beta, gc)


def _gdn_scan_body(u_ref, wq_ref, kdt_ref, qk_ref, gl_ref, z_ref, dnw_ref, o_ref, s_out_ref, st_s):
    hs, rows_per_step = u_ref.shape[0], u_ref.shape[1]
    step = pl.program_id(2)

    @pl.when(step == 0)
    def _():
        st_s[...] = jnp.zeros(st_s.shape, F32)

    def chunk(c, carry):
        rows = pl.ds(pl.multiple_of(c * CHUNK, CHUNK), CHUNK)
        wrows = pl.ds(pl.multiple_of(c * 2 * CHUNK, 2 * CHUNK), 2 * CHUNK)
        grows = pl.ds(pl.multiple_of(c * SUBLANES, SUBLANES), SUBLANES)
        heads = range(hs)
        st = [st_s[hh] for hh in heads]
        r = [_dot(wq_ref[hh, wrows, :], st[hh].astype(BF16)) for hh in heads]
        vnew = [(u_ref[hh, rows, :] - r[hh][:CHUNK]).astype(BF16) for hh in heads]
        for hh in heads:
            st_s[hh] = st[hh] * gl_ref[hh, grows, :][0:1, :] + _dot(kdt_ref[hh, :, rows], vnew[hh])
        o = [r[hh][CHUNK:] + _dot(qk_ref[hh, rows, :], vnew[hh]) for hh in heads]
        for hh in heads:
            lo = hh * HEAD_DIM
            o_ref[rows, lo:lo + HEAD_DIM] = (
                _rms(o[hh], dnw_ref[...]) * _silu(z_ref[rows, lo:lo + HEAD_DIM])).astype(BF16)
        return carry

    lax.fori_loop(0, rows_per_step // CHUNK, chunk, 0)

    @pl.when(step == pl.num_programs(2) - 1)
    def _():
        s_out_ref[...] = st_s[...]


def _gdn_scan(u, wq, kdt, qk, gl, proj, dnw, l, batch, seq):
    hs = SCAN_HEADS
    rs = min(SCAN_ROWS, seq)
    nr = seq // rs
    zcol = (3 * DN_WIDTH) // (hs * HEAD_DIM)
    per_group = lambda rows, cols: pl.BlockSpec((None, hs, rows, cols), lambda b, g, r: (b, g, r, 0))
    return pl.pallas_call(
        _gdn_scan_body,
        grid=(batch, DN_HEADS // hs, nr),
        in_specs=[
            per_group(rs, HEAD_DIM),
            per_group(2 * rs, HEAD_DIM),
            pl.BlockSpec((None, hs, HEAD_DIM, rs), lambda b, g, r: (b, g, 0, r)),
            per_group(rs, CHUNK),
            per_group((rs // CHUNK) * SUBLANES, LANES),
            pl.BlockSpec((rs, hs * HEAD_DIM), lambda b, g, r: (b * nr + r, zcol + g)),
            pl.BlockSpec((None, 1, HEAD_DIM), lambda b, g, r: (l, 0, 0)),
        ],
        out_specs=[
            pl.BlockSpec((rs, hs * HEAD_DIM), lambda b, g, r: (b * nr + r, g)),
            pl.BlockSpec((None, hs, HEAD_DIM, HEAD_DIM), lambda b, g, r: (b, g, 0, 0)),
        ],
        out_shape=[
            jax.ShapeDtypeStruct((batch * seq, DN_WIDTH), BF16),
            jax.ShapeDtypeStruct((batch, DN_HEADS, HEAD_DIM, HEAD_DIM), F32),
        ],
        scratch_shapes=[pltpu.VMEM((hs, HEAD_DIM, HEAD_DIM), F32)],
        compiler_params=_params(("parallel", "parallel", "arbitrary"), 40),
        name="gdn_scan",
    )(u, wq, kdt, qk, gl, proj, dnw)


def _gdn_sample_body(qkv_ref, z_ref, gate_ref, cs_ref, s_ref, cw_ref, alog_ref, dtb_ref, dnw_ref, *rest):
    dn_ref, sn_ref, cn_ref, q_s, k_s, v_s, eg_s, beta_s, o_s = rest[-9:]
    nb = qkv_ref.shape[0]
    x = qkv_ref[...]
    c0, c1, c2 = cs_ref[0], cs_ref[1], cs_ref[2]
    cw = cw_ref[...]
    y = c0 * cw[0:1] + c1 * cw[1:2] + c2 * cw[2:3] + x * cw[3:4]
    y = _silu(y)
    cn_ref[0] = c1
    cn_ref[1] = c2
    cn_ref[2] = x
    for h in range(DN_HEADS):
        lo = h * HEAD_DIM
        q_s[h] = _l2(y[:, lo:lo + HEAD_DIM]) * (HEAD_DIM ** -0.5)
        k_s[h] = _l2(y[:, DN_WIDTH + lo:DN_WIDTH + lo + HEAD_DIM])
        v_s[h] = y[:, 2 * DN_WIDTH + lo:2 * DN_WIDTH + lo + HEAD_DIM]
    beta_s[...] = jax.nn.sigmoid(gate_ref[:, 0:LANES])
    eg_s[...] = jnp.exp(-jnp.exp(alog_ref[...]) * _softplus(gate_ref[:, LANES:2 * LANES] + dtb_ref[...]))

    pad7 = jnp.zeros((SUBLANES - 1, HEAD_DIM), F32)
    pad6 = jnp.zeros((SUBLANES - 2, HEAD_DIM), F32)

    tn_dims = (((0,), (0,)), ((), ()))

    def per_samples(i, carry):
        units = [(i * SAMPLE_UNROLL + t, h) for t in range(SAMPLE_UNROLL) for h in range(DN_HEADS)]
        k = [k_s[h, pl.ds(b, 1), :] for b, h in units]
        q = [q_s[h, pl.ds(b, 1), :] for b, h in units]
        st = [s_ref[b, h] * eg_s[pl.ds(b, 1), :][:, h:h + 1] for b, h in units]
        r = [_dot(jnp.concatenate([kk, qq, pad6], axis=0), s) for kk, qq, s in zip(k, q, st)]
        dv = [(v_s[h, pl.ds(b, 1), :] - x[0:1]) * beta_s[pl.ds(b, 1), :][:, h:h + 1]
              for (b, h), x in zip(units, r)]
        outer = [lax.dot_general(jnp.concatenate([kk, pad7], axis=0), jnp.concatenate([d, pad7], axis=0),
                                 tn_dims, preferred_element_type=F32) for kk, d in zip(k, dv)]
        for (b, h), s, x, o, kk, qq, d in zip(units, st, outer, r, k, q, dv):
            sn_ref[b, h] = s + x
            o_s[h, pl.ds(b, 1), :] = o[1:2] + jnp.sum(qq * kk, axis=-1, keepdims=True) * d
        return carry

    lax.fori_loop(0, nb // SAMPLE_UNROLL, per_samples, 0)

    z = z_ref[...]
    for h in range(DN_HEADS):
        lo = h * HEAD_DIM
        dn_ref[:, lo:lo + HEAD_DIM] = (_rms(o_s[h], dnw_ref[...]) * _silu(z[:, lo:lo + HEAD_DIM])).astype(BF16)


def _gdn_sample(proj, gate, state_conv, state_delta, conv_w, alog, dtb, dnw, prev, l, row0):
    depth, nseq = state_delta.shape[:2]
    nb = SAMPLE_BLOCK
    base = row0 // nb
    vec = pl.BlockSpec((None, 1, LANES), lambda i: (l, 0, 0))
    delta_spec = pl.BlockSpec((None, nb, DN_HEADS, HEAD_DIM, HEAD_DIM), lambda i: (l, i, 0, 0, 0))
    conv_spec = pl.BlockSpec((None, CONV_W - 1, nb, 3 * DN_WIDTH), lambda i: (l, 0, i, 0))
    inputs = [proj, proj, gate, state_conv, state_delta, conv_w, alog, dtb, dnw]
    in_specs = [
        pl.BlockSpec((nb, 3 * DN_WIDTH), lambda i: (base + i, 0)),
        pl.BlockSpec((nb, DN_WIDTH), lambda i: (base + i, 3)),
        pl.BlockSpec((nb, GATE_COLS), lambda i: (base + i, 0)),
        conv_spec, delta_spec,
        pl.BlockSpec((None, CONV_W, 3 * DN_WIDTH), lambda i: (l, 0, 0)),
        vec, vec, vec,
    ]
    aliases = {}
    if prev is not None:
        inputs += list(prev)
        in_specs += [_any_spec(), _any_spec()]
        aliases = {9: 1, 10: 2}
    return pl.pallas_call(
        _gdn_sample_body,
        grid=(nseq // nb,),
        in_specs=in_specs,
        out_specs=[pl.BlockSpec((nb, DN_WIDTH), lambda i: (i, 0)), delta_spec, conv_spec],
        out_shape=[
            jax.ShapeDtypeStruct((nseq, DN_WIDTH), BF16),
            jax.ShapeDtypeStruct(state_delta.shape, F32),
            jax.ShapeDtypeStruct(state_conv.shape, F32),
        ],
        scratch_shapes=[
            pltpu.VMEM((DN_HEADS, nb, HEAD_DIM), F32),
            pltpu.VMEM((DN_HEADS, nb, HEAD_DIM), F32),
            pltpu.VMEM((DN_HEADS, nb, HEAD_DIM), F32),
            pltpu.VMEM((nb, LANES), F32),
            pltpu.VMEM((nb, LANES), F32),
            pltpu.VMEM((DN_HEADS, nb, HEAD_DIM), F32),
        ],
        input_output_aliases=aliases,
        compiler_params=_params(("parallel",), 48),
        name="gdn_sample",
    )(*inputs)


def _pool_matmul(d, w_ref, sc_ref, o_ref, gi):
    lo = gi * POOL_GROUP
    y = _dot(d.astype(BF16), w_ref[gi]) * sc_ref[:, lo:lo + POOL_GROUP]
    o_ref[:, lo:lo + POOL_GROUP] = y.astype(BF16)


def _pool_prompt_body(p_ref, prev_ref, w_ref, sc_ref, o_ref, *, tiles_per_seq):
    rt = p_ref.shape[0]
    hist = prev_ref.shape[0]
    t = pl.program_id(0) % tiles_per_seq
    cur = p_ref[...]
    prev = jnp.where(t == 0, 0.0, prev_ref[...])
    pos = t * rt + lax.broadcasted_iota(jnp.int32, (rt, POOL_GROUP), 0)
    for gi, win in enumerate(POOL_WINDOWS):
        lo = gi * POOL_GROUP
        x = cur[:, lo:lo + POOL_GROUP]
        s = jnp.concatenate([prev[:, lo:lo + POOL_GROUP], x], axis=0)
        sh = 1
        while sh < win:
            s = s + pltpu.roll(s, sh, axis=0)
            sh *= 2
        cnt = jnp.minimum(pos + 1, win).astype(F32)
        _pool_matmul(s[hist:] / cnt - x, w_ref, sc_ref, o_ref, gi)


def _pool_prompt(proj, w_pool, scale, l, batch, seq):
    rt = min(POOL_ROWS, seq)
    hist = POOL_BUF + 1
    tiles_per_seq = seq // rt
    pcol = (4 * DN_WIDTH) // POOL_WIDTH
    ng = len(POOL_WINDOWS)
    return pl.pallas_call(
        functools.partial(_pool_prompt_body, tiles_per_seq=tiles_per_seq),
        grid=(batch * tiles_per_seq,),
        in_specs=[
            pl.BlockSpec((rt, POOL_WIDTH), lambda r: (r, pcol)),
            pl.BlockSpec((hist, POOL_WIDTH), lambda r: (jnp.maximum(r * (rt // hist) - 1, 0), pcol)),
            pl.BlockSpec((None, ng, POOL_GROUP, POOL_GROUP), lambda r: (l, 0, 0, 0)),
            pl.BlockSpec((None, 1, POOL_WIDTH), lambda r: (l, 0, 0)),
        ],
        out_specs=pl.BlockSpec((rt, POOL_WIDTH), lambda r: (r, 0)),
        out_shape=jax.ShapeDtypeStruct((batch * seq, POOL_WIDTH), BF16),
        compiler_params=_params(("parallel",), 32),
        name="pool_prompt",
    )(proj, proj, w_pool, scale)


def _pool_sample_body(p_ref, buf_ref, w_ref, sc_ref, *rest):
    o_ref, nb_ref = rest[-2:]
    cur = p_ref[...]
    for gi, win in enumerate(POOL_WINDOWS):
        lo = gi * POOL_GROUP
        x = cur[:, lo:lo + POOL_GROUP]
        s = x
        for j in range(1, win):
            s = s + buf_ref[POOL_BUF - j, :, lo:lo + POOL_GROUP]
        cnt = float(min(PAST_LEN + 1, win))
        _pool_matmul(s / cnt - x, w_ref, sc_ref, o_ref, gi)
    for j in range(POOL_BUF - 1):
        nb_ref[j] = buf_ref[j + 1]
    nb_ref[POOL_BUF - 1] = cur


def _pool_sample(proj, state_pool, w_pool, scale, prev, l, row0):
    nseq = state_pool.shape[2]
    nb = POOL_SAMPLE_BLOCK
    base = row0 // nb
    pcol = (4 * DN_WIDTH) // POOL_WIDTH
    ng = len(POOL_WINDOWS)
    state_spec = pl.BlockSpec((None, POOL_BUF, nb, POOL_WIDTH), lambda i: (l, 0, i, 0))
    inputs = [proj, state_pool, w_pool, scale]
    in_specs = [
        pl.BlockSpec((nb, POOL_WIDTH), lambda i: (base + i, pcol)),
        state_spec,
        pl.BlockSpec((None, ng, POOL_GROUP, POOL_GROUP), lambda i: (l, 0, 0, 0)),
        pl.BlockSpec((None, 1, POOL_WIDTH), lambda i: (l, 0, 0)),
    ]
    aliases = {}
    if prev is not None:
        inputs.append(prev)
        in_specs.append(_any_spec())
        aliases = {4: 1}
    return pl.pallas_call(
        _pool_sample_body,
        grid=(nseq // nb,),
        in_specs=in_specs,
        out_specs=[pl.BlockSpec((nb, POOL_WIDTH), lambda i: (i, 0)), state_spec],
        out_shape=[
            jax.ShapeDtypeStruct((nseq, POOL_WIDTH), BF16),
            jax.ShapeDtypeStruct(state_pool.shape, F32),
        ],
        input_output_aliases=aliases,
        compiler_params=_params(("parallel",), 32),
        name="pool_sample",
    )(*inputs)


def _outproj_body(dn_ref, pool_ref, dns_ref, pools_ref, w1_ref, w2_ref, h_ref, *rest, sample_row0):
    hs_ref = rest[0] if len(rest) == 2 else None
    o_ref = rest[-1]
    is_last = pl.program_id(0) == pl.num_programs(0) - 1

    def project(dn, pool, h):
        o_ref[...] = h + _dot(dn, w1_ref[...]) + _dot(pool, w2_ref[...])

    @pl.when(jnp.logical_not(is_last))
    def _():
        project(dn_ref[...], pool_ref[...], h_ref[...])

    @pl.when(is_last)
    def _():
        h = h_ref[...] if hs_ref is None else jnp.concatenate([h_ref[:sample_row0, :], hs_ref[...]], axis=0)
        project(jnp.concatenate([dn_ref[:sample_row0, :], dns_ref[...]], axis=0),
                jnp.concatenate([pool_ref[:sample_row0, :], pools_ref[...]], axis=0), h)


def _outproj(dn_p, dn_s, pool_p, pool_s, w_out, h, h_tail, l):
    rows_p, nseq = dn_p.shape[0], dn_s.shape[0]
    m = rows_p + nseq
    tm, tn = _row_tile(m, OUTPROJ_ROW_TILE), OUTPROJ_COL_TILE
    sample_row0 = _tail_row0(rows_p, nseq, tm)
    tail = [] if h_tail is None else [h_tail]
    tail_specs = [] if h_tail is None else [pl.BlockSpec((nseq, tn), lambda i, j: (0, j))]
    return pl.pallas_call(
        functools.partial(_outproj_body, sample_row0=sample_row0),
        grid=(m // tm, D_MODEL // tn),
        in_specs=[
            pl.BlockSpec((tm, DN_WIDTH), lambda i, j: (i, 0)),
            pl.BlockSpec((tm, POOL_WIDTH), lambda i, j: (i, 0)),
            pl.BlockSpec((nseq, DN_WIDTH), lambda i, j: (0, 0)),
            pl.BlockSpec((nseq, POOL_WIDTH), lambda i, j: (0, 0)),
            pl.BlockSpec((None, DN_WIDTH, tn), lambda i, j: (l, 0, j)),
            pl.BlockSpec((None, POOL_WIDTH, tn), lambda i, j: (l, 1, j)),
            pl.BlockSpec((tm, tn), lambda i, j: (i, j)),
        ] + tail_specs,
        out_specs=pl.BlockSpec((tm, tn), lambda i, j: (i, j)),
        out_shape=jax.ShapeDtypeStruct((m, D_MODEL), F32),
        compiler_params=_params(("parallel", "parallel"), 54),
        name="outproj",
    )(dn_p, pool_p, dn_s, pool_s, w_out, w_out, h, *tail)


def _ffn_body(h_ref, nw_ref, wg_ref, wu_ref, wd_ref, nf_ref, *rest, final_norm, sample_row0, n_cast):
    cast_src, rest = rest[:n_cast], rest[n_cast:]
    o_ref, xn_ref = rest[0], rest[-1]
    cast_dst = rest[len(rest) - 1 - n_cast:len(rest) - 1]
    f = pl.program_id(1)
    last = pl.num_programs(1) - 1

    @pl.when(f == 0)
    def _():
        x = h_ref[...]
        xn_ref[...] = _rms(x, nw_ref[...]).astype(BF16)
        o_ref[...] = x

    xn = xn_ref[...]
    act = (_silu(_dot(xn, wg_ref[...])) * _dot(xn, wu_ref[...])).astype(BF16)
    o_ref[...] += _dot(act, wd_ref[...])
    for src, dst in zip(cast_src, cast_dst):
        dst[...] = src[...].astype(BF16)

    if final_norm:
        @pl.when(f == last)
        def _():
            o_ref[...] = _rms(o_ref[...], nf_ref[...])

    if sample_row0 is not None:
        @pl.when((f == last) & (pl.program_id(0) == pl.num_programs(0) - 1))
        def _():
            rest[1][...] = o_ref[sample_row0:, :]


def _ffn_cast_plan(m):
    steps = (m // _row_tile(m, FFN_ROW_TILE)) * (D_FF // FFN_TILE)
    n_units = 2 * D_FF // LANES
    for per_step in range(1, n_units + 1):
        if n_units % per_step == 0 and per_step * steps >= n_units:
            return per_step, n_units // per_step
    return None


def _ffn(h, nw, w_gate_up, w_down, nf, l, final_norm, split_rows=None, next_weights=None):
    m = h.shape[0]
    tm, tf = _row_tile(m, FFN_ROW_TILE), FFN_TILE
    nft = D_FF // tf
    out_spec = pl.BlockSpec((tm, D_MODEL), lambda i, f: (i, 0))
    if split_rows is None:
        out_specs, out_shape, sample_row0 = [out_spec], [jax.ShapeDtypeStruct((m, D_MODEL), F32)], None
    else:
        sample_row0 = split_rows - (m // tm - 1) * tm
        assert 0 < sample_row0 < tm, (m, tm, split_rows)
        out_specs = [out_spec, pl.BlockSpec((m - split_rows, D_MODEL), lambda i, f: (0, 0))]
        out_shape = [jax.ShapeDtypeStruct((split_rows, D_MODEL), F32),
                     jax.ShapeDtypeStruct((m - split_rows, D_MODEL), F32)]
    inputs = [h, nw, w_gate_up, w_gate_up, w_down, nf]
    in_specs = [
        pl.BlockSpec((tm, D_MODEL), lambda i, f: (i, 0)),
        pl.BlockSpec((None, 1, D_MODEL), lambda i, f: (l, 0, 0)),
        pl.BlockSpec((D_MODEL, tf), lambda i, f: (0, f)),
        pl.BlockSpec((D_MODEL, tf), lambda i, f: (0, nft + f)),
        pl.BlockSpec((tf, D_MODEL), lambda i, f: (f, 0)),
        pl.BlockSpec((1, D_MODEL), lambda i, f: (0, 0)),
    ]
    n_cast = 0
    if next_weights is not None:
        per_step, n_blocks = _ffn_cast_plan(m)
        cols, rows = per_step * LANES, per_step * (D_FF // (2 * D_FF // LANES))
        blk = lambda i, f: jnp.minimum(i * nft + f, n_blocks - 1)
        n_cast = 2
        inputs += list(next_weights)
        in_specs += [pl.BlockSpec((None, D_MODEL, cols), lambda i, f: (l + 1, 0, blk(i, f))),
                     pl.BlockSpec((None, rows, D_MODEL), lambda i, f: (l + 1, blk(i, f), 0))]
        out_specs += [pl.BlockSpec((D_MODEL, cols), lambda i, f: (0, blk(i, f))),
                      pl.BlockSpec((rows, D_MODEL), lambda i, f: (blk(i, f), 0))]
        out_shape += [jax.ShapeDtypeStruct((D_MODEL, 2 * D_FF), BF16), jax.ShapeDtypeStruct((D_FF, D_MODEL), BF16)]
    out = pl.pallas_call(
        functools.partial(_ffn_body, final_norm=final_norm, sample_row0=sample_row0, n_cast=n_cast),
        grid=(m // tm, nft),
        in_specs=in_specs,
        out_specs=out_specs,
        out_shape=out_shape,
        scratch_shapes=[pltpu.VMEM((tm, D_MODEL), BF16)],
        compiler_params=_params(("parallel", "arbitrary"), 48),
        name="ffn",
    )(*inputs)
    return out[0] if len(out) == 1 else out


def _lane_vecs(x):
    depth, n = x.shape
    return jnp.zeros((depth, 1, LANES), F32).at[:, 0, :n].set(x)


def kernel(x_prompt, x_sample, state_delta, state_conv, state_pool, norm_mix, w_in, conv_w, a_log, dt_bias,
           dn_norm, w_pool, pool_scale, w_out, norm_ffn, w_gate_up, w_down, norm_final):
    batch, seq, _ = x_prompt.shape
    nseq = x_sample.shape[0]
    depth = w_in.shape[0]
    rows_p = batch * seq
    h, h_tail = x_prompt.reshape(rows_p, D_MODEL), x_sample.reshape(nseq, D_MODEL)

    o1 = 4 * DN_WIDTH
    w_a = w_in.astype(BF16)
    w_b = w_a[:, :, o1 + 2 * DN_HEADS:]
    lane_pad = ((0, 0), (0, 0), (0, LANES - DN_HEADS))
    w_gate = jnp.concatenate([jnp.pad(w_a[:, :, o1:o1 + DN_HEADS], lane_pad),
                              jnp.pad(w_a[:, :, o1 + DN_HEADS:o1 + 2 * DN_HEADS], lane_pad)], axis=-1)
    w_out_b = w_out.astype(BF16)
    w_pool_b = w_pool.astype(BF16)
    alog, dtb = _lane_vecs(a_log), _lane_vecs(dt_bias)
    dnw = dn_norm.reshape(depth, 1, HEAD_DIM)
    scale = pool_scale.reshape(depth, 1, POOL_WIDTH)
    nmix = norm_mix.reshape(depth, 1, D_MODEL)
    nffn = norm_ffn.reshape(depth, 1, D_MODEL)
    nfin = norm_final.reshape(1, D_MODEL)
    conv_state = jnp.transpose(state_conv, (0, 2, 1, 3))
    pool_state = jnp.transpose(state_pool, (0, 2, 1, 3))
    hide_casts = _ffn_cast_plan(rows_p + nseq) is not None
    w_gu_l, w_down_l = w_gate_up[0].astype(BF16), w_down[0].astype(BF16)

    delta_p, conv_p, poolbuf_p = [], [], []
    sample_states = None
    pool_states = None
    for l in range(depth):
        proj, gate = _inproj(h, h_tail, nmix, w_a, w_b, w_gate, l)

        beta, gc = _gates(gate, alog, dtb, l, batch, seq)
        u, wq, kdt, qk, gl = _gdn_prep(proj, beta, gc, conv_w, l, batch, seq)
        dn_p, s_p = _gdn_scan(u, wq, kdt, qk, gl, proj, dnw, l, batch, seq)
        dn_s, *sample_states = _gdn_sample(proj, gate, conv_state, state_delta, conv_w, alog, dtb, dnw,
                                           sample_states, l, rows_p)
        pool_p = _pool_prompt(proj, w_pool_b, scale, l, batch, seq)
        pool_s, pool_states = _pool_sample(proj, pool_state, w_pool_b, scale, pool_states, l, rows_p)

        delta_p.append(s_p)
        conv_p.append(jnp.stack([proj[(b + 1) * seq - (CONV_W - 1):(b + 1) * seq, :3 * DN_WIDTH]
                                 for b in range(batch)]))
        poolbuf_p.append(jnp.stack([proj[(b + 1) * seq - POOL_BUF:(b + 1) * seq, o1:] for b in range(batch)]))

        h, h_tail = _outproj(dn_p, dn_s, pool_p, pool_s, w_out_b, h, h_tail, l), None
        if l < depth - 1:
            if hide_casts:
                h, w_gu_l, w_down_l = _ffn(h, nffn, w_gu_l, w_down_l, nfin, l, False,
                                           next_weights=(w_gate_up, w_down))
            else:
                h = _ffn(h, nffn, w_gu_l, w_down_l, nfin, l, False)
                w_gu_l, w_down_l = w_gate_up[l + 1].astype(BF16), w_down[l + 1].astype(BF16)
        elif rows_p % _row_tile(h.shape[0], FFN_ROW_TILE) == 0:
            h = _ffn(h, nffn, w_gu_l, w_down_l, nfin, l, True)
            y_p, y_s = h[:rows_p], h[rows_p:]
        else:
            y_p, y_s = _ffn(h, nffn, w_gu_l, w_down_l, nfin, l, True, split_rows=rows_p)

    delta_s, conv_s = sample_states
    y_prompt = y_p.reshape(batch, seq, D_MODEL)
    y_sample = y_s.reshape(nseq, 1, D_MODEL)
    return (y_prompt, y_sample, jnp.stack(delta_p), jnp.stack(conv_p), jnp.stack(poolbuf_p),
            delta_s, jnp.transpose(conv_s, (0, 2, 1, 3)), jnp.transpose(pool_states, (0, 2, 1, 3)))
```

```python
import functools

import jax
import jax.numpy as jnp
from jax import lax
from jax.experimental import pallas as pl
from jax.experimental.pallas import tpu as pltpu

F32 = jnp.float32
BF16 = jnp.bfloat16

D_MODEL = 2048
DN_WIDTH = 1024
DN_HEADS = 8
HEAD_DIM = 128
CONV_W = 4
CHUNK = 128
POOL_WIDTH = 1024
POOL_WINDOWS = (2, 4, 8, 16)
POOL_GROUP = 256
POOL_BUF = 15
D_FF = 5632
EPS = 1e-6
PAST_LEN = 16384

MAIN_COLS = 4 * DN_WIDTH + POOL_WIDTH
GATE_COLS = 256
LANES = 128
SUBLANES = 8

ROW_TILES = (1040, 640, 512, 256, 128)
INPROJ_COL_TILE = 1024
OUTPROJ_ROW_TILE = 640
OUTPROJ_COL_TILE = 2048
FFN_ROW_TILE = 640
FFN_TILE = 512
CONV_ROWS = 256
PREP_ROWS = 2048
PREP_UNROLL = 16
SCAN_HEADS = 8
SCAN_ROWS = 512
POOL_ROWS = 512
SAMPLE_BLOCK = 16
SAMPLE_UNROLL = 4
POOL_SAMPLE_BLOCK = 32
MIB = 1024 * 1024


def _params(semantics, vmem_mib):
    return pltpu.CompilerParams(dimension_semantics=semantics, vmem_limit_bytes=vmem_mib * MIB)


def _silu(x):
    return x * jax.nn.sigmoid(x)


def _softplus(x):
    return jnp.maximum(x, 0.0) + jnp.log1p(jnp.exp(-jnp.abs(x)))


def _rms(x, w):
    return x * lax.rsqrt(jnp.mean(x * x, axis=-1, keepdims=True) + EPS) * w


def _l2(x):
    return x * lax.rsqrt(jnp.sum(x * x, axis=-1, keepdims=True) + EPS)


def _dot(a, b):
    return jnp.dot(a, b, preferred_element_type=F32)


def _row_tile(m, largest=ROW_TILES[0]):
    return next(t for t in ROW_TILES if t <= largest and m % t == 0)


def _any_spec():
    return pl.BlockSpec(memory_space=pl.ANY)


def _stacked_rows(head_ref, tail_ref, tail_row0, is_last, fn):
    if tail_ref is None:
        fn(head_ref[...])
        return

    @pl.when(jnp.logical_not(is_last))
    def _():
        fn(head_ref[...])

    @pl.when(is_last)
    def _():
        fn(jnp.concatenate([head_ref[:tail_row0, :], tail_ref[...]], axis=0))


def _tail_row0(rows_head, rows_tail, tm):
    row0 = rows_head - ((rows_head + rows_tail) // tm - 1) * tm
    assert 0 < row0 and row0 + rows_tail == tm, (rows_head, rows_tail, tm)
    return row0


def _inproj_body(h_ref, *rest, n_a, tail_row0):
    tail_ref = None if tail_row0 is None else rest[0]
    nw_ref, wa_ref, wb_ref, wg_ref, proj_ref, gate_ref, xn_ref = rest[-7:]
    j = pl.program_id(1)
    is_last = pl.program_id(0) == pl.num_programs(0) - 1

    def normalise(x):
        xn = _rms(x, nw_ref[...]).astype(BF16)
        xn_ref[...] = xn
        gate_ref[...] = _dot(xn, wg_ref[...])

    @pl.when(j == 0)
    def _():
        _stacked_rows(h_ref, tail_ref, tail_row0, is_last, normalise)

    @pl.when(j < n_a)
    def _():
        proj_ref[...] = _dot(xn_ref[...], wa_ref[...])

    @pl.when(j >= n_a)
    def _():
        proj_ref[...] = _dot(xn_ref[...], wb_ref[...])


def _inproj(h, h_tail, nw, w_a, w_b, w_gate, l):
    m = h.shape[0] + (0 if h_tail is None else h_tail.shape[0])
    tm, tn = _row_tile(m), INPROJ_COL_TILE
    n_a = (MAIN_COLS - w_b.shape[-1]) // tn
    tail_row0 = None if h_tail is None else _tail_row0(h.shape[0], h_tail.shape[0], tm)
    tail = [] if h_tail is None else [h_tail]
    tail_specs = [] if h_tail is None else [pl.BlockSpec(h_tail.shape, lambda i, j: (0, 0))]
    return pl.pallas_call(
        functools.partial(_inproj_body, n_a=n_a, tail_row0=tail_row0),
        grid=(m // tm, MAIN_COLS // tn),
        in_specs=[pl.BlockSpec((tm, D_MODEL), lambda i, j: (i, 0))] + tail_specs + [
            pl.BlockSpec((None, 1, D_MODEL), lambda i, j: (l, 0, 0)),
            pl.BlockSpec((None, D_MODEL, tn), lambda i, j: (l, 0, jnp.minimum(j, n_a - 1))),
            pl.BlockSpec((None, D_MODEL, tn), lambda i, j: (l, 0, jnp.maximum(j - n_a, 0))),
            pl.BlockSpec((None, D_MODEL, GATE_COLS), lambda i, j: (l, 0, 0)),
        ],
        out_specs=[
            pl.BlockSpec((tm, tn), lambda i, j: (i, j)),
            pl.BlockSpec((tm, GATE_COLS), lambda i, j: (i, 0)),
        ],
        out_shape=[
            jax.ShapeDtypeStruct((m, MAIN_COLS), F32),
            jax.ShapeDtypeStruct((m, GATE_COLS), F32),
        ],
        scratch_shapes=[pltpu.VMEM((tm, D_MODEL), BF16)],
        compiler_params=_params(("parallel", "arbitrary"), 54),
        name="inproj",
    )(h, *tail, nw, w_a, w_b, w_gate)


def _gates_body(gate_ref, alog_ref, dtb_ref, beta_ref, gc_ref):
    beta_ref[...] = jax.nn.sigmoid(gate_ref[:, 0:LANES])
    g = -jnp.exp(alog_ref[...]) * _softplus(gate_ref[:, LANES:2 * LANES] + dtb_ref[...])
    pos = lax.broadcasted_iota(jnp.int32, g.shape, 0) & (CHUNK - 1)
    sh = 1
    while sh < CHUNK:
        g = g + jnp.where(pos >= sh, pltpu.roll(g, sh, axis=0), 0.0)
        sh *= 2
    gc_ref[...] = g


def _gates(gate, alog, dtb, l, batch, seq):
    vec = pl.BlockSpec((None, 1, LANES), lambda b: (l, 0, 0))
    return pl.pallas_call(
        _gates_body,
        grid=(batch,),
        in_specs=[pl.BlockSpec((seq, GATE_COLS), lambda b: (b, 0)), vec, vec],
        out_specs=[pl.BlockSpec((seq, LANES), lambda b: (b, 0))] * 2,
        out_shape=[jax.ShapeDtypeStruct((batch * seq, LANES), F32)] * 2,
        compiler_params=_params(("parallel",), 32),
        name="gates",
    )(gate, alog, dtb)


def _gdn_prep_body(q_ref, k_ref, v_ref, qh_ref, kh_ref, vh_ref, cwq_ref, cwk_ref, cwv_ref, beta_ref, gc_ref,
                   u_ref, wq_ref, kdt_ref, qk_ref, gl_ref, qs, ks, vs):
    head = pl.program_id(1)
    r = pl.program_id(2)
    rp = q_ref.shape[0]
    rb = min(CONV_ROWS, rp)

    def conv_silu(x_ref, halo_ref, cw_ref, i):
        cw = cw_ref[...]
        if i == 0:
            halo = jnp.where(r > 0, halo_ref[...], 0.0)
            ext = jnp.concatenate([halo, x_ref[0:rb, :]], axis=0)
            off = SUBLANES - (CONV_W - 1)
            taps = [ext[off + t:off + t + rb] for t in range(CONV_W)]
        else:
            off = i * rb - (CONV_W - 1)
            taps = [x_ref[off + t:off + t + rb, :] for t in range(CONV_W)]
        y = taps[0] * cw[0:1]
        for t in range(1, CONV_W):
            y = y + taps[t] * cw[t:t + 1]
        return _silu(y)

    def normalise(i):
        rows = slice(i * rb, (i + 1) * rb)
        qs[rows, :] = _l2(conv_silu(q_ref, qh_ref, cwq_ref, i)) * (HEAD_DIM ** -0.5)
        ks[rows, :] = _l2(conv_silu(k_ref, kh_ref, cwk_ref, i))
        vs[rows, :] = conv_silu(v_ref, vh_ref, cwv_ref, i)

    head_lane = lax.broadcasted_iota(jnp.int32, (CHUNK, LANES), 1) == head
    ri = lax.broadcasted_iota(jnp.int32, (CHUNK, CHUNK), 0)
    ci = lax.broadcasted_iota(jnp.int32, (CHUNK, CHUNK), 1)
    tril = ri >= ci
    strict = ri > ci
    half = CHUNK // 2
    same_half = (ri >= half) == (ci >= half)
    lower_left = (ri >= half) & (ci < half)
    eye = (ri == ci).astype(F32)
    nt_dims = (((1,), (1,)), ((), ()))

    def lane_bcast(x):
        return jnp.broadcast_to(jnp.sum(jnp.where(head_lane, x, 0.0), axis=1, keepdims=True), (CHUNK, LANES))

    def group(js):
        rows = [slice(j * CHUNK, (j + 1) * CHUNK) for j in js]
        q = [qs[r, :] for r in rows]
        k = [ks[r, :] for r in rows]
        bb = [lane_bcast(beta_ref[r, :]) for r in rows]
        gc = [lane_bcast(gc_ref[r, :]) for r in rows]
        decay = [jnp.where(tril, jnp.exp(jnp.where(tril, x - x.T, 0.0)), 0.0) for x in gc]
        kb = [a * b for a, b in zip(k, bb)]
        a1 = [lax.dot_general(jnp.concatenate([a, b], axis=0), x, nt_dims, preferred_element_type=F32)
              for a, b, x in zip(q, kb, k)]
        for r, a, d in zip(rows, a1, decay):
            qk_ref[r, :] = (a[:CHUNK] * d).astype(BF16)
        kbk = [a[CHUNK:] * d for a, d in zip(a1, decay)]
        m_d = [jnp.where(strict & same_half, x, 0.0) for x in kbk]
        corner = [jnp.where(lower_left, x, 0.0) for x in kbk]
        inv = [eye - x for x in m_d]
        mp = m_d
        sh = 1
        while 4 * sh < CHUNK:
            sh *= 2
            mp = [_dot(x, x) for x in mp]
            inv = [a + _dot(a, x) for a, x in zip(inv, mp)]
        dc = [_dot(a, x) for a, x in zip(inv, corner)]
        inv = [a - _dot(x, a) for a, x in zip(inv, dc)]
        eg = [jnp.exp(x) for x in gc]
        uw = [_dot(a, jnp.concatenate([vs[r, :] * b, x * e], axis=1))
              for a, r, b, x, e in zip(inv, rows, bb, kb, eg)]
        for j, r, x, a, e, kj, gcx in zip(js, rows, uw, q, eg, k, gc):
            u_ref[r, :] = x[:, :HEAD_DIM]
            wq_ref[2 * j * CHUNK:2 * (j + 1) * CHUNK, :] = jnp.concatenate(
                [x[:, HEAD_DIM:], a * e], axis=0).astype(BF16)
            glast = gcx[CHUNK - 1:CHUNK, :]
            kdt_ref[:, r] = (kj * jnp.exp(glast - gcx)).T.astype(BF16)
            gl_ref[j * SUBLANES:(j + 1) * SUBLANES, :] = jnp.broadcast_to(jnp.exp(glast), (SUBLANES, LANES))

    n_chunks = rp // CHUNK
    unroll = min(PREP_UNROLL, n_chunks)
    blocks_per_part = max(unroll * CHUNK // rb, 1)
    for part in range(n_chunks // unroll):
        for i in range(part * blocks_per_part, min((part + 1) * blocks_per_part, rp // rb)):
            normalise(i)
        group([part * unroll + t for t in range(unroll)])


def _gdn_prep(proj, beta, gc, conv_w, l, batch, seq):
    nh = DN_HEADS
    rp = min(PREP_ROWS, seq)
    nr = seq // rp
    n_chunks = seq // CHUNK
    col = lambda base: (lambda b, h, r: (b * nr + r, base + h))
    halo = lambda base: (lambda b, h, r: (jnp.maximum((b * nr + r) * (rp // SUBLANES) - 1, 0), base + h))
    cwcol = lambda base: (lambda b, h, r: (l, 0, base + h))
    gate_spec = pl.BlockSpec((rp, LANES), lambda b, h, r: (b * nr + r, 0))
    per_head = lambda rows, cols: pl.BlockSpec((None, None, rows, cols), lambda b, h, r: (b, h, r, 0))
    return pl.pallas_call(
        _gdn_prep_body,
        grid=(batch, nh, nr),
        in_specs=[
            pl.BlockSpec((rp, HEAD_DIM), col(0)),
            pl.BlockSpec((rp, HEAD_DIM), col(nh)),
            pl.BlockSpec((rp, HEAD_DIM), col(2 * nh)),
            pl.BlockSpec((SUBLANES, HEAD_DIM), halo(0)),
            pl.BlockSpec((SUBLANES, HEAD_DIM), halo(nh)),
            pl.BlockSpec((SUBLANES, HEAD_DIM), halo(2 * nh)),
            pl.BlockSpec((None, CONV_W, HEAD_DIM), cwcol(0)),
            pl.BlockSpec((None, CONV_W, HEAD_DIM), cwcol(nh)),
            pl.BlockSpec((None, CONV_W, HEAD_DIM), cwcol(2 * nh)),
            gate_spec, gate_spec,
        ],
        out_specs=[
            per_head(rp, HEAD_DIM),
            per_head(2 * rp, HEAD_DIM),
            pl.BlockSpec((None, None, HEAD_DIM, rp), lambda b, h, r: (b, h, 0, r)),
            per_head(rp, HEAD_DIM),
            per_head((rp // CHUNK) * SUBLANES, LANES),
        ],
        out_shape=[
            jax.ShapeDtypeStruct((batch, nh, seq, HEAD_DIM), F32),
            jax.ShapeDtypeStruct((batch, nh, 2 * seq, HEAD_DIM), BF16),
            jax.ShapeDtypeStruct((batch, nh, HEAD_DIM, seq), BF16),
            jax.ShapeDtypeStruct((batch, nh, seq, CHUNK), BF16),
            jax.ShapeDtypeStruct((batch, nh, n_chunks * SUBLANES, LANES), F32),
        ],
        scratch_shapes=[pltpu.VMEM((rp, HEAD_DIM), F32)] * 3,
        compiler_params=_params(("parallel", "parallel", "parallel"), 32),
        name="gdn_prep",
    )(proj, proj, proj, proj, proj, proj, conv_w, conv_w, conv_w, beta, gc)


def _gdn_scan_body(u_ref, wq_ref, kdt_ref, qk_ref, gl_ref, z_ref, dnw_ref, o_ref, s_out_ref, st_s):
    hs, rows_per_step = u_ref.shape[0], u_ref.shape[1]
    step = pl.program_id(2)

    @pl.when(step == 0)
    def _():
        st_s[...] = jnp.zeros(st_s.shape, F32)

    def chunk(c, carry):
        rows = pl.ds(pl.multiple_of(c * CHUNK, CHUNK), CHUNK)
        wrows = pl.ds(pl.multiple_of(c * 2 * CHUNK, 2 * CHUNK), 2 * CHUNK)
        grows = pl.ds(pl.multiple_of(c * SUBLANES, SUBLANES), SUBLANES)
        heads = range(hs)
        st = [st_s[hh] for hh in heads]
        r = [_dot(wq_ref[hh, wrows, :], st[hh].astype(BF16)) for hh in heads]
        vnew = [(u_ref[hh, rows, :] - r[hh][:CHUNK]).astype(BF16) for hh in heads]
        for hh in heads:
            st_s[hh] = st[hh] * gl_ref[hh, grows, :][0:1, :] + _dot(kdt_ref[hh, :, rows], vnew[hh])
        o = [r[hh][CHUNK:] + _dot(qk_ref[hh, rows, :], vnew[hh]) for hh in heads]
        for hh in heads:
            lo = hh * HEAD_DIM
            o_ref[rows, lo:lo + HEAD_DIM] = (
                _rms(o[hh], dnw_ref[...]) * _silu(z_ref[rows, lo:lo + HEAD_DIM])).astype(BF16)
        return carry

    lax.fori_loop(0, rows_per_step // CHUNK, chunk, 0)

    @pl.when(step == pl.num_programs(2) - 1)
    def _():
        s_out_ref[...] = st_s[...]


def _gdn_scan(u, wq, kdt, qk, gl, proj, dnw, l, batch, seq):
    hs = SCAN_HEADS
    rs = min(SCAN_ROWS, seq)
    nr = seq // rs
    zcol = (3 * DN_WIDTH) // (hs * HEAD_DIM)
    per_group = lambda rows, cols: pl.BlockSpec((None, hs, rows, cols), lambda b, g, r: (b, g, r, 0))
    return pl.pallas_call(
        _gdn_scan_body,
        grid=(batch, DN_HEADS // hs, nr),
        in_specs=[
            per_group(rs, HEAD_DIM),
            per_group(2 * rs, HEAD_DIM),
            pl.BlockSpec((None, hs, HEAD_DIM, rs), lambda b, g, r: (b, g, 0, r)),
            per_group(rs, CHUNK),
            per_group((rs // CHUNK) * SUBLANES, LANES),
            pl.BlockSpec((rs, hs * HEAD_DIM), lambda b, g, r: (b * nr + r, zcol + g)),
            pl.BlockSpec((None, 1, HEAD_DIM), lambda b, g, r: (l, 0, 0)),
        ],
        out_specs=[
            pl.BlockSpec((rs, hs * HEAD_DIM), lambda b, g, r: (b * nr + r, g)),
            pl.BlockSpec((None, hs, HEAD_DIM, HEAD_DIM), lambda b, g, r: (b, g, 0, 0)),
        ],
        out_shape=[
            jax.ShapeDtypeStruct((batch * seq, DN_WIDTH), BF16),
            jax.ShapeDtypeStruct((batch, DN_HEADS, HEAD_DIM, HEAD_DIM), F32),
        ],
        scratch_shapes=[pltpu.VMEM((hs, HEAD_DIM, HEAD_DIM), F32)],
        compiler_params=_params(("parallel", "parallel", "arbitrary"), 40),
        name="gdn_scan",
    )(u, wq, kdt, qk, gl, proj, dnw)


def _gdn_sample_body(qkv_ref, z_ref, gate_ref, cs_ref, s_ref, cw_ref, alog_ref, dtb_ref, dnw_ref, *rest):
    dn_ref, sn_ref, cn_ref, q_s, k_s, v_s, eg_s, beta_s, o_s = rest[-9:]
    nb = qkv_ref.shape[0]
    x = qkv_ref[...]
    c0, c1, c2 = cs_ref[0], cs_ref[1], cs_ref[2]
    cw = cw_ref[...]
    y = c0 * cw[0:1] + c1 * cw[1:2] + c2 * cw[2:3] + x * cw[3:4]
    y = _silu(y)
    cn_ref[0] = c1
    cn_ref[1] = c2
    cn_ref[2] = x
    for h in range(DN_HEADS):
        lo = h * HEAD_DIM
        q_s[h] = _l2(y[:, lo:lo + HEAD_DIM]) * (HEAD_DIM ** -0.5)
        k_s[h] = _l2(y[:, DN_WIDTH + lo:DN_WIDTH + lo + HEAD_DIM])
        v_s[h] = y[:, 2 * DN_WIDTH + lo:2 * DN_WIDTH + lo + HEAD_DIM]
    beta_s[...] = jax.nn.sigmoid(gate_ref[:, 0:LANES])
    eg_s[...] = jnp.exp(-jnp.exp(alog_ref[...]) * _softplus(gate_ref[:, LANES:2 * LANES] + dtb_ref[...]))

    pad7 = jnp.zeros((SUBLANES - 1, HEAD_DIM), F32)
    pad6 = jnp.zeros((SUBLANES - 2, HEAD_DIM), F32)

    tn_dims = (((0,), (0,)), ((), ()))

    def per_samples(i, carry):
        units = [(i * SAMPLE_UNROLL + t, h) for t in range(SAMPLE_UNROLL) for h in range(DN_HEADS)]
        k = [k_s[h, pl.ds(b, 1), :] for b, h in units]
        q = [q_s[h, pl.ds(b, 1), :] for b, h in units]
        st = [s_ref[b, h] * eg_s[pl.ds(b, 1), :][:, h:h + 1] for b, h in units]
        r = [_dot(jnp.concatenate([kk, qq, pad6], axis=0), s) for kk, qq, s in zip(k, q, st)]
        dv = [(v_s[h, pl.ds(b, 1), :] - x[0:1]) * beta_s[pl.ds(b, 1), :][:, h:h + 1]
              for (b, h), x in zip(units, r)]
        outer = [lax.dot_general(jnp.concatenate([kk, pad7], axis=0), jnp.concatenate([d, pad7], axis=0),
                                 tn_dims, preferred_element_type=F32) for kk, d in zip(k, dv)]
        for (b, h), s, x, o, kk, qq, d in zip(units, st, outer, r, k, q, dv):
            sn_ref[b, h] = s + x
            o_s[h, pl.ds(b, 1), :] = o[1:2] + jnp.sum(qq * kk, axis=-1, keepdims=True) * d
        return carry

    lax.fori_loop(0, nb // SAMPLE_UNROLL, per_samples, 0)

    z = z_ref[...]
    for h in range(DN_HEADS):
        lo = h * HEAD_DIM
        dn_ref[:, lo:lo + HEAD_DIM] = (_rms(o_s[h], dnw_ref[...]) * _silu(z[:, lo:lo + HEAD_DIM])).astype(BF16)


def _gdn_sample(proj, gate, state_conv, state_delta, conv_w, alog, dtb, dnw, prev, l, row0):
    depth, nseq = state_delta.shape[:2]
    nb = SAMPLE_BLOCK
    base = row0 // nb
    vec = pl.BlockSpec((None, 1, LANES), lambda i: (l, 0, 0))
    delta_spec = pl.BlockSpec((None, nb, DN_HEADS, HEAD_DIM, HEAD_DIM), lambda i: (l, i, 0, 0, 0))
    conv_spec = pl.BlockSpec((None, CONV_W - 1, nb, 3 * DN_WIDTH), lambda i: (l, 0, i, 0))
    inputs = [proj, proj, gate, state_conv, state_delta, conv_w, alog, dtb, dnw]
    in_specs = [
        pl.BlockSpec((nb, 3 * DN_WIDTH), lambda i: (base + i, 0)),
        pl.BlockSpec((nb, DN_WIDTH), lambda i: (base + i, 3)),
        pl.BlockSpec((nb, GATE_COLS), lambda i: (base + i, 0)),
        conv_spec, delta_spec,
        pl.BlockSpec((None, CONV_W, 3 * DN_WIDTH), lambda i: (l, 0, 0)),
        vec, vec, vec,
    ]
    aliases = {}
    if prev is not None:
        inputs += list(prev)
        in_specs += [_any_spec(), _any_spec()]
        aliases = {9: 1, 10: 2}
    return pl.pallas_call(
        _gdn_sample_body,
        grid=(nseq // nb,),
        in_specs=in_specs,
        out_specs=[pl.BlockSpec((nb, DN_WIDTH), lambda i: (i, 0)), delta_spec, conv_spec],
        out_shape=[
            jax.ShapeDtypeStruct((nseq, DN_WIDTH), BF16),
            jax.ShapeDtypeStruct(state_delta.shape, F32),
            jax.ShapeDtypeStruct(state_conv.shape, F32),
        ],
        scratch_shapes=[
            pltpu.VMEM((DN_HEADS, nb, HEAD_DIM), F32),
            pltpu.VMEM((DN_HEADS, nb, HEAD_DIM), F32),
            pltpu.VMEM((DN_HEADS, nb, HEAD_DIM), F32),
            pltpu.VMEM((nb, LANES), F32),
            pltpu.VMEM((nb, LANES), F32),
            pltpu.VMEM((DN_HEADS, nb, HEAD_DIM), F32),
        ],
        input_output_aliases=aliases,
        compiler_params=_params(("parallel",), 48),
        name="gdn_sample",
    )(*inputs)


def _pool_matmul(d, w_ref, sc_ref, o_ref, gi):
    lo = gi * POOL_GROUP
    y = _dot(d.astype(BF16), w_ref[gi]) * sc_ref[:, lo:lo + POOL_GROUP]
    o_ref[:, lo:lo + POOL_GROUP] = y.astype(BF16)


def _pool_prompt_body(p_ref, prev_ref, w_ref, sc_ref, o_ref, *, tiles_per_seq):
    rt = p_ref.shape[0]
    hist = prev_ref.shape[0]
    t = pl.program_id(0) % tiles_per_seq
    cur = p_ref[...]
    prev = jnp.where(t == 0, 0.0, prev_ref[...])
    pos = t * rt + lax.broadcasted_iota(jnp.int32, (rt, POOL_GROUP), 0)
    for gi, win in enumerate(POOL_WINDOWS):
        lo = gi * POOL_GROUP
        x = cur[:, lo:lo + POOL_GROUP]
        s = jnp.concatenate([prev[:, lo:lo + POOL_GROUP], x], axis=0)
        sh = 1
        while sh < win:
            s = s + pltpu.roll(s, sh, axis=0)
            sh *= 2
        cnt = jnp.minimum(pos + 1, win).astype(F32)
        _pool_matmul(s[hist:] / cnt - x, w_ref, sc_ref, o_ref, gi)


def _pool_prompt(proj, w_pool, scale, l, batch, seq):
    rt = min(POOL_ROWS, seq)
    hist = POOL_BUF + 1
    tiles_per_seq = seq // rt
    pcol = (4 * DN_WIDTH) // POOL_WIDTH
    ng = len(POOL_WINDOWS)
    return pl.pallas_call(
        functools.partial(_pool_prompt_body, tiles_per_seq=tiles_per_seq),
        grid=(batch * tiles_per_seq,),
        in_specs=[
            pl.BlockSpec((rt, POOL_WIDTH), lambda r: (r, pcol)),
            pl.BlockSpec((hist, POOL_WIDTH), lambda r: (jnp.maximum(r * (rt // hist) - 1, 0), pcol)),
            pl.BlockSpec((None, ng, POOL_GROUP, POOL_GROUP), lambda r: (l, 0, 0, 0)),
            pl.BlockSpec((None, 1, POOL_WIDTH), lambda r: (l, 0, 0)),
        ],
        out_specs=pl.BlockSpec((rt, POOL_WIDTH), lambda r: (r, 0)),
        out_shape=jax.ShapeDtypeStruct((batch * seq, POOL_WIDTH), BF16),
        compiler_params=_params(("parallel",), 32),
        name="pool_prompt",
    )(proj, proj, w_pool, scale)


def _pool_sample_body(p_ref, buf_ref, w_ref, sc_ref, *rest):
    o_ref, nb_ref = rest[-2:]
    cur = p_ref[...]
    for gi, win in enumerate(POOL_WINDOWS):
        lo = gi * POOL_GROUP
        x = cur[:, lo:lo + POOL_GROUP]
        s = x
        for j in range(1, win):
            s = s + buf_ref[POOL_BUF - j, :, lo:lo + POOL_GROUP]
        cnt = float(min(PAST_LEN + 1, win))
        _pool_matmul(s / cnt - x, w_ref, sc_ref, o_ref, gi)
    for j in range(POOL_BUF - 1):
        nb_ref[j] = buf_ref[j + 1]
    nb_ref[POOL_BUF - 1] = cur


def _pool_sample(proj, state_pool, w_pool, scale, prev, l, row0):
    nseq = state_pool.shape[2]
    nb = POOL_SAMPLE_BLOCK
    base = row0 // nb
    pcol = (4 * DN_WIDTH) // POOL_WIDTH
    ng = len(POOL_WINDOWS)
    state_spec = pl.BlockSpec((None, POOL_BUF, nb, POOL_WIDTH), lambda i: (l, 0, i, 0))
    inputs = [proj, state_pool, w_pool, scale]
    in_specs = [
        pl.BlockSpec((nb, POOL_WIDTH), lambda i: (base + i, pcol)),
        state_spec,
        pl.BlockSpec((None, ng, POOL_GROUP, POOL_GROUP), lambda i: (l, 0, 0, 0)),
        pl.BlockSpec((None, 1, POOL_WIDTH), lambda i: (l, 0, 0)),
    ]
    aliases = {}
    if prev is not None:
        inputs.append(prev)
        in_specs.append(_any_spec())
        aliases = {4: 1}
    return pl.pallas_call(
        _pool_sample_body,
        grid=(nseq // nb,),
        in_specs=in_specs,
        out_specs=[pl.BlockSpec((nb, POOL_WIDTH), lambda i: (i, 0)), state_spec],
        out_shape=[
            jax.ShapeDtypeStruct((nseq, POOL_WIDTH), BF16),
            jax.ShapeDtypeStruct(state_pool.shape, F32),
        ],
        input_output_aliases=aliases,
        compiler_params=_params(("parallel",), 32),
        name="pool_sample",
    )(*inputs)


def _outproj_body(dn_ref, pool_ref, dns_ref, pools_ref, w1_ref, w2_ref, h_ref, *rest, sample_row0):
    hs_ref = rest[0] if len(rest) == 2 else None
    o_ref = rest[-1]
    is_last = pl.program_id(0) == pl.num_programs(0) - 1

    def project(dn, pool, h):
        o_ref[...] = h + _dot(dn, w1_ref[...]) + _dot(pool, w2_ref[...])

    @pl.when(jnp.logical_not(is_last))
    def _():
        project(dn_ref[...], pool_ref[...], h_ref[...])

    @pl.when(is_last)
    def _():
        h = h_ref[...] if hs_ref is None else jnp.concatenate([h_ref[:sample_row0, :], hs_ref[...]], axis=0)
        project(jnp.concatenate([dn_ref[:sample_row0, :], dns_ref[...]], axis=0),
                jnp.concatenate([pool_ref[:sample_row0, :], pools_ref[...]], axis=0), h)


def _outproj(dn_p, dn_s, pool_p, pool_s, w_out, h, h_tail, l):
    rows_p, nseq = dn_p.shape[0], dn_s.shape[0]
    m = rows_p + nseq
    tm, tn = _row_tile(m, OUTPROJ_ROW_TILE), OUTPROJ_COL_TILE
    sample_row0 = _tail_row0(rows_p, nseq, tm)
    tail = [] if h_tail is None else [h_tail]
    tail_specs = [] if h_tail is None else [pl.BlockSpec((nseq, tn), lambda i, j: (0, j))]
    return pl.pallas_call(
        functools.partial(_outproj_body, sample_row0=sample_row0),
        grid=(m // tm, D_MODEL // tn),
        in_specs=[
            pl.BlockSpec((tm, DN_WIDTH), lambda i, j: (i, 0)),
            pl.BlockSpec((tm, POOL_WIDTH), lambda i, j: (i, 0)),
            pl.BlockSpec((nseq, DN_WIDTH), lambda i, j: (0, 0)),
            pl.BlockSpec((nseq, POOL_WIDTH), lambda i, j: (0, 0)),
            pl.BlockSpec((None, DN_WIDTH, tn), lambda i, j: (l, 0, j)),
            pl.BlockSpec((None, POOL_WIDTH, tn), lambda i, j: (l, 1, j)),
            pl.BlockSpec((tm, tn), lambda i, j: (i, j)),
        ] + tail_specs,
        out_specs=pl.BlockSpec((tm, tn), lambda i, j: (i, j)),
        out_shape=jax.ShapeDtypeStruct((m, D_MODEL), F32),
        compiler_params=_params(("parallel", "parallel"), 54),
        name="outproj",
    )(dn_p, pool_p, dn_s, pool_s, w_out, w_out, h, *tail)


def _ffn_body(h_ref, nw_ref, wg_ref, wu_ref, wd_ref, nf_ref, *rest, final_norm, sample_row0, n_cast):
    cast_src, rest = rest[:n_cast], rest[n_cast:]
    o_ref, xn_ref = rest[0], rest[-1]
    cast_dst = rest[len(rest) - 1 - n_cast:len(rest) - 1]
    f = pl.program_id(1)
    last = pl.num_programs(1) - 1

    @pl.when(f == 0)
    def _():
        x = h_ref[...]
        xn_ref[...] = _rms(x, nw_ref[...]).astype(BF16)
        o_ref[...] = x

    xn = xn_ref[...]
    act = (_silu(_dot(xn, wg_ref[...])) * _dot(xn, wu_ref[...])).astype(BF16)
    o_ref[...] += _dot(act, wd_ref[...])
    for src, dst in zip(cast_src, cast_dst):
        dst[...] = src[...].astype(BF16)

    if final_norm:
        @pl.when(f == last)
        def _():
            o_ref[...] = _rms(o_ref[...], nf_ref[...])

    if sample_row0 is not None:
        @pl.when((f == last) & (pl.program_id(0) == pl.num_programs(0) - 1))
        def _():
            rest[1][...] = o_ref[sample_row0:, :]


def _ffn_cast_plan(m):
    steps = (m // _row_tile(m, FFN_ROW_TILE)) * (D_FF // FFN_TILE)
    n_units = 2 * D_FF // LANES
    for per_step in range(1, n_units + 1):
        if n_units % per_step == 0 and per_step * steps >= n_units:
            return per_step, n_units // per_step
    return None


def _ffn(h, nw, w_gate_up, w_down, nf, l, final_norm, split_rows=None, next_weights=None):
    m = h.shape[0]
    tm, tf = _row_tile(m, FFN_ROW_TILE), FFN_TILE
    nft = D_FF // tf
    out_spec = pl.BlockSpec((tm, D_MODEL), lambda i, f: (i, 0))
    if split_rows is None:
        out_specs, out_shape, sample_row0 = [out_spec], [jax.ShapeDtypeStruct((m, D_MODEL), F32)], None
    else:
        sample_row0 = split_rows - (m // tm - 1) * tm
        assert 0 < sample_row0 < tm, (m, tm, split_rows)
        out_specs = [out_spec, pl.BlockSpec((m - split_rows, D_MODEL), lambda i, f: (0, 0))]
        out_shape = [jax.ShapeDtypeStruct((split_rows, D_MODEL), F32),
                     jax.ShapeDtypeStruct((m - split_rows, D_MODEL), F32)]
    inputs = [h, nw, w_gate_up, w_gate_up, w_down, nf]
    in_specs = [
        pl.BlockSpec((tm, D_MODEL), lambda i, f: (i, 0)),
        pl.BlockSpec((None, 1, D_MODEL), lambda i, f: (l, 0, 0)),
        pl.BlockSpec((D_MODEL, tf), lambda i, f: (0, f)),
        pl.BlockSpec((D_MODEL, tf), lambda i, f: (0, nft + f)),
        pl.BlockSpec((tf, D_MODEL), lambda i, f: (f, 0)),
        pl.BlockSpec((1, D_MODEL), lambda i, f: (0, 0)),
    ]
    n_cast = 0
    if next_weights is not None:
        per_step, n_blocks = _ffn_cast_plan(m)
        cols, rows = per_step * LANES, per_step * (D_FF // (2 * D_FF // LANES))
        blk = lambda i, f: jnp.minimum(i * nft + f, n_blocks - 1)
        n_cast = 2
        inputs += list(next_weights)
        in_specs += [pl.BlockSpec((None, D_MODEL, cols), lambda i, f: (l + 1, 0, blk(i, f))),
                     pl.BlockSpec((None, rows, D_MODEL), lambda i, f: (l + 1, blk(i, f), 0))]
        out_specs += [pl.BlockSpec((D_MODEL, cols), lambda i, f: (0, blk(i, f))),
                      pl.BlockSpec((rows, D_MODEL), lambda i, f: (blk(i, f), 0))]
        out_shape += [jax.ShapeDtypeStruct((D_MODEL, 2 * D_FF), BF16), jax.ShapeDtypeStruct((D_FF, D_MODEL), BF16)]
    out = pl.pallas_call(
        functools.partial(_ffn_body, final_norm=final_norm, sample_row0=sample_row0, n_cast=n_cast),
        grid=(m // tm, nft),
        in_specs=in_specs,
        out_specs=out_specs,
        out_shape=out_shape,
        scratch_shapes=[pltpu.VMEM((tm, D_MODEL), BF16)],
        compiler_params=_params(("parallel", "arbitrary"), 48),
        name="ffn",
    )(*inputs)
    return out[0] if len(out) == 1 else out


def _lane_vecs(x):
    depth, n = x.shape
    return jnp.zeros((depth, 1, LANES), F32).at[:, 0, :n].set(x)


def kernel(x_prompt, x_sample, state_delta, state_conv, state_pool, norm_mix, w_in, conv_w, a_log, dt_bias,
           dn_norm, w_pool, pool_scale, w_out, norm_ffn, w_gate_up, w_down, norm_final):
    batch, seq, _ = x_prompt.shape
    nseq = x_sample.shape[0]
    depth = w_in.shape[0]
    rows_p = batch * seq
    h, h_tail = x_prompt.reshape(rows_p, D_MODEL), x_sample.reshape(nseq, D_MODEL)

    o1 = 4 * DN_WIDTH
    w_a = w_in.astype(BF16)
    w_b = w_a[:, :, o1 + 2 * DN_HEADS:]
    lane_pad = ((0, 0), (0, 0), (0, LANES - DN_HEADS))
    w_gate = jnp.concatenate([jnp.pad(w_a[:, :, o1:o1 + DN_HEADS], lane_pad),
                              jnp.pad(w_a[:, :, o1 + DN_HEADS:o1 + 2 * DN_HEADS], lane_pad)], axis=-1)
    w_out_b = w_out.astype(BF16)
    w_pool_b = w_pool.astype(BF16)
    alog, dtb = _lane_vecs(a_log), _lane_vecs(dt_bias)
    dnw = dn_norm.reshape(depth, 1, HEAD_DIM)
    scale = pool_scale.reshape(depth, 1, POOL_WIDTH)
    nmix = norm_mix.reshape(depth, 1, D_MODEL)
    nffn = norm_ffn.reshape(depth, 1, D_MODEL)
    nfin = norm_final.reshape(1, D_MODEL)
    conv_state = jnp.transpose(state_conv, (0, 2, 1, 3))
    pool_state = jnp.transpose(state_pool, (0, 2, 1, 3))
    hide_casts = _ffn_cast_plan(rows_p + nseq) is not None
    w_gu_l, w_down_l = w_gate_up[0].astype(BF16), w_down[0].astype(BF16)

    delta_p, conv_p, poolbuf_p = [], [], []
    sample_states = None
    pool_states = None
    for l in range(depth):
        proj, gate = _inproj(h, h_tail, nmix, w_a, w_b, w_gate, l)

        beta, gc = _gates(gate, alog, dtb, l, batch, seq)
        u, wq, kdt, qk, gl = _gdn_prep(proj, beta, gc, conv_w, l, batch, seq)
        dn_p, s_p = _gdn_scan(u, wq, kdt, qk, gl, proj, dnw, l, batch, seq)
        dn_s, *sample_states = _gdn_sample(proj, gate, conv_state, state_delta, conv_w, alog, dtb, dnw,
                                           sample_states, l, rows_p)
        pool_p = _pool_prompt(proj, w_pool_b, scale, l, batch, seq)
        pool_s, pool_states = _pool_sample(proj, pool_state, w_pool_b, scale, pool_states, l, rows_p)

        delta_p.append(s_p)
        conv_p.append(jnp.stack([proj[(b + 1) * seq - (CONV_W - 1):(b + 1) * seq, :3 * DN_WIDTH]
                                 for b in range(batch)]))
        poolbuf_p.append(jnp.stack([proj[(b + 1) * seq - POOL_BUF:(b + 1) * seq, o1:] for b in range(batch)]))

        h, h_tail = _outproj(dn_p, dn_s, pool_p, pool_s, w_out_b, h, h_tail, l), None
        if l < depth - 1:
            if hide_casts:
                h, w_gu_l, w_down_l = _ffn(h, nffn, w_gu_l, w_down_l, nfin, l, False,
                                           next_weights=(w_gate_up, w_down))
            else:
                h = _ffn(h, nffn, w_gu_l, w_down_l, nfin, l, False)
                w_gu_l, w_down_l = w_gate_up[l + 1].astype(BF16), w_down[l + 1].astype(BF16)
        elif rows_p % _row_tile(h.shape[0], FFN_ROW_TILE) == 0:
            h = _ffn(h, nffn, w_gu_l, w_down_l, nfin, l, True)
            y_p, y_s = h[:rows_p], h[rows_p:]
        else:
            y_p, y_s = _ffn(h, nffn, w_gu_l, w_down_l, nfin, l, True, split_rows=rows_p)

    delta_s, conv_s = sample_states
    y_prompt = y_p.reshape(batch, seq, D_MODEL)
    y_sample = y_s.reshape(nseq, 1, D_MODEL)
    return (y_prompt, y_sample, jnp.stack(delta_p), jnp.stack(conv_p), jnp.stack(poolbuf_p),
            delta_s, jnp.transpose(conv_s, (0, 2, 1, 3)), jnp.transpose(pool_states, (0, 2, 1, 3)))
```

```python
import functools

import jax
import jax.numpy as jnp
from jax import lax
from jax.experimental import pallas as pl
from jax.experimental.pallas import tpu as pltpu

F32 = jnp.float32
BF16 = jnp.bfloat16

D_MODEL = 2048
DN_WIDTH = 1024
DN_HEADS = 8
HEAD_DIM = 128
CONV_W = 4
CHUNK = 128
POOL_WIDTH = 1024
POOL_WINDOWS = (2, 4, 8, 16)
POOL_GROUP = 256
POOL_BUF = 15
D_FF = 5632
EPS = 1e-6
PAST_LEN = 16384

MAIN_COLS = 4 * DN_WIDTH + POOL_WIDTH
GATE_COLS = 256
LANES = 128
SUBLANES = 8

ROW_TILES = (1040, 640, 512, 256, 128)
INPROJ_COL_TILE = 1024
OUTPROJ_ROW_TILE = 640
OUTPROJ_COL_TILE = 2048
FFN_ROW_TILE = 640
FFN_TILE = 512
CONV_ROWS = 256
PREP_ROWS = 2048
PREP_UNROLL = 16
SCAN_HEADS = 8
SCAN_ROWS = 512
POOL_ROWS = 512
SAMPLE_BLOCK = 16
SAMPLE_UNROLL = 4
POOL_SAMPLE_BLOCK = 32
MIB = 1024 * 1024


def _params(semantics, vmem_mib):
    return pltpu.CompilerParams(dimension_semantics=semantics, vmem_limit_bytes=vmem_mib * MIB)


def _silu(x):
    return x * jax.nn.sigmoid(x)


def _softplus(x):
    return jnp.maximum(x, 0.0) + jnp.log1p(jnp.exp(-jnp.abs(x)))


def _rms(x, w):
    return x * lax.rsqrt(jnp.mean(x * x, axis=-1, keepdims=True) + EPS) * w


def _l2(x):
    return x * lax.rsqrt(jnp.sum(x * x, axis=-1, keepdims=True) + EPS)


def _dot(a, b):
    return jnp.dot(a, b, preferred_element_type=F32)


def _row_tile(m, largest=ROW_TILES[0]):
    return next(t for t in ROW_TILES if t <= largest and m % t == 0)


def _any_spec():
    return pl.BlockSpec(memory_space=pl.ANY)


def _stacked_rows(head_ref, tail_ref, tail_row0, is_last, fn):
    if tail_ref is None:
        fn(head_ref[...])
        return

    @pl.when(jnp.logical_not(is_last))
    def _():
        fn(head_ref[...])

    @pl.when(is_last)
    def _():
        fn(jnp.concatenate([head_ref[:tail_row0, :], tail_ref[...]], axis=0))


def _tail_row0(rows_head, rows_tail, tm):
    row0 = rows_head - ((rows_head + rows_tail) // tm - 1) * tm
    assert 0 < row0 and row0 + rows_tail == tm, (rows_head, rows_tail, tm)
    return row0


def _inproj_body(h_ref, *rest, n_a, tail_row0):
    tail_ref = None if tail_row0 is None else rest[0]
    nw_ref, wa_ref, wb_ref, wg_ref, proj_ref, gate_ref, xn_ref = rest[-7:]
    j = pl.program_id(1)
    is_last = pl.program_id(0) == pl.num_programs(0) - 1

    def normalise(x):
        xn = _rms(x, nw_ref[...]).astype(BF16)
        xn_ref[...] = xn
        gate_ref[...] = _dot(xn, wg_ref[...])

    @pl.when(j == 0)
    def _():
        _stacked_rows(h_ref, tail_ref, tail_row0, is_last, normalise)

    @pl.when(j < n_a)
    def _():
        proj_ref[...] = _dot(xn_ref[...], wa_ref[...])

    @pl.when(j >= n_a)
    def _():
        proj_ref[...] = _dot(xn_ref[...], wb_ref[...])


def _inproj(h, h_tail, nw, w_a, w_b, w_gate, l):
    m = h.shape[0] + (0 if h_tail is None else h_tail.shape[0])
    tm, tn = _row_tile(m), INPROJ_COL_TILE
    n_a = (MAIN_COLS - w_b.shape[-1]) // tn
    tail_row0 = None if h_tail is None else _tail_row0(h.shape[0], h_tail.shape[0], tm)
    tail = [] if h_tail is None else [h_tail]
    tail_specs = [] if h_tail is None else [pl.BlockSpec(h_tail.shape, lambda i, j: (0, 0))]
    return pl.pallas_call(
        functools.partial(_inproj_body, n_a=n_a, tail_row0=tail_row0),
        grid=(m // tm, MAIN_COLS // tn),
        in_specs=[pl.BlockSpec((tm, D_MODEL), lambda i, j: (i, 0))] + tail_specs + [
            pl.BlockSpec((None, 1, D_MODEL), lambda i, j: (l, 0, 0)),
            pl.BlockSpec((None, D_MODEL, tn), lambda i, j: (l, 0, jnp.minimum(j, n_a - 1))),
            pl.BlockSpec((None, D_MODEL, tn), lambda i, j: (l, 0, jnp.maximum(j - n_a, 0))),
            pl.BlockSpec((None, D_MODEL, GATE_COLS), lambda i, j: (l, 0, 0)),
        ],
        out_specs=[
            pl.BlockSpec((tm, tn), lambda i, j: (i, j)),
            pl.BlockSpec((tm, GATE_COLS), lambda i, j: (i, 0)),
        ],
        out_shape=[
            jax.ShapeDtypeStruct((m, MAIN_COLS), F32),
            jax.ShapeDtypeStruct((m, GATE_COLS), F32),
        ],
        scratch_shapes=[pltpu.VMEM((tm, D_MODEL), BF16)],
        compiler_params=_params(("parallel", "arbitrary"), 54),
        name="inproj",
    )(h, *tail, nw, w_a, w_b, w_gate)


def _gates_body(gate_ref, alog_ref, dtb_ref, beta_ref, gc_ref):
    beta_ref[...] = jax.nn.sigmoid(gate_ref[:, 0:LANES])
    g = -jnp.exp(alog_ref[...]) * _softplus(gate_ref[:, LANES:2 * LANES] + dtb_ref[...])
    pos = lax.broadcasted_iota(jnp.int32, g.shape, 0) & (CHUNK - 1)
    sh = 1
    while sh < CHUNK:
        g = g + jnp.where(pos >= sh, pltpu.roll(g, sh, axis=0), 0.0)
        sh *= 2
    gc_ref[...] = g


def _gates(gate, alog, dtb, l, batch, seq):
    vec = pl.BlockSpec((None, 1, LANES), lambda b: (l, 0, 0))
    return pl.pallas_call(
        _gates_body,
        grid=(batch,),
        in_specs=[pl.BlockSpec((seq, GATE_COLS), lambda b: (b, 0)), vec, vec],
        out_specs=[pl.BlockSpec((seq, LANES), lambda b: (b, 0))] * 2,
        out_shape=[jax.ShapeDtypeStruct((batch * seq, LANES), F32)] * 2,
        compiler_params=_params(("parallel",), 32),
        name="gates",
    )(gate, alog, dtb)


def _gdn_prep_body(q_ref, k_ref, v_ref, qh_ref, kh_ref, vh_ref, cwq_ref, cwk_ref, cwv_ref, beta_ref, gc_ref,
                   u_ref, wq_ref, kdt_ref, qk_ref, gl_ref, qs, ks, vs):
    head = pl.program_id(1)
    r = pl.program_id(2)
    rp = q_ref.shape[0]
    rb = min(CONV_ROWS, rp)

    def conv_silu(x_ref, halo_ref, cw_ref, i):
        cw = cw_ref[...]
        if i == 0:
            halo = jnp.where(r > 0, halo_ref[...], 0.0)
            ext = jnp.concatenate([halo, x_ref[0:rb, :]], axis=0)
            off = SUBLANES - (CONV_W - 1)
            taps = [ext[off + t:off + t + rb] for t in range(CONV_W)]
        else:
            off = i * rb - (CONV_W - 1)
            taps = [x_ref[off + t:off + t + rb, :] for t in range(CONV_W)]
        y = taps[0] * cw[0:1]
        for t in range(1, CONV_W):
            y = y + taps[t] * cw[t:t + 1]
        return _silu(y)

    def normalise(i):
        rows = slice(i * rb, (i + 1) * rb)
        qs[rows, :] = _l2(conv_silu(q_ref, qh_ref, cwq_ref, i)) * (HEAD_DIM ** -0.5)
        ks[rows, :] = _l2(conv_silu(k_ref, kh_ref, cwk_ref, i))
        vs[rows, :] = conv_silu(v_ref, vh_ref, cwv_ref, i)

    head_lane = lax.broadcasted_iota(jnp.int32, (CHUNK, LANES), 1) == head
    ri = lax.broadcasted_iota(jnp.int32, (CHUNK, CHUNK), 0)
    ci = lax.broadcasted_iota(jnp.int32, (CHUNK, CHUNK), 1)
    tril = ri >= ci
    strict = ri > ci
    half = CHUNK // 2
    same_half = (ri >= half) == (ci >= half)
    lower_left = (ri >= half) & (ci < half)
    eye = (ri == ci).astype(F32)
    nt_dims = (((1,), (1,)), ((), ()))

    def lane_bcast(x):
        return jnp.broadcast_to(jnp.sum(jnp.where(head_lane, x, 0.0), axis=1, keepdims=True), (CHUNK, LANES))

    def group(js):
        rows = [slice(j * CHUNK, (j + 1) * CHUNK) for j in js]
        q = [qs[r, :] for r in rows]
        k = [ks[r, :] for r in rows]
        bb = [lane_bcast(beta_ref[r, :]) for r in rows]
        gc = [lane_bcast(gc_ref[r, :]) for r in rows]
        decay = [jnp.where(tril, jnp.exp(jnp.where(tril, x - x.T, 0.0)), 0.0) for x in gc]
        kb = [a * b for a, b in zip(k, bb)]
        a1 = [lax.dot_general(jnp.concatenate([a, b], axis=0), x, nt_dims, preferred_element_type=F32)
              for a, b, x in zip(q, kb, k)]
        for r, a, d in zip(rows, a1, decay):
            qk_ref[r, :] = (a[:CHUNK] * d).astype(BF16)
        kbk = [a[CHUNK:] * d for a, d in zip(a1, decay)]
        m_d = [jnp.where(strict & same_half, x, 0.0) for x in kbk]
        corner = [jnp.where(lower_left, x, 0.0) for x in kbk]
        inv = [eye - x for x in m_d]
        mp = m_d
        sh = 1
        while 4 * sh < CHUNK:
            sh *= 2
            mp = [_dot(x, x) for x in mp]
            inv = [a + _dot(a, x) for a, x in zip(inv, mp)]
        dc = [_dot(a, x) for a, x in zip(inv, corner)]
        inv = [a - _dot(x, a) for a, x in zip(inv, dc)]
        eg = [jnp.exp(x) for x in gc]
        uw = [_dot(a, jnp.concatenate([vs[r, :] * b, x * e], axis=1))
              for a, r, b, x, e in zip(inv, rows, bb, kb, eg)]
        for j, r, x, a, e, kj, gcx in zip(js, rows, uw, q, eg, k, gc):
            u_ref[r, :] = x[:, :HEAD_DIM]
            wq_ref[2 * j * CHUNK:2 * (j + 1) * CHUNK, :] = jnp.concatenate(
                [x[:, HEAD_DIM:], a * e], axis=0).astype(BF16)
            glast = gcx[CHUNK - 1:CHUNK, :]
            kdt_ref[:, r] = (kj * jnp.exp(glast - gcx)).T.astype(BF16)
            gl_ref[j * SUBLANES:(j + 1) * SUBLANES, :] = jnp.broadcast_to(jnp.exp(glast), (SUBLANES, LANES))

    n_chunks = rp // CHUNK
    unroll = min(PREP_UNROLL, n_chunks)
    blocks_per_part = max(unroll * CHUNK // rb, 1)
    for part in range(n_chunks // unroll):
        for i in range(part * blocks_per_part, min((part + 1) * blocks_per_part, rp // rb)):
            normalise(i)
        group([part * unroll + t for t in range(unroll)])


def _gdn_prep(proj, beta, gc, conv_w, l, batch, seq):
    nh = DN_HEADS
    rp = min(PREP_ROWS, seq)
    nr = seq // rp
    n_chunks = seq // CHUNK
    col = lambda base: (lambda b, h, r: (b * nr + r, base + h))
    halo = lambda base: (lambda b, h, r: (jnp.maximum((b * nr + r) * (rp // SUBLANES) - 1, 0), base + h))
    cwcol = lambda base: (lambda b, h, r: (l, 0, base + h))
    gate_spec = pl.BlockSpec((rp, LANES), lambda b, h, r: (b * nr + r, 0))
    per_head = lambda rows, cols: pl.BlockSpec((None, None, rows, cols), lambda b, h, r: (b, h, r, 0))
    return pl.pallas_call(
        _gdn_prep_body,
        grid=(batch, nh, nr),
        in_specs=[
            pl.BlockSpec((rp, HEAD_DIM), col(0)),
            pl.BlockSpec((rp, HEAD_DIM), col(nh)),
            pl.BlockSpec((rp, HEAD_DIM), col(2 * nh)),
            pl.BlockSpec((SUBLANES, HEAD_DIM), halo(0)),
            pl.BlockSpec((SUBLANES, HEAD_DIM), halo(nh)),
            pl.BlockSpec((SUBLANES, HEAD_DIM), halo(2 * nh)),
            pl.BlockSpec((None, CONV_W, HEAD_DIM), cwcol(0)),
            pl.BlockSpec((None, CONV_W, HEAD_DIM), cwcol(nh)),
            pl.BlockSpec((None, CONV_W, HEAD_DIM), cwcol(2 * nh)),
            gate_spec, gate_spec,
        ],
        out_specs=[
            per_head(rp, HEAD_DIM),
            per_head(2 * rp, HEAD_DIM),
            pl.BlockSpec((None, None, HEAD_DIM, rp), lambda b, h, r: (b, h, 0, r)),
            per_head(rp, HEAD_DIM),
            per_head((rp // CHUNK) * SUBLANES, LANES),
        ],
        out_shape=[
            jax.ShapeDtypeStruct((batch, nh, seq, HEAD_DIM), F32),
            jax.ShapeDtypeStruct((batch, nh, 2 * seq, HEAD_DIM), BF16),
            jax.ShapeDtypeStruct((batch, nh, HEAD_DIM, seq), BF16),
            jax.ShapeDtypeStruct((batch, nh, seq, CHUNK), BF16),
            jax.ShapeDtypeStruct((batch, nh, n_chunks * SUBLANES, LANES), F32),
        ],
        scratch_shapes=[pltpu.VMEM((rp, HEAD_DIM), F32)] * 3,
        compiler_params=_params(("parallel", "parallel", "parallel"), 32),
        name="gdn_prep",
    )(proj, proj, proj, proj, proj, proj, conv_w, conv_w, conv_w, beta, gc)


def _gdn_scan_body(u_ref, wq_ref, kdt_ref, qk_ref, gl_ref, z_ref, dnw_ref, o_ref, s_out_ref, st_s):
    hs, rows_per_step = u_ref.shape[0], u_ref.shape[1]
    step = pl.program_id(2)

    @pl.when(step == 0)
    def _():
        st_s[...] = jnp.zeros(st_s.shape, F32)

    def chunk(c, carry):
        rows = pl.ds(pl.multiple_of(c * CHUNK, CHUNK), CHUNK)
        wrows = pl.ds(pl.multiple_of(c * 2 * CHUNK, 2 * CHUNK), 2 * CHUNK)
        grows = pl.ds(pl.multiple_of(c * SUBLANES, SUBLANES), SUBLANES)
        heads = range(hs)
        st = [st_s[hh] for hh in heads]
        r = [_dot(wq_ref[hh, wrows, :], st[hh].astype(BF16)) for hh in heads]
        vnew = [(u_ref[hh, rows, :] - r[hh][:CHUNK]).astype(BF16) for hh in heads]
        for hh in heads:
            st_s[hh] = st[hh] * gl_ref[hh, grows, :][0:1, :] + _dot(kdt_ref[hh, :, rows], vnew[hh])
        o = [r[hh][CHUNK:] + _dot(qk_ref[hh, rows, :], vnew[hh]) for hh in heads]
        for hh in heads:
            lo = hh * HEAD_DIM
            o_ref[rows, lo:lo + HEAD_DIM] = (
                _rms(o[hh], dnw_ref[...]) * _silu(z_ref[rows, lo:lo + HEAD_DIM])).astype(BF16)
        return carry

    lax.fori_loop(0, rows_per_step // CHUNK, chunk, 0, unroll=True)

    @pl.when(step == pl.num_programs(2) - 1)
    def _():
        s_out_ref[...] = st_s[...]


def _gdn_scan(u, wq, kdt, qk, gl, proj, dnw, l, batch, seq):
    hs = SCAN_HEADS
    rs = min(SCAN_ROWS, seq)
    nr = seq // rs
    zcol = (3 * DN_WIDTH) // (hs * HEAD_DIM)
    per_group = lambda rows, cols: pl.BlockSpec((None, hs, rows, cols), lambda b, g, r: (b, g, r, 0))
    return pl.pallas_call(
        _gdn_scan_body,
        grid=(batch, DN_HEADS // hs, nr),
        in_specs=[
            per_group(rs, HEAD_DIM),
            per_group(2 * rs, HEAD_DIM),
            pl.BlockSpec((None, hs, HEAD_DIM, rs), lambda b, g, r: (b, g, 0, r)),
            per_group(rs, CHUNK),
            per_group((rs // CHUNK) * SUBLANES, LANES),
            pl.BlockSpec((rs, hs * HEAD_DIM), lambda b, g, r: (b * nr + r, zcol + g)),
            pl.BlockSpec((None, 1, HEAD_DIM), lambda b, g, r: (l, 0, 0)),
        ],
        out_specs=[
            pl.BlockSpec((rs, hs * HEAD_DIM), lambda b, g, r: (b * nr + r, g)),
            pl.BlockSpec((None, hs, HEAD_DIM, HEAD_DIM), lambda b, g, r: (b, g, 0, 0)),
        ],
        out_shape=[
            jax.ShapeDtypeStruct((batch * seq, DN_WIDTH), BF16),
            jax.ShapeDtypeStruct((batch, DN_HEADS, HEAD_DIM, HEAD_DIM), F32),
        ],
        scratch_shapes=[pltpu.VMEM((hs, HEAD_DIM, HEAD_DIM), F32)],
        compiler_params=_params(("parallel", "parallel", "arbitrary"), 40),
        name="gdn_scan",
    )(u, wq, kdt, qk, gl, proj, dnw)


def _gdn_sample_body(qkv_ref, z_ref, gate_ref, cs_ref, s_ref, cw_ref, alog_ref, dtb_ref, dnw_ref, *rest):
    dn_ref, sn_ref, cn_ref, q_s, k_s, v_s, eg_s, beta_s, o_s = rest[-9:]
    nb = qkv_ref.shape[0]
    x = qkv_ref[...]
    c0, c1, c2 = cs_ref[0], cs_ref[1], cs_ref[2]
    cw = cw_ref[...]
    y = c0 * cw[0:1] + c1 * cw[1:2] + c2 * cw[2:3] + x * cw[3:4]
    y = _silu(y)
    cn_ref[0] = c1
    cn_ref[1] = c2
    cn_ref[2] = x
    for h in range(DN_HEADS):
        lo = h * HEAD_DIM
        q_s[h] = _l2(y[:, lo:lo + HEAD_DIM]) * (HEAD_DIM ** -0.5)
        k_s[h] = _l2(y[:, DN_WIDTH + lo:DN_WIDTH + lo + HEAD_DIM])
        v_s[h] = y[:, 2 * DN_WIDTH + lo:2 * DN_WIDTH + lo + HEAD_DIM]
    beta_s[...] = jax.nn.sigmoid(gate_ref[:, 0:LANES])
    eg_s[...] = jnp.exp(-jnp.exp(alog_ref[...]) * _softplus(gate_ref[:, LANES:2 * LANES] + dtb_ref[...]))

    pad7 = jnp.zeros((SUBLANES - 1, HEAD_DIM), F32)
    pad6 = jnp.zeros((SUBLANES - 2, HEAD_DIM), F32)

    tn_dims = (((0,), (0,)), ((), ()))

    def per_samples(i, carry):
        units = [(i * SAMPLE_UNROLL + t, h) for t in range(SAMPLE_UNROLL) for h in range(DN_HEADS)]
        k = [k_s[h, pl.ds(b, 1), :] for b, h in units]
        q = [q_s[h, pl.ds(b, 1), :] for b, h in units]
        st = [s_ref[b, h] * eg_s[pl.ds(b, 1), :][:, h:h + 1] for b, h in units]
        r = [_dot(jnp.concatenate([kk, qq, pad6], axis=0), s) for kk, qq, s in zip(k, q, st)]
        dv = [(v_s[h, pl.ds(b, 1), :] - x[0:1]) * beta_s[pl.ds(b, 1), :][:, h:h + 1]
              for (b, h), x in zip(units, r)]
        outer = [lax.dot_general(jnp.concatenate([kk, pad7], axis=0), jnp.concatenate([d, pad7], axis=0),
                                 tn_dims, preferred_element_type=F32) for kk, d in zip(k, dv)]
        for (b, h), s, x, o, kk, qq, d in zip(units, st, outer, r, k, q, dv):
            sn_ref[b, h] = s + x
            o_s[h, pl.ds(b, 1), :] = o[1:2] + jnp.sum(qq * kk, axis=-1, keepdims=True) * d
        return carry

    lax.fori_loop(0, nb // SAMPLE_UNROLL, per_samples, 0)

    z = z_ref[...]
    for h in range(DN_HEADS):
        lo = h * HEAD_DIM
        dn_ref[:, lo:lo + HEAD_DIM] = (_rms(o_s[h], dnw_ref[...]) * _silu(z[:, lo:lo + HEAD_DIM])).astype(BF16)


def _gdn_sample(proj, gate, state_conv, state_delta, conv_w, alog, dtb, dnw, prev, l, row0):
    depth, nseq = state_delta.shape[:2]
    nb = SAMPLE_BLOCK
    base = row0 // nb
    vec = pl.BlockSpec((None, 1, LANES), lambda i: (l, 0, 0))
    delta_spec = pl.BlockSpec((None, nb, DN_HEADS, HEAD_DIM, HEAD_DIM), lambda i: (l, i, 0, 0, 0))
    conv_spec = pl.BlockSpec((None, CONV_W - 1, nb, 3 * DN_WIDTH), lambda i: (l, 0, i, 0))
    inputs = [proj, proj, gate, state_conv, state_delta, conv_w, alog, dtb, dnw]
    in_specs = [
        pl.BlockSpec((nb, 3 * DN_WIDTH), lambda i: (base + i, 0)),
        pl.BlockSpec((nb, DN_WIDTH), lambda i: (base + i, 3)),
        pl.BlockSpec((nb, GATE_COLS), lambda i: (base + i, 0)),
        conv_spec, delta_spec,
        pl.BlockSpec((None, CONV_W, 3 * DN_WIDTH), lambda i: (l, 0, 0)),
        vec, vec, vec,
    ]
    aliases = {}
    if prev is not None:
        inputs += list(prev)
        in_specs += [_any_spec(), _any_spec()]
        aliases = {9: 1, 10: 2}
    return pl.pallas_call(
        _gdn_sample_body,
        grid=(nseq // nb,),
        in_specs=in_specs,
        out_specs=[pl.BlockSpec((nb, DN_WIDTH), lambda i: (i, 0)), delta_spec, conv_spec],
        out_shape=[
            jax.ShapeDtypeStruct((nseq, DN_WIDTH), BF16),
            jax.ShapeDtypeStruct(state_delta.shape, F32),
            jax.ShapeDtypeStruct(state_conv.shape, F32),
        ],
        scratch_shapes=[
            pltpu.VMEM((DN_HEADS, nb, HEAD_DIM), F32),
            pltpu.VMEM((DN_HEADS, nb, HEAD_DIM), F32),
            pltpu.VMEM((DN_HEADS, nb, HEAD_DIM), F32),
            pltpu.VMEM((nb, LANES), F32),
            pltpu.VMEM((nb, LANES), F32),
            pltpu.VMEM((DN_HEADS, nb, HEAD_DIM), F32),
        ],
        input_output_aliases=aliases,
        compiler_params=_params(("parallel",), 48),
        name="gdn_sample",
    )(*inputs)


def _pool_matmul(d, w_ref, sc_ref, o_ref, gi):
    lo = gi * POOL_GROUP
    y = _dot(d.astype(BF16), w_ref[gi]) * sc_ref[:, lo:lo + POOL_GROUP]
    o_ref[:, lo:lo + POOL_GROUP] = y.astype(BF16)


def _pool_prompt_body(p_ref, prev_ref, w_ref, sc_ref, o_ref, *, tiles_per_seq):
    rt = p_ref.shape[0]
    hist = prev_ref.shape[0]
    t = pl.program_id(0) % tiles_per_seq
    cur = p_ref[...]
    prev = jnp.where(t == 0, 0.0, prev_ref[...])
    pos = t * rt + lax.broadcasted_iota(jnp.int32, (rt, POOL_GROUP), 0)
    for gi, win in enumerate(POOL_WINDOWS):
        lo = gi * POOL_GROUP
        x = cur[:, lo:lo + POOL_GROUP]
        s = jnp.concatenate([prev[:, lo:lo + POOL_GROUP], x], axis=0)
        sh = 1
        while sh < win:
            s = s + pltpu.roll(s, sh, axis=0)
            sh *= 2
        cnt = jnp.minimum(pos + 1, win).astype(F32)
        _pool_matmul(s[hist:] / cnt - x, w_ref, sc_ref, o_ref, gi)


def _pool_prompt(proj, w_pool, scale, l, batch, seq):
    rt = min(POOL_ROWS, seq)
    hist = POOL_BUF + 1
    tiles_per_seq = seq // rt
    pcol = (4 * DN_WIDTH) // POOL_WIDTH
    ng = len(POOL_WINDOWS)
    return pl.pallas_call(
        functools.partial(_pool_prompt_body, tiles_per_seq=tiles_per_seq),
        grid=(batch * tiles_per_seq,),
        in_specs=[
            pl.BlockSpec((rt, POOL_WIDTH), lambda r: (r, pcol)),
            pl.BlockSpec((hist, POOL_WIDTH), lambda r: (jnp.maximum(r * (rt // hist) - 1, 0), pcol)),
            pl.BlockSpec((None, ng, POOL_GROUP, POOL_GROUP), lambda r: (l, 0, 0, 0)),
            pl.BlockSpec((None, 1, POOL_WIDTH), lambda r: (l, 0, 0)),
        ],
        out_specs=pl.BlockSpec((rt, POOL_WIDTH), lambda r: (r, 0)),
        out_shape=jax.ShapeDtypeStruct((batch * seq, POOL_WIDTH), BF16),
        compiler_params=_params(("parallel",), 32),
        name="pool_prompt",
    )(proj, proj, w_pool, scale)


def _pool_sample_body(p_ref, buf_ref, w_ref, sc_ref, *rest):
    o_ref, nb_ref = rest[-2:]
    cur = p_ref[...]
    for gi, win in enumerate(POOL_WINDOWS):
        lo = gi * POOL_GROUP
        x = cur[:, lo:lo + POOL_GROUP]
        s = x
        for j in range(1, win):
            s = s + buf_ref[POOL_BUF - j, :, lo:lo + POOL_GROUP]
        cnt = float(min(PAST_LEN + 1, win))
        _pool_matmul(s / cnt - x, w_ref, sc_ref, o_ref, gi)
    for j in range(POOL_BUF - 1):
        nb_ref[j] = buf_ref[j + 1]
    nb_ref[POOL_BUF - 1] = cur


def _pool_sample(proj, state_pool, w_pool, scale, prev, l, row0):
    nseq = state_pool.shape[2]
    nb = POOL_SAMPLE_BLOCK
    base = row0 // nb
    pcol = (4 * DN_WIDTH) // POOL_WIDTH
    ng = len(POOL_WINDOWS)
    state_spec = pl.BlockSpec((None, POOL_BUF, nb, POOL_WIDTH), lambda i: (l, 0, i, 0))
    inputs = [proj, state_pool, w_pool, scale]
    in_specs = [
        pl.BlockSpec((nb, POOL_WIDTH), lambda i: (base + i, pcol)),
        state_spec,
        pl.BlockSpec((None, ng, POOL_GROUP, POOL_GROUP), lambda i: (l, 0, 0, 0)),
        pl.BlockSpec((None, 1, POOL_WIDTH), lambda i: (l, 0, 0)),
    ]
    aliases = {}
    if prev is not None:
        inputs.append(prev)
        in_specs.append(_any_spec())
        aliases = {4: 1}
    return pl.pallas_call(
        _pool_sample_body,
        grid=(nseq // nb,),
        in_specs=in_specs,
        out_specs=[pl.BlockSpec((nb, POOL_WIDTH), lambda i: (i, 0)), state_spec],
        out_shape=[
            jax.ShapeDtypeStruct((nseq, POOL_WIDTH), BF16),
            jax.ShapeDtypeStruct(state_pool.shape, F32),
        ],
        input_output_aliases=aliases,
        compiler_params=_params(("parallel",), 32),
        name="pool_sample",
    )(*inputs)


def _outproj_body(dn_ref, pool_ref, dns_ref, pools_ref, w1_ref, w2_ref, h_ref, *rest, sample_row0):
    hs_ref = rest[0] if len(rest) == 2 else None
    o_ref = rest[-1]
    is_last = pl.program_id(0) == pl.num_programs(0) - 1

    def project(dn, pool, h):
        o_ref[...] = h + _dot(dn, w1_ref[...]) + _dot(pool, w2_ref[...])

    @pl.when(jnp.logical_not(is_last))
    def _():
        project(dn_ref[...], pool_ref[...], h_ref[...])

    @pl.when(is_last)
    def _():
        h = h_ref[...] if hs_ref is None else jnp.concatenate([h_ref[:sample_row0, :], hs_ref[...]], axis=0)
        project(jnp.concatenate([dn_ref[:sample_row0, :], dns_ref[...]], axis=0),
                jnp.concatenate([pool_ref[:sample_row0, :], pools_ref[...]], axis=0), h)


def _outproj(dn_p, dn_s, pool_p, pool_s, w_out, h, h_tail, l):
    rows_p, nseq = dn_p.shape[0], dn_s.shape[0]
    m = rows_p + nseq
    tm, tn = _row_tile(m, OUTPROJ_ROW_TILE), OUTPROJ_COL_TILE
    sample_row0 = _tail_row0(rows_p, nseq, tm)
    tail = [] if h_tail is None else [h_tail]
    tail_specs = [] if h_tail is None else [pl.BlockSpec((nseq, tn), lambda i, j: (0, j))]
    return pl.pallas_call(
        functools.partial(_outproj_body, sample_row0=sample_row0),
        grid=(m // tm, D_MODEL // tn),
        in_specs=[
            pl.BlockSpec((tm, DN_WIDTH), lambda i, j: (i, 0)),
            pl.BlockSpec((tm, POOL_WIDTH), lambda i, j: (i, 0)),
            pl.BlockSpec((nseq, DN_WIDTH), lambda i, j: (0, 0)),
            pl.BlockSpec((nseq, POOL_WIDTH), lambda i, j: (0, 0)),
            pl.BlockSpec((None, DN_WIDTH, tn), lambda i, j: (l, 0, j)),
            pl.BlockSpec((None, POOL_WIDTH, tn), lambda i, j: (l, 1, j)),
            pl.BlockSpec((tm, tn), lambda i, j: (i, j)),
        ] + tail_specs,
        out_specs=pl.BlockSpec((tm, tn), lambda i, j: (i, j)),
        out_shape=jax.ShapeDtypeStruct((m, D_MODEL), F32),
        compiler_params=_params(("parallel", "parallel"), 54),
        name="outproj",
    )(dn_p, pool_p, dn_s, pool_s, w_out, w_out, h, *tail)


def _ffn_body(h_ref, nw_ref, wg_ref, wu_ref, wd_ref, nf_ref, *rest, final_norm, sample_row0, n_cast):
    cast_src, rest = rest[:n_cast], rest[n_cast:]
    o_ref, xn_ref = rest[0], rest[-1]
    cast_dst = rest[len(rest) - 1 - n_cast:len(rest) - 1]
    f = pl.program_id(1)
    last = pl.num_programs(1) - 1

    @pl.when(f == 0)
    def _():
        x = h_ref[...]
        xn_ref[...] = _rms(x, nw_ref[...]).astype(BF16)
        o_ref[...] = x

    xn = xn_ref[...]
    act = (_silu(_dot(xn, wg_ref[...])) * _dot(xn, wu_ref[...])).astype(BF16)
    o_ref[...] += _dot(act, wd_ref[...])
    for src, dst in zip(cast_src, cast_dst):
        dst[...] = src[...].astype(BF16)

    if final_norm:
        @pl.when(f == last)
        def _():
            o_ref[...] = _rms(o_ref[...], nf_ref[...])

    if sample_row0 is not None:
        @pl.when((f == last) & (pl.program_id(0) == pl.num_programs(0) - 1))
        def _():
            rest[1][...] = o_ref[sample_row0:, :]


def _ffn_cast_plan(m):
    steps = (m // _row_tile(m, FFN_ROW_TILE)) * (D_FF // FFN_TILE)
    n_units = 2 * D_FF // LANES
    for per_step in range(1, n_units + 1):
        if n_units % per_step == 0 and per_step * steps >= n_units:
            return per_step, n_units // per_step
    return None


def _ffn(h, nw, w_gate_up, w_down, nf, l, final_norm, split_rows=None, next_weights=None):
    m = h.shape[0]
    tm, tf = _row_tile(m, FFN_ROW_TILE), FFN_TILE
    nft = D_FF // tf
    out_spec = pl.BlockSpec((tm, D_MODEL), lambda i, f: (i, 0))
    if split_rows is None:
        out_specs, out_shape, sample_row0 = [out_spec], [jax.ShapeDtypeStruct((m, D_MODEL), F32)], None
    else:
        sample_row0 = split_rows - (m // tm - 1) * tm
        assert 0 < sample_row0 < tm, (m, tm, split_rows)
        out_specs = [out_spec, pl.BlockSpec((m - split_rows, D_MODEL), lambda i, f: (0, 0))]
        out_shape = [jax.ShapeDtypeStruct((split_rows, D_MODEL), F32),
                     jax.ShapeDtypeStruct((m - split_rows, D_MODEL), F32)]
    inputs = [h, nw, w_gate_up, w_gate_up, w_down, nf]
    in_specs = [
        pl.BlockSpec((tm, D_MODEL), lambda i, f: (i, 0)),
        pl.BlockSpec((None, 1, D_MODEL), lambda i, f: (l, 0, 0)),
        pl.BlockSpec((D_MODEL, tf), lambda i, f: (0, f)),
        pl.BlockSpec((D_MODEL, tf), lambda i, f: (0, nft + f)),
        pl.BlockSpec((tf, D_MODEL), lambda i, f: (f, 0)),
        pl.BlockSpec((1, D_MODEL), lambda i, f: (0, 0)),
    ]
    n_cast = 0
    if next_weights is not None:
        per_step, n_blocks = _ffn_cast_plan(m)
        cols, rows = per_step * LANES, per_step * (D_FF // (2 * D_FF // LANES))
        blk = lambda i, f: jnp.minimum(i * nft + f, n_blocks - 1)
        n_cast = 2
        inputs += list(next_weights)
        in_specs += [pl.BlockSpec((None, D_MODEL, cols), lambda i, f: (l + 1, 0, blk(i, f))),
                     pl.BlockSpec((None, rows, D_MODEL), lambda i, f: (l + 1, blk(i, f), 0))]
        out_specs += [pl.BlockSpec((D_MODEL, cols), lambda i, f: (0, blk(i, f))),
                      pl.BlockSpec((rows, D_MODEL), lambda i, f: (blk(i, f), 0))]
        out_shape += [jax.ShapeDtypeStruct((D_MODEL, 2 * D_FF), BF16), jax.ShapeDtypeStruct((D_FF, D_MODEL), BF16)]
    out = pl.pallas_call(
        functools.partial(_ffn_body, final_norm=final_norm, sample_row0=sample_row0, n_cast=n_cast),
        grid=(m // tm, nft),
        in_specs=in_specs,
        out_specs=out_specs,
        out_shape=out_shape,
        scratch_shapes=[pltpu.VMEM((tm, D_MODEL), BF16)],
        compiler_params=_params(("parallel", "arbitrary"), 48),
        name="ffn",
    )(*inputs)
    return out[0] if len(out) == 1 else out


def _lane_vecs(x):
    depth, n = x.shape
    return jnp.zeros((depth, 1, LANES), F32).at[:, 0, :n].set(x)


def kernel(x_prompt, x_sample, state_delta, state_conv, state_pool, norm_mix, w_in, conv_w, a_log, dt_bias,
           dn_norm, w_pool, pool_scale, w_out, norm_ffn, w_gate_up, w_down, norm_final):
    batch, seq, _ = x_prompt.shape
    nseq = x_sample.shape[0]
    depth = w_in.shape[0]
    rows_p = batch * seq
    h, h_tail = x_prompt.reshape(rows_p, D_MODEL), x_sample.reshape(nseq, D_MODEL)

    o1 = 4 * DN_WIDTH
    w_a = w_in.astype(BF16)
    w_b = w_a[:, :, o1 + 2 * DN_HEADS:]
    lane_pad = ((0, 0), (0, 0), (0, LANES - DN_HEADS))
    w_gate = jnp.concatenate([jnp.pad(w_a[:, :, o1:o1 + DN_HEADS], lane_pad),
                              jnp.pad(w_a[:, :, o1 + DN_HEADS:o1 + 2 * DN_HEADS], lane_pad)], axis=-1)
    w_out_b = w_out.astype(BF16)
    w_pool_b = w_pool.astype(BF16)
    alog, dtb = _lane_vecs(a_log), _lane_vecs(dt_bias)
    dnw = dn_norm.reshape(depth, 1, HEAD_DIM)
    scale = pool_scale.reshape(depth, 1, POOL_WIDTH)
    nmix = norm_mix.reshape(depth, 1, D_MODEL)
    nffn = norm_ffn.reshape(depth, 1, D_MODEL)
    nfin = norm_final.reshape(1, D_MODEL)
    conv_state = jnp.transpose(state_conv, (0, 2, 1, 3))
    pool_state = jnp.transpose(state_pool, (0, 2, 1, 3))
    hide_casts = _ffn_cast_plan(rows_p + nseq) is not None
    w_gu_l, w_down_l = w_gate_up[0].astype(BF16), w_down[0].astype(BF16)

    delta_p, conv_p, poolbuf_p = [], [], []
    sample_states = None
    pool_states = None
    for l in range(depth):
        proj, gate = _inproj(h, h_tail, nmix, w_a, w_b, w_gate, l)

        beta, gc = _gates(gate, alog, dtb, l, batch, seq)
        u, wq, kdt, qk, gl = _gdn_prep(proj, beta, gc, conv_w, l, batch, seq)
        dn_p, s_p = _gdn_scan(u, wq, kdt, qk, gl, proj, dnw, l, batch, seq)
        dn_s, *sample_states = _gdn_sample(proj, gate, conv_state, state_delta, conv_w, alog, dtb, dnw,
                                           sample_states, l, rows_p)
        pool_p = _pool_prompt(proj, w_pool_b, scale, l, batch, seq)
        pool_s, pool_states = _pool_sample(proj, pool_state, w_pool_b, scale, pool_states, l, rows_p)

        delta_p.append(s_p)
        conv_p.append(jnp.stack([proj[(b + 1) * seq - (CONV_W - 1):(b + 1) * seq, :3 * DN_WIDTH]
                                 for b in range(batch)]))
        poolbuf_p.append(jnp.stack([proj[(b + 1) * seq - POOL_BUF:(b + 1) * seq, o1:] for b in range(batch)]))

        h, h_tail = _outproj(dn_p, dn_s, pool_p, pool_s, w_out_b, h, h_tail, l), None
        if l < depth - 1:
            if hide_casts:
                h, w_gu_l, w_down_l = _ffn(h, nffn, w_gu_l, w_down_l, nfin, l, False,
                                           next_weights=(w_gate_up, w_down))
            else:
                h = _ffn(h, nffn, w_gu_l, w_down_l, nfin, l, False)
                w_gu_l, w_down_l = w_gate_up[l + 1].astype(BF16), w_down[l + 1].astype(BF16)
        elif rows_p % _row_tile(h.shape[0], FFN_ROW_TILE) == 0:
            h = _ffn(h, nffn, w_gu_l, w_down_l, nfin, l, True)
            y_p, y_s = h[:rows_p], h[rows_p:]
        else:
            y_p, y_s = _ffn(h, nffn, w_gu_l, w_down_l, nfin, l, True, split_rows=rows_p)

    delta_s, conv_s = sample_states
    y_prompt = y_p.reshape(batch, seq, D_MODEL)
    y_sample = y_s.reshape(nseq, 1, D_MODEL)
    return (y_prompt, y_sample, jnp.stack(delta_p), jnp.stack(conv_p), jnp.stack(poolbuf_p),
            delta_s, jnp.transpose(conv_s, (0, 2, 1, 3)), jnp.transpose(pool_states, (0, 2, 1, 3)))
```
